```python
import math
import jax, jax.numpy as jnp
from jax import lax
import numpy as np

D_MODEL = 1024
BATCH = 8
SEQ = 2048
DEPTH = 2
DEC_BATCH = 1
DEC_SEQ = 16384
PAST_LEN = 128

W_HY = 512
HY_ORDER = 2
HY_DIRS = 2
HY_BANDS = 16
HY_EMB = 1 + 2 * HY_BANDS
HY_HID = 64
HY_FAST_DECAY = 0.3
HY_SLOW_DECAY = 1.5
HY_TARGET = 1e-2
W_POOL = 512
POOL_WINDOWS = (2, 4, 8, 16)
N_POOL_GROUPS = 4
POOL_GROUP = W_POOL // N_POOL_GROUPS
W_SC = 512
SHORT_K = 3
N_BRANCH = 3
COL_HY = 3 * W_HY
COL_POOL = W_POOL
COL_SC = 3 * W_SC
COL_GATE = N_BRANCH * D_MODEL
PROJ_COLS = COL_HY + COL_POOL + COL_SC + COL_GATE
N_GROUPS = 4
EXP_PER_GROUP = 4
N_EXPERTS = N_GROUPS * EXP_PER_GROUP
D_EXPERT = 256
TOP_K = 2
EPS = 1e-6

kernel_name = "hybrid_hyena_pool_shortconv_hmoe_encoder"


def rms_norm(x, g):
    x32 = x.astype(jnp.float32)
    y = x32 * lax.rsqrt(jnp.mean(x32 * x32, axis=-1, keepdims=True) + EPS)
    return (y * g.astype(jnp.float32)).astype(x.dtype)


def ada_modulate(x, g, shift, scale):
    return rms_norm(x, g) * (1 + scale[:, None, :]) + shift[:, None, :]


def dwconv3(u, w):
    up = jnp.pad(u, ((0, 0), (1, 1), (0, 0)))
    return up[:, :-2] * w[0] + up[:, 1:-1] * w[1] + up[:, 2:] * w[2]


def hyena_filters(L, w1, b1, w2, b2, w_out, freq):
    f32 = jnp.float32
    t = jnp.linspace(0.0, 1.0, L, dtype=f32)[:, None]
    pos = jnp.arange(L, dtype=f32)[:, None]
    bands = jnp.linspace(1e-4, HY_BANDS - 1, HY_BANDS, dtype=f32)[None, :]
    ang = (2 * math.pi / L) * pos * bands
    z = jnp.concatenate([t, jnp.cos(ang), -jnp.sin(ang)], axis=-1)
    fr = freq.astype(f32)
    h = jnp.sin(fr * (z @ w1.astype(f32) + b1.astype(f32)))
    h = jnp.sin(fr * (h @ w2.astype(f32) + b2.astype(f32)))
    h = (h @ w_out.astype(f32)).reshape(L, HY_DIRS, HY_ORDER, W_HY)
    max_decay = math.log(HY_TARGET) / HY_FAST_DECAY
    min_decay = math.log(HY_TARGET) / HY_SLOW_DECAY
    deltas = jnp.linspace(min_decay, max_decay, W_HY, dtype=f32)
    decay = jnp.exp(-t * jnp.abs(deltas)[None, :])
    h = h * decay[:, None, None, :]
    fwd, bwd = h[:, 0], h[:, 1]
    k = jnp.concatenate([fwd, jnp.zeros((1, HY_ORDER, W_HY), f32), bwd[1:][::-1]], axis=0)
    return k / jnp.sum(jnp.abs(k), axis=0, keepdims=True)


def hyena_mixer(u, conv_w, skip, w1, b1, w2, b2, w_out, freq):
    L = u.shape[1]
    uc = dwconv3(u, conv_w).astype(jnp.float32)
    v, x1, x2 = jnp.split(uc, 3, axis=-1)
    k = hyena_filters(L, w1, b1, w2, b2, w_out, freq)
    K = jnp.fft.rfft(k, axis=0)
    sk = skip.astype(jnp.float32)
    z = v
    for n, gate in enumerate((x1, x2)):
        Z = jnp.fft.rfft(z, n=2 * L, axis=1)
        conv = jnp.fft.irfft(Z * K[None, :, n], n=2 * L, axis=1)[:, :L]
        z = gate * (conv + sk[n] * z)
    return z.astype(u.dtype)


def pool_mixer(u, pool_w, pool_scale):
    B, L, _ = u.shape
    u32 = u.astype(jnp.float32)
    cs = jnp.pad(jnp.cumsum(u32, axis=1), ((0, 0), (1, 0), (0, 0)))
    t = jnp.arange(L)
    outs = []
    for g, win in enumerate(POOL_WINDOWS):
        lo = win // 2
        hi = win - 1 - lo
        start = jnp.clip(t - lo, 0, L)
        end = jnp.clip(t + hi + 1, 0, L)
        sl = slice(g * POOL_GROUP, (g + 1) * POOL_GROUP)
        csg = cs[:, :, sl]
        cnt = (end - start).astype(jnp.float32)[None, :, None]
        outs.append((csg[:, end] - csg[:, start]) / cnt - u32[:, :, sl])
    p = jnp.stack(outs, axis=2)
    y = jnp.einsum('blgc,gcd->blgd', p, pool_w.astype(jnp.float32)).reshape(B, L, W_POOL)
    return (y * pool_scale.astype(jnp.float32)).astype(u.dtype)


def short_conv_mixer(u, conv_w):
    bg, cg, xin = jnp.split(u, 3, axis=-1)
    return bg * dwconv3(cg * xin, conv_w)


def hierarchical_moe(h, router_g, router_e, w1, w3, w2):
    B, L, D = h.shape
    t = h.reshape(-1, D)
    lg = (t @ router_g).astype(jnp.float32)
    pg = jax.nn.softmax(lg, axis=-1)
    g_idx = jnp.argmax(lg, axis=-1)
    g_w = jnp.take_along_axis(pg, g_idx[:, None], axis=1)[:, 0]
    le = (t @ router_e).astype(jnp.float32).reshape(-1, N_GROUPS, EXP_PER_GROUP)
    le_sel = jnp.take_along_axis(le, g_idx[:, None, None], axis=1)[:, 0]
    pe = jax.nn.softmax(le_sel, axis=-1)
    top_v, top_i = lax.top_k(pe, TOP_K)
    top_v = top_v / jnp.sum(top_v, axis=-1, keepdims=True)
    expert_id = g_idx[:, None] * EXP_PER_GROUP + top_i
    comb = jnp.sum(jax.nn.one_hot(expert_id, N_EXPERTS, dtype=jnp.float32)
                   * (g_w[:, None] * top_v)[..., None], axis=1)
    hid = jax.nn.silu(t @ w1) * (t @ w3)
    hid = (hid.reshape(-1, N_EXPERTS, D_EXPERT) * comb[:, :, None].astype(hid.dtype))
    y = hid.reshape(-1, N_EXPERTS * D_EXPERT) @ w2
    return y.reshape(B, L, D)


def encoder_layer(x, c, l, p):
    mod = jax.nn.silu(c) @ p['ada_w'][l] + p['ada_b'][l]
    sh1, sc1, ga1, sh2, sc2, ga2 = jnp.split(mod, 6, axis=-1)
    B, L, D = x.shape
    h = ada_modulate(x, p['norm1_g'][l], sh1, sc1)
    proj = h @ p['w_in'][l]
    u_hy, u_pool, u_sc, u_gate = jnp.split(
        proj, [COL_HY, COL_HY + COL_POOL, COL_HY + COL_POOL + COL_SC], axis=-1)
    ya = hyena_mixer(u_hy, p['hy_conv_w'][l], p['hy_skip'][l], p['hy_w1'][l], p['hy_b1'][l],
                     p['hy_w2'][l], p['hy_b2'][l], p['hy_w_out'][l], p['hy_freq'][l]) @ p['w_br_a'][l]
    yb = pool_mixer(u_pool, p['pool_w'][l], p['pool_scale'][l]) @ p['w_br_b'][l]
    yc = short_conv_mixer(u_sc, p['sc_conv_w'][l]) @ p['w_br_c'][l]
    gates = jax.nn.sigmoid(u_gate).reshape(B, L, N_BRANCH, D)
    merged = gates[:, :, 0] * ya + gates[:, :, 1] * yb + gates[:, :, 2] * yc
    x = x + ga1[:, None, :] * (merged @ p['w_out'][l])
    h2 = ada_modulate(x, p['norm2_g'][l], sh2, sc2)
    y2 = hierarchical_moe(h2, p['router_g'][l], p['router_e'][l], p['moe_w1'][l], p['moe_w3'][l], p['moe_w2'][l])
    return x + ga2[:, None, :] * y2


def trunk(x, c, p, final_g):
    for l in range(DEPTH):
        x = encoder_layer(x, c, l, p)
    return rms_norm(x, final_g)


def setup_inputs(seed: int = 0) -> dict:
    key = jax.random.key(seed)
    ks = iter(jax.random.split(key, 40))
    f32 = jnp.float32

    def nrm(shape, scale):
        return jax.random.normal(next(ks), shape, f32) * scale

    D = D_MODEL
    return {
        'x_prompt': nrm((BATCH, SEQ, D), 1.0),
        'x_sample': nrm((DEC_BATCH, DEC_SEQ, D), 1.0),
        'c_prompt': nrm((BATCH, D), 1.0),
        'c_sample': nrm((DEC_BATCH, D), 1.0),
        'ada_w': nrm((DEPTH, D, 6 * D), 0.5 * D ** -0.5),
        'ada_b': nrm((DEPTH, 6 * D), 0.01),
        'norm1_g': 1.0 + nrm((DEPTH, D), 0.05),
        'norm2_g': 1.0 + nrm((DEPTH, D), 0.05),
        'w_in': nrm((DEPTH, D, PROJ_COLS), D ** -0.5),
        'hy_conv_w': nrm((DEPTH, SHORT_K, COL_HY), SHORT_K ** -0.5),
        'hy_skip': nrm((DEPTH, HY_ORDER, W_HY), 0.5),
        'hy_w1': nrm((DEPTH, HY_EMB, HY_HID), HY_EMB ** -0.5),
        'hy_b1': nrm((DEPTH, HY_HID), 0.1),
        'hy_w2': nrm((DEPTH, HY_HID, HY_HID), HY_HID ** -0.5),
        'hy_b2': nrm((DEPTH, HY_HID), 0.1),
        'hy_w_out': nrm((DEPTH, HY_HID, HY_DIRS * HY_ORDER * W_HY), HY_HID ** -0.5),
        'hy_freq': 1.0 + nrm((DEPTH, HY_HID), 0.1),
        'pool_w': nrm((DEPTH, N_POOL_GROUPS, POOL_GROUP, POOL_GROUP), POOL_GROUP ** -0.5),
        'pool_scale': 1.0 + nrm((DEPTH, W_POOL), 0.1),
        'sc_conv_w': nrm((DEPTH, SHORT_K, W_SC), SHORT_K ** -0.5),
        'w_br_a': nrm((DEPTH, W_HY, D), W_HY ** -0.5),
        'w_br_b': nrm((DEPTH, W_POOL, D), W_POOL ** -0.5),
        'w_br_c': nrm((DEPTH, W_SC, D), W_SC ** -0.5),
        'w_out': nrm((DEPTH, D, D), D ** -0.5),
        'router_g': nrm((DEPTH, D, N_GROUPS), D ** -0.5),
        'router_e': nrm((DEPTH, D, N_EXPERTS), D ** -0.5),
        'moe_w1': nrm((DEPTH, D, N_EXPERTS * D_EXPERT), D ** -0.5),
        'moe_w3': nrm((DEPTH, D, N_EXPERTS * D_EXPERT), D ** -0.5),
        'moe_w2': nrm((DEPTH, N_EXPERTS * D_EXPERT, D), D_EXPERT ** -0.5),
        'final_g': 1.0 + nrm((D,), 0.05),
    }


def reference(x_prompt, x_sample, c_prompt, c_sample, ada_w, ada_b, norm1_g, norm2_g, w_in,
              hy_conv_w, hy_skip, hy_w1, hy_b1, hy_w2, hy_b2, hy_w_out, hy_freq,
              pool_w, pool_scale, sc_conv_w, w_br_a, w_br_b, w_br_c, w_out,
              router_g, router_e, moe_w1, moe_w3, moe_w2, final_g):
    p = {
        'ada_w': ada_w, 'ada_b': ada_b, 'norm1_g': norm1_g, 'norm2_g': norm2_g, 'w_in': w_in,
        'hy_conv_w': hy_conv_w, 'hy_skip': hy_skip, 'hy_w1': hy_w1, 'hy_b1': hy_b1,
        'hy_w2': hy_w2, 'hy_b2': hy_b2, 'hy_w_out': hy_w_out, 'hy_freq': hy_freq,
        'pool_w': pool_w, 'pool_scale': pool_scale, 'sc_conv_w': sc_conv_w,
        'w_br_a': w_br_a, 'w_br_b': w_br_b, 'w_br_c': w_br_c, 'w_out': w_out,
        'router_g': router_g, 'router_e': router_e,
        'moe_w1': moe_w1, 'moe_w3': moe_w3, 'moe_w2': moe_w2,
    }
    y_prompt = trunk(x_prompt, c_prompt, p, final_g)
    y_sample = trunk(x_sample, c_sample, p, final_g)
    return (y_prompt, y_sample)
```

```python
import functools
import math

import jax
import jax.numpy as jnp
from jax import lax
from jax.experimental import pallas as pl
from jax.experimental.pallas import tpu as pltpu

f32 = jnp.float32
bf16 = jnp.bfloat16
HIGHEST = lax.Precision.HIGHEST

D_MODEL = 1024
DEPTH = 2
W_MIX = 512
HY_ORDER = 2
HY_BANDS = 16
HY_HID = 64
HY_FAST_DECAY = 0.3
HY_SLOW_DECAY = 1.5
HY_TARGET = 1e-2
POOL_WINDOWS = (2, 4, 8, 16)
POOL_GROUP = W_MIX // len(POOL_WINDOWS)
COL_HY = 3 * W_MIX
COL_POOL = W_MIX
COL_SC = 3 * W_MIX
COL_GATE = 3 * D_MODEL
OFF_POOL = COL_HY
OFF_SC = COL_HY + COL_POOL
OFF_GATE = COL_HY + COL_POOL + COL_SC
PROJ_COLS = OFF_GATE + COL_GATE
N_GROUPS = 4
EXP_PER_GROUP = 4
N_EXPERTS = N_GROUPS * EXP_PER_GROUP
D_EXPERT = 256
GROUP_HID = EXP_PER_GROUP * D_EXPERT
EPS = 1e-6

HALO = 8
DFT_N2 = 256
DFT_ROWS = 128
LANES = 128
VMEM_LIMIT = 56 * 1024 * 1024


def _cparams(sem):
    return pltpu.CompilerParams(dimension_semantics=sem, vmem_limit_bytes=VMEM_LIMIT)


def _const_spec(shape):
    nd = len(shape)
    return pl.BlockSpec(shape, lambda *_: (0,) * nd, pipeline_mode=pl.Buffered(1))


def _mod_body(c_ref, w_ref, b_ref, o_ref):
    c = c_ref[...]
    s = c * jax.nn.sigmoid(c)
    o_ref[...] = jnp.dot(s, w_ref[...], preferred_element_type=f32, precision=HIGHEST) + b_ref[...]


def _mod_call(c_all, ada_w, ada_b):
    rows = c_all.shape[0]
    tn = 1536
    return pl.pallas_call(
        _mod_body,
        grid=(6 * D_MODEL // tn,),
        in_specs=[pl.BlockSpec((rows, D_MODEL), lambda j: (0, 0)),
                  pl.BlockSpec((D_MODEL, tn), lambda j: (0, j)),
                  pl.BlockSpec((1, tn), lambda j: (0, j))],
        out_specs=pl.BlockSpec((rows, tn), lambda j: (0, j)),
        out_shape=jax.ShapeDtypeStruct((rows, 6 * D_MODEL), f32),
        compiler_params=_cparams(("arbitrary",)),
        name="mod",
    )(c_all, ada_w, ada_b)


def _proj_body(xm_ref, xp_ref, xn_ref, sh_ref, sc_ref, g_ref, win_ref, hyw_ref, pw_ref, ps_ref, scw_ref,
               wbb_ref, wbc_ref, v_ref, x1_ref, x2_ref, g0_ref, rest_ref, *, tm, seq_len):
    i = pl.program_id(1)
    nt = pl.num_programs(1)
    rt = tm + 2 * HALO
    ctr = slice(HALO, HALO + tm)

    def modulated(x):
        ms = jnp.mean(x * x, axis=-1, keepdims=True)
        h = x * lax.rsqrt(ms + EPS) * g_ref[...]
        return h * (1.0 + sc_ref[...]) + sh_ref[...]

    hp = jnp.where(i > 0, modulated(xp_ref[...]), 0.0)
    hn = jnp.where(i < nt - 1, modulated(xn_ref[...]), 0.0)
    hc = modulated(xm_ref[...])
    hb = jnp.concatenate([hp, hc, hn], axis=0).astype(bf16)
    hcb = hc.astype(bf16)

    def down(a, s):
        return pltpu.roll(a, s, 0)

    def up(a, s):
        return pltpu.roll(a, rt - s, 0)

    u = jnp.dot(hb, win_ref[:, 0:COL_HY], preferred_element_type=f32)
    w = hyw_ref[...]
    uc = (down(u, 1) * w[0:1] + u * w[1:2] + up(u, 1) * w[2:3])[ctr]
    v_ref[...] = uc[:, 0:W_MIX]
    x1_ref[...] = uc[:, W_MIX:2 * W_MIX]
    x2_ref[...] = uc[:, 2 * W_MIX:3 * W_MIX]

    q = jnp.dot(hb, win_ref[:, OFF_POOL:OFF_POOL + COL_POOL], preferred_element_type=f32)
    s2 = q + down(q, 1)
    s4 = s2 + down(s2, 2)
    s8 = s4 + down(s4, 4)
    s16 = s8 + down(s8, 8)
    tpos = i * tm + lax.broadcasted_iota(jnp.int32, (tm, 1), 0)
    pooled = []
    for g, (win, ssum) in enumerate(zip(POOL_WINDOWS, (s2, s4, s8, s16))):
        lo = win // 2
        hi = win - 1 - lo
        lanes = slice(g * POOL_GROUP, (g + 1) * POOL_GROUP)
        ws = ssum[:, lanes]
        if hi > 0:
            ws = up(ws, hi)
        cnt = (jnp.minimum(tpos + hi + 1, seq_len) - jnp.maximum(tpos - lo, 0)).astype(f32)
        p = ws[ctr] / cnt - q[ctr, lanes]
        pooled.append(jnp.dot(p.astype(bf16), pw_ref[g], preferred_element_type=f32))
    yb_in = jnp.concatenate(pooled, axis=1) * ps_ref[...]
    yb = jnp.dot(yb_in.astype(bf16), wbb_ref[...], preferred_element_type=f32)

    us = jnp.dot(hb, win_ref[:, OFF_SC:OFF_SC + COL_SC], preferred_element_type=f32)
    cx = us[:, W_MIX:2 * W_MIX] * us[:, 2 * W_MIX:3 * W_MIX]
    sw = scw_ref[...]
    dw = down(cx, 1) * sw[0:1] + cx * sw[1:2] + up(cx, 1) * sw[2:3]
    sc_out = (us[:, 0:W_MIX] * dw)[ctr]
    yc = jnp.dot(sc_out.astype(bf16), wbc_ref[...], preferred_element_type=f32)

    gt = jax.nn.sigmoid(jnp.dot(hcb, win_ref[:, OFF_GATE:PROJ_COLS], preferred_element_type=f32))
    g0_ref[...] = gt[:, 0:D_MODEL]
    rest_ref[...] = gt[:, D_MODEL:2 * D_MODEL] * yb + gt[:, 2 * D_MODEL:3 * D_MODEL] * yc


def _proj_call(x, sh, sc, g1, win, hyw, pw, ps, scw, wbb, wbc, tm):
    B, L, D = x.shape
    nt = L // tm
    hb = tm // HALO
    row = lambda b, i: (b, i, 0)
    vec = lambda b, i: (b, 0, 0)
    out_w = jax.ShapeDtypeStruct((B, L, W_MIX), f32)
    out_d = jax.ShapeDtypeStruct((B, L, D), f32)
    return pl.pallas_call(
        functools.partial(_proj_body, tm=tm, seq_len=L),
        grid=(B, nt),
        in_specs=[
            pl.BlockSpec((None, tm, D), row),
            pl.BlockSpec((None, HALO, D), lambda b, i: (b, jnp.maximum(i * hb - 1, 0), 0)),
            pl.BlockSpec((None, HALO, D), lambda b, i: (b, jnp.minimum((i + 1) * hb, L // HALO - 1), 0)),
            pl.BlockSpec((None, 1, D), vec),
            pl.BlockSpec((None, 1, D), vec),
            _const_spec((1, D)),
            _const_spec((D, PROJ_COLS)),
            _const_spec((3, COL_HY)),
            _const_spec((len(POOL_WINDOWS), POOL_GROUP, POOL_GROUP)),
            _const_spec((1, W_MIX)),
            _const_spec((3, W_MIX)),
            _const_spec((W_MIX, D)),
            _const_spec((W_MIX, D)),
        ],
        out_specs=[pl.BlockSpec((None, tm, W_MIX), row)] * 3 + [pl.BlockSpec((None, tm, D), row)] * 2,
        out_shape=[out_w, out_w, out_w, out_d, out_d],
        compiler_params=_cparams(("parallel", "arbitrary")),
        name="proj",
    )(x, x, x, sh, sc, g1, win, hyw, pw, ps, scw, wbb, wbc)


def _filter_body(bands_ref, w1_ref, b1_ref, w2_ref, b2_ref, fr_ref, wo_ref, dl_ref, k_ref, asum_ref, *, tr, seq_len):
    i = pl.program_id(0)
    L = seq_len
    r = i * tr + lax.broadcasted_iota(jnp.int32, (tr, 1), 0)
    lag = jnp.where(r < L, r, 2 * L - r).astype(f32)
    t = lag / (L - 1)
    lane = lax.broadcasted_iota(jnp.int32, (tr, LANES), 1)
    ang = (2 * math.pi / L) * lag * bands_ref[...]
    z = jnp.where(lane == 0, t,
                  jnp.where(lane <= HY_BANDS, jnp.cos(ang),
                            jnp.where(lane <= 2 * HY_BANDS, -jnp.sin(ang), 0.0)))
    fr = fr_ref[...]
    h = jnp.sin(fr * (jnp.dot(z, w1_ref[...], preferred_element_type=f32, precision=HIGHEST) + b1_ref[...]))
    h = jnp.sin(fr * (jnp.dot(h, w2_ref[...], preferred_element_type=f32, precision=HIGHEST) + b2_ref[...]))
    ho = jnp.dot(h, wo_ref[...], preferred_element_type=f32, precision=HIGHEST)
    k = jnp.where(r == L, 0.0, ho * jnp.exp(-t * dl_ref[...]))
    k_ref[...] = k

    @pl.when(i == 0)
    def _():
        asum_ref[...] = jnp.zeros_like(asum_ref)

    asum_ref[...] += jnp.sum(jnp.abs(k), axis=0, keepdims=True)


def _filter_call(L, bands_row, w1p, b1p, w2p, b2p, frp, wop, dl_row):
    tr = 512
    nt = 2 * L // tr
    cw = HY_ORDER * W_MIX
    return pl.pallas_call(
        functools.partial(_filter_body, tr=tr, seq_len=L),
        grid=(nt,),
        in_specs=[_const_spec((1, LANES)), _const_spec((LANES, LANES)), _const_spec((1, LANES)),
                  _const_spec((LANES, LANES)), _const_spec((1, LANES)), _const_spec((1, LANES)),
                  pl.BlockSpec((LANES, cw), lambda i: (0, i // (nt // 2))),
                  _const_spec((1, cw))],
        out_specs=[pl.BlockSpec((tr, cw), lambda i: (i, 0)), pl.BlockSpec((1, cw), lambda i: (0, 0))],
        out_shape=[jax.ShapeDtypeStruct((2 * L, cw), f32), jax.ShapeDtypeStruct((1, cw), f32)],
        compiler_params=_cparams(("arbitrary",)),
        name="filt",
    )(bands_row, w1p, b1p, w2p, b2p, frp, wop, dl_row)


def _dft_tables(L):
    n = 2 * L
    n1c = n // DFT_N2
    k2 = jnp.arange(DFT_N2, dtype=jnp.int32)
    tw_ang = ((jnp.arange(n1c, dtype=jnp.int32)[:, None] * k2[None, :]) % n).astype(f32) * (2 * math.pi / n)
    f_ang = ((k2[:, None] * k2[None, :]) % DFT_N2).astype(f32) * (2 * math.pi / DFT_N2)
    twr, twi = jnp.cos(tw_ang)[:, :, None], -jnp.sin(tw_ang)[:, :, None]
    fr, fi = jnp.cos(f_ang)[None], -jnp.sin(f_ang)[None]
    gr = twr * fr - twi * fi
    gi = twr * fi + twi * fr
    g_fwd = jnp.concatenate([gr, gi], axis=1)
    g_inv = jnp.transpose(g_fwd[:, :, :DFT_ROWS], (0, 2, 1)) * (1.0 / n)
    a = jnp.arange(n1c, dtype=jnp.int32)
    s_ang = ((a[:, None] * a[None, :]) % n1c).astype(f32) * (2 * math.pi / n1c)
    eye = jnp.eye(DFT_ROWS // n1c, dtype=f32)
    sr = jnp.kron(eye, jnp.cos(s_ang))
    si = jnp.kron(eye, -jnp.sin(s_ang))
    m_fwd = jnp.block([[sr, -si], [si, sr]])
    m_inv = jnp.block([[sr, si], [-si, sr]])
    return dict(g_fwd=g_fwd.astype(bf16), g_inv=g_inv.astype(bf16), m_fwd=m_fwd.astype(bf16),
                m_inv=m_inv.astype(bf16), n1=n1c)


def _fft1_body(x_ref, s_ref, g_ref, o_ref):
    x = (x_ref[...] * (1.0 / s_ref[...])).astype(bf16)
    b = jnp.dot(g_ref[...], x, preferred_element_type=f32)
    o_ref[...] = b.reshape(o_ref.shape).astype(bf16)


def _fft1_call(xq, scale_row, g_fwd, rows):
    B, n1c, _, C = xq.shape
    return pl.pallas_call(
        _fft1_body,
        grid=(B, n1c),
        in_specs=[pl.BlockSpec((None, None, rows, C), lambda b, j: (b, j, 0, 0)),
                  _const_spec((1, C)),
                  pl.BlockSpec((None, 2 * DFT_N2, rows), lambda b, j: (j, 0, 0))],
        out_specs=pl.BlockSpec((None, None, 2, DFT_N2, C), lambda b, j: (b, j, 0, 0, 0)),
        out_shape=jax.ShapeDtypeStruct((B, n1c, 2, DFT_N2, C), bf16),
        compiler_params=_cparams(("parallel", "arbitrary")),
        name="fft1",
    )(xq, scale_row, g_fwd)


def _fft2_body(b_ref, k_ref, mf_ref, mi_ref, o_ref):
    c = b_ref.shape[-1]
    xs = jnp.dot(mf_ref[...], b_ref[...].reshape(2 * DFT_ROWS, c), preferred_element_type=f32)
    xr, xi = xs[:DFT_ROWS], xs[DFT_ROWS:]
    kr, ki = k_ref[0], k_ref[1]
    ys = jnp.concatenate([xr * kr - xi * ki, xr * ki + xi * kr], axis=0).astype(bf16)
    cs = jnp.dot(mi_ref[...], ys, preferred_element_type=f32)
    o_ref[...] = cs.reshape(o_ref.shape).astype(bf16)


def _fft2_call(bs, ksp, order, m_fwd, m_inv):
    B, _, n, C = bs.shape
    blk = pl.BlockSpec((None, 2, DFT_ROWS, C), lambda b, j: (b, 0, j, 0))
    return pl.pallas_call(
        _fft2_body,
        grid=(B, n // DFT_ROWS),
        in_specs=[blk,
                  pl.BlockSpec((2, DFT_ROWS, C), lambda b, j: (0, j, order)),
                  _const_spec((2 * DFT_ROWS, 2 * DFT_ROWS)),
                  _const_spec((2 * DFT_ROWS, 2 * DFT_ROWS))],
        out_specs=blk,
        out_shape=jax.ShapeDtypeStruct(bs.shape, bf16),
        compiler_params=_cparams(("parallel", "arbitrary")),
        name="fft2",
    )(bs, ksp, m_fwd, m_inv)


def _fft2_filter_body(b_ref, mf_ref, o_ref):
    c = b_ref.shape[-1]
    xs = jnp.dot(mf_ref[...], b_ref[...].reshape(2 * DFT_ROWS, c), preferred_element_type=f32)
    o_ref[...] = xs.reshape(o_ref.shape)


def _fft2_filter_call(bs, m_fwd):
    _, n, C = bs.shape
    blk = pl.BlockSpec((2, DFT_ROWS, C), lambda j: (0, j, 0))
    return pl.pallas_call(
        _fft2_filter_body,
        grid=(n // DFT_ROWS,),
        in_specs=[blk, _const_spec((2 * DFT_ROWS, 2 * DFT_ROWS))],
        out_specs=blk,
        out_shape=jax.ShapeDtypeStruct(bs.shape, f32),
        compiler_params=_cparams(("arbitrary",)),
        name="fft2_filter",
    )(bs, m_fwd)


def _fft3_body(c_ref, gi_ref, gate_ref, prev_ref, sk_ref, *rest, fuse_next):
    c = c_ref.shape[-1]
    y = jnp.dot(gi_ref[...], c_ref[...].reshape(2 * DFT_N2, c), preferred_element_type=f32)
    z = gate_ref[...] * (y + sk_ref[...] * prev_ref[...])
    if fuse_next:
        gf_ref, z_ref, b_ref = rest
        z_ref[...] = z
        b = jnp.dot(gf_ref[...], z.astype(bf16), preferred_element_type=f32)
        b_ref[...] = b.reshape(b_ref.shape).astype(bf16)
    else:
        (z_ref,) = rest
        z_ref[...] = z


def _fft3_call(cs, g_inv, gate, prev, sk_row, g_fwd=None):
    B, n1c, _, _, C = cs.shape
    tblk = pl.BlockSpec((None, None, DFT_ROWS, C), lambda b, j: (b, j, 0, 0))
    sblk = pl.BlockSpec((None, None, 2, DFT_N2, C), lambda b, j: (b, j, 0, 0, 0))
    in_specs = [sblk, pl.BlockSpec((None, DFT_ROWS, 2 * DFT_N2), lambda b, j: (j, 0, 0)), tblk, tblk,
                _const_spec((1, C))]
    args = [cs, g_inv, gate, prev, sk_row]
    out_specs = [tblk]
    out_shape = [jax.ShapeDtypeStruct(gate.shape, f32)]
    fuse_next = g_fwd is not None
    if fuse_next:
        in_specs.append(pl.BlockSpec((None, 2 * DFT_N2, DFT_ROWS), lambda b, j: (j, 0, 0)))
        args.append(g_fwd)
        out_specs.append(sblk)
        out_shape.append(jax.ShapeDtypeStruct(cs.shape, bf16))
    return pl.pallas_call(
        functools.partial(_fft3_body, fuse_next=fuse_next),
        grid=(B, n1c),
        in_specs=in_specs,
        out_specs=out_specs,
        out_shape=out_shape,
        compiler_params=_cparams(("parallel", "arbitrary")),
        name="fft3_next" if fuse_next else "fft3",
    )(*args)


def _to_stage2(bs):
    B, n1c, _, _, C = bs.shape
    return jnp.transpose(bs, (0, 2, 3, 1, 4)).reshape(B, 2, DFT_N2 * n1c, C)


def _from_stage2(cs, n1c):
    B, _, n, C = cs.shape
    return jnp.transpose(cs.reshape(B, 2, DFT_N2, n1c, C), (0, 3, 1, 2, 4))


def _time_split(a, n1c):
    B, L, C = a.shape
    return jnp.transpose(a.reshape(B, L // n1c, n1c, C), (0, 2, 1, 3))


def _time_merge(a):
    B, n1c, n2c, C = a.shape
    return jnp.transpose(a, (0, 2, 1, 3)).reshape(B, n1c * n2c, C)


def _filter_spectrum(k, asum, tabs):
    n1c = tabs["n1"]
    kq = _time_split(k[None], n1c)
    bs = _fft1_call(kq, asum, tabs["g_fwd"], DFT_N2)
    return _fft2_filter_call(_to_stage2(bs)[0], tabs["m_fwd"])


def _hyena_conv(v, x1, x2, ksp, skip, tabs):
    n1c = tabs["n1"]
    C = v.shape[-1]
    vq, x1q, x2q = (_time_split(a, n1c) for a in (v, x1, x2))
    ones = jnp.ones((1, C), f32)
    bs = _fft1_call(vq, ones, tabs["g_fwd"], DFT_ROWS)
    cs = _fft2_call(_to_stage2(bs), ksp, 0, tabs["m_fwd"], tabs["m_inv"])
    z1q, bs = _fft3_call(_from_stage2(cs, n1c), tabs["g_inv"], x1q, vq, skip[0:1], tabs["g_fwd"])
    cs = _fft2_call(_to_stage2(bs), ksp, 1, tabs["m_fwd"], tabs["m_inv"])
    (z2q,) = _fft3_call(_from_stage2(cs, n1c), tabs["g_inv"], x2q, z1q, skip[1:2])
    return _time_merge(z2q)


def _route(r):
    lane = lax.broadcasted_iota(jnp.int32, r.shape, 1)
    ninf = jnp.float32(-jnp.inf)
    big = jnp.int32(1 << 20)
    is_g = lane < N_GROUPS
    gmax = jnp.max(jnp.where(is_g, r, ninf), axis=-1, keepdims=True)
    gidx = jnp.min(jnp.where(jnp.logical_and(is_g, r == gmax), lane, big), axis=-1, keepdims=True)
    gw = 1.0 / jnp.sum(jnp.where(is_g, jnp.exp(r - gmax), 0.0), axis=-1, keepdims=True)
    e_lane = lane - N_GROUPS
    sel = jnp.logical_and(jnp.logical_and(e_lane >= 0, e_lane < N_EXPERTS), (e_lane >> 2) == gidx)
    le = jnp.where(sel, r, ninf)
    m1 = jnp.max(le, axis=-1, keepdims=True)
    i1 = jnp.min(jnp.where(le == m1, lane, big), axis=-1, keepdims=True)
    le2 = jnp.where(lane == i1, ninf, le)
    m2 = jnp.max(le2, axis=-1, keepdims=True)
    i2 = jnp.min(jnp.where(le2 == m2, lane, big), axis=-1, keepdims=True)
    e2 = jnp.exp(m2 - m1)
    den = 1.0 + e2
    return jnp.where(lane == i1, gw / den, jnp.where(lane == i2, gw * e2 / den, 0.0))


def _mix_moe_body(x_ref, z_ref, g0_ref, rest_ref, ga1_ref, sh2_ref, sc2_ref, ga2_ref, n2g_ref, wba_ref, wout_ref,
                  wr_ref, ex_ref, w1_ref, w3_ref, w2_ref, fg_ref, o_ref, xo_s, h2_s, comb_s, acc_s, *, final_norm):
    g = pl.program_id(2)

    @pl.when(g == 0)
    def _():
        ya = jnp.dot(z_ref[...].astype(bf16), wba_ref[...], preferred_element_type=f32)
        merged = g0_ref[...] * ya + rest_ref[...]
        xo = x_ref[...] + ga1_ref[...] * jnp.dot(merged.astype(bf16), wout_ref[...], preferred_element_type=f32)
        xo_s[...] = xo
        ms = jnp.mean(xo * xo, axis=-1, keepdims=True)
        h2 = xo * lax.rsqrt(ms + EPS) * n2g_ref[...]
        h2 = h2 * (1.0 + sc2_ref[...]) + sh2_ref[...]
        h2_s[...] = h2.astype(bf16)
        comb_s[...] = _route(jnp.dot(h2, wr_ref[...], preferred_element_type=f32, precision=HIGHEST))
        acc_s[...] = jnp.zeros_like(acc_s)

    h2b = h2_s[...]
    a = jnp.dot(h2b, w1_ref[...], preferred_element_type=f32)
    b = jnp.dot(h2b, w3_ref[...], preferred_element_type=f32)
    cw = jnp.dot(comb_s[...], ex_ref[...], preferred_element_type=f32, precision=HIGHEST)
    hid = (a * jax.nn.sigmoid(a) * b * cw).astype(bf16)
    acc_s[...] += jnp.dot(hid, w2_ref[...], preferred_element_type=f32)

    @pl.when(g == pl.num_programs(2) - 1)
    def _():
        y = xo_s[...] + ga2_ref[...] * acc_s[...]
        if final_norm:
            ms = jnp.mean(y * y, axis=-1, keepdims=True)
            y = y * lax.rsqrt(ms + EPS) * fg_ref[...]
        o_ref[...] = y


def _mix_moe_call(x, z, g0, rest, ga1, sh2, sc2, ga2, n2g, wba, wout, wr, ex, w1, w3, w2, fg, tm, final_norm):
    B, L, D = x.shape
    row = lambda b, i, g: (b, i, 0)
    vec = lambda b, i, g: (b, 0, 0)
    vspec = pl.BlockSpec((None, 1, D), vec)
    return pl.pallas_call(
        functools.partial(_mix_moe_body, final_norm=final_norm),
        grid=(B, L // tm, N_GROUPS),
        in_specs=[
            pl.BlockSpec((None, tm, D), row),
            pl.BlockSpec((None, tm, W_MIX), row),
            pl.BlockSpec((None, tm, D), row),
            pl.BlockSpec((None, tm, D), row),
            vspec, vspec, vspec, vspec,
            _const_spec((1, D)),
            _const_spec((W_MIX, D)),
            _const_spec((D, D)),
            _const_spec((D, LANES)),
            pl.BlockSpec((None, LANES, GROUP_HID), lambda b, i, g: (g, 0, 0)),
            pl.BlockSpec((D, GROUP_HID), lambda b, i, g: (0, g)),
            pl.BlockSpec((D, GROUP_HID), lambda b, i, g: (0, g)),
            pl.BlockSpec((GROUP_HID, D), lambda b, i, g: (g, 0)),
            _const_spec((1, D)),
        ],
        out_specs=pl.BlockSpec((None, tm, D), row),
        out_shape=jax.ShapeDtypeStruct((B, L, D), f32),
        scratch_shapes=[pltpu.VMEM((tm, D), f32), pltpu.VMEM((tm, D), bf16), pltpu.VMEM((tm, LANES), f32),
                        pltpu.VMEM((tm, D), f32)],
        compiler_params=_cparams(("parallel", "arbitrary", "arbitrary")),
        name="mix_moe",
    )(x, z, g0, rest, ga1, sh2, sc2, ga2, n2g, wba, wout, wr, ex, w1, w3, w2, fg)


def _pad_to(a, shape):
    return jnp.pad(a, [(0, s - d) for d, s in zip(a.shape, shape)])


def _prep_layer(l, p):
    bands = jnp.linspace(1e-4, HY_BANDS - 1, HY_BANDS, dtype=f32)
    bands_row = jnp.zeros((1, LANES), f32).at[0, 1:1 + HY_BANDS].set(bands).at[0, 1 + HY_BANDS:1 + 2 * HY_BANDS].set(bands)
    max_decay = math.log(HY_TARGET) / HY_FAST_DECAY
    min_decay = math.log(HY_TARGET) / HY_SLOW_DECAY
    deltas = jnp.abs(jnp.linspace(min_decay, max_decay, W_MIX, dtype=f32))
    router = jnp.concatenate([p["router_g"][l], p["router_e"][l]], axis=1)
    lanes = jnp.arange(LANES)[None, :, None]
    cols = jnp.arange(GROUP_HID)[None, None, :]
    grp = jnp.arange(N_GROUPS)[:, None, None]
    expand = (lanes == N_GROUPS + EXP_PER_GROUP * grp + cols // D_EXPERT).astype(f32)
    return dict(
        norm1_g=p["norm1_g"][l][None], norm2_g=p["norm2_g"][l][None],
        w_in=p["w_in"][l].astype(bf16), hy_conv_w=p["hy_conv_w"][l], hy_skip=p["hy_skip"][l],
        pool_w=p["pool_w"][l].astype(bf16), pool_scale=p["pool_scale"][l][None], sc_conv_w=p["sc_conv_w"][l],
        w_br_a=p["w_br_a"][l].astype(bf16), w_br_b=p["w_br_b"][l].astype(bf16), w_br_c=p["w_br_c"][l].astype(bf16),
        w_out=p["w_out"][l].astype(bf16),
        router=_pad_to(router, (D_MODEL, LANES)), expand=expand,
        moe_w1=p["moe_w1"][l].astype(bf16), moe_w3=p["moe_w3"][l].astype(bf16), moe_w2=p["moe_w2"][l].astype(bf16),
        filt=(bands_row,
              _pad_to(p["hy_w1"][l], (LANES, LANES)), _pad_to(p["hy_b1"][l][None], (1, LANES)),
              _pad_to(p["hy_w2"][l], (LANES, LANES)), _pad_to(p["hy_b2"][l][None], (1, LANES)),
              _pad_to(p["hy_freq"][l][None], (1, LANES)),
              _pad_to(p["hy_w_out"][l], (LANES, 2 * HY_ORDER * W_MIX)),
              jnp.concatenate([deltas, deltas])[None]),
    )


def _tile(L, want):
    return want if L % want == 0 else L


def _encoder_layer(x, mod, lp, ksp, tabs, final_g, final_norm):
    B, L, D = x.shape
    sh1, sc1, ga1, sh2, sc2, ga2 = (m[:, None, :] for m in jnp.split(mod, 6, axis=-1))
    v, x1, x2, g0, rest = _proj_call(x, sh1, sc1, lp["norm1_g"], lp["w_in"], lp["hy_conv_w"], lp["pool_w"],
                                     lp["pool_scale"], lp["sc_conv_w"], lp["w_br_b"], lp["w_br_c"], _tile(L, 512))
    z = _hyena_conv(v, x1, x2, ksp, lp["hy_skip"], tabs)
    return _mix_moe_call(x, z, g0, rest, ga1, sh2, sc2, ga2, lp["norm2_g"], lp["w_br_a"], lp["w_out"], lp["router"],
                         lp["expand"], lp["moe_w1"], lp["moe_w3"], lp["moe_w2"], final_g, _tile(L, 512), final_norm)


def _forward(xs, cs, p, final_g):
    depth = p["w_in"].shape[0]
    nb = [c.shape[0] for c in cs]
    rows = -(-sum(nb) // 8) * 8
    c_all = _pad_to(jnp.concatenate(cs, axis=0), (rows, D_MODEL))
    lens = sorted({x.shape[1] for x in xs})
    tabs = {L: _dft_tables(L) for L in lens}
    fg = final_g[None]
    for l in range(depth):
        lp = _prep_layer(l, p)
        mod = _mod_call(c_all, p["ada_w"][l], p["ada_b"][l][None])
        ksp = {}
        for L in lens:
            k, asum = _filter_call(L, *lp["filt"])
            ksp[L] = _filter_spectrum(k, asum, tabs[L])
        off = 0
        out = []
        for x, n in zip(xs, nb):
            L = x.shape[1]
            out.append(_encoder_layer(x, mod[off:off + n], lp, ksp[L], tabs[L], fg, l == depth - 1))
            off += n
        xs = out
    return xs


def kernel(x_prompt, x_sample, c_prompt, c_sample, ada_w, ada_b, norm1_g, norm2_g, w_in, hy_conv_w, hy_skip, hy_w1, hy_b1, hy_w2, hy_b2, hy_w_out, hy_freq, pool_w, pool_scale, sc_conv_w, w_br_a, w_br_b, w_br_c, w_out, router_g, router_e, moe_w1, moe_w3, moe_w2, final_g):
    p = dict(ada_w=ada_w, ada_b=ada_b, norm1_g=norm1_g, norm2_g=norm2_g, w_in=w_in, hy_conv_w=hy_conv_w,
             hy_skip=hy_skip, hy_w1=hy_w1, hy_b1=hy_b1, hy_w2=hy_w2, hy_b2=hy_b2, hy_w_out=hy_w_out, hy_freq=hy_freq,
             pool_w=pool_w, pool_scale=pool_scale, sc_conv_w=sc_conv_w, w_br_a=w_br_a, w_br_b=w_br_b, w_br_c=w_br_c,
             w_out=w_out, router_g=router_g, router_e=router_e, moe_w1=moe_w1, moe_w3=moe_w3, moe_w2=moe_w2)
    y_prompt, y_sample = _forward([x_prompt, x_sample], [c_prompt, c_sample], p, final_g)
    return (y_prompt, y_sample)
```

```python
import functools
import math

import jax
import jax.numpy as jnp
from jax import lax
from jax.experimental import pallas as pl
from jax.experimental.pallas import tpu as pltpu

f32 = jnp.float32
bf16 = jnp.bfloat16
HIGHEST = lax.Precision.HIGHEST

D_MODEL = 1024
DEPTH = 2
W_MIX = 512
HY_ORDER = 2
HY_BANDS = 16
HY_HID = 64
HY_FAST_DECAY = 0.3
HY_SLOW_DECAY = 1.5
HY_TARGET = 1e-2
POOL_WINDOWS = (2, 4, 8, 16)
POOL_GROUP = W_MIX // len(POOL_WINDOWS)
COL_HY = 3 * W_MIX
COL_POOL = W_MIX
COL_SC = 3 * W_MIX
COL_GATE = 3 * D_MODEL
OFF_POOL = COL_HY
OFF_SC = COL_HY + COL_POOL
OFF_GATE = COL_HY + COL_POOL + COL_SC
PROJ_COLS = OFF_GATE + COL_GATE
N_GROUPS = 4
EXP_PER_GROUP = 4
N_EXPERTS = N_GROUPS * EXP_PER_GROUP
D_EXPERT = 256
GROUP_HID = EXP_PER_GROUP * D_EXPERT
EPS = 1e-6

HALO = 8
DFT_N2 = 256
DFT_ROWS = 128
DFT_RPB = 16
DFT_SLAB = 256
DFT_STAGE2_ROWS = 512
MOE_TILE = 512
MOE_CHUNK = 128
MOE_SUB = 4
LANES = 128
VMEM_LIMIT = 56 * 1024 * 1024


def _cparams(sem):
    return pltpu.CompilerParams(dimension_semantics=sem, vmem_limit_bytes=VMEM_LIMIT)


def _const_spec(shape):
    nd = len(shape)
    return pl.BlockSpec(shape, lambda *_: (0,) * nd, pipeline_mode=pl.Buffered(1))


def _mod_body(c_ref, w_ref, b_ref, o_ref):
    c = c_ref[...]
    s = c * jax.nn.sigmoid(c)
    o_ref[...] = jnp.dot(s, w_ref[...], preferred_element_type=f32, precision=HIGHEST) + b_ref[...]


def _mod_call(c_all, ada_w, ada_b):
    rows = c_all.shape[0]
    tn = 1536
    return pl.pallas_call(
        _mod_body,
        grid=(6 * D_MODEL // tn,),
        in_specs=[pl.BlockSpec((rows, D_MODEL), lambda j: (0, 0)),
                  pl.BlockSpec((D_MODEL, tn), lambda j: (0, j)),
                  pl.BlockSpec((1, tn), lambda j: (0, j))],
        out_specs=pl.BlockSpec((rows, tn), lambda j: (0, j)),
        out_shape=jax.ShapeDtypeStruct((rows, 6 * D_MODEL), f32),
        compiler_params=_cparams(("arbitrary",)),
        name="mod",
    )(c_all, ada_w, ada_b)


def _proj_body(xm_ref, xp_ref, xn_ref, sh_ref, sc_ref, g_ref, win_ref, hyw_ref, pw_ref, ps_ref, scw_ref,
               wbb_ref, wbc_ref, v_ref, x1_ref, x2_ref, g0_ref, rest_ref, *, tm, seq_len):
    i = pl.program_id(1)
    nt = pl.num_programs(1)
    rt = tm + 2 * HALO
    ctr = slice(HALO, HALO + tm)

    def modulated(x):
        ms = jnp.mean(x * x, axis=-1, keepdims=True)
        h = x * lax.rsqrt(ms + EPS) * g_ref[...]
        return h * (1.0 + sc_ref[...]) + sh_ref[...]

    hp = jnp.where(i > 0, modulated(xp_ref[...]), 0.0)
    hn = jnp.where(i < nt - 1, modulated(xn_ref[...]), 0.0)
    hc = modulated(xm_ref[...])
    hb = jnp.concatenate([hp, hc, hn], axis=0).astype(bf16)
    hcb = hc.astype(bf16)

    def down(a, s):
        return pltpu.roll(a, s, 0)

    def up(a, s):
        return pltpu.roll(a, rt - s, 0)

    u = jnp.dot(hb, win_ref[:, 0:COL_HY], preferred_element_type=f32)
    w = hyw_ref[...]
    uc = (down(u, 1) * w[0:1] + u * w[1:2] + up(u, 1) * w[2:3])[ctr]
    v_ref[...] = uc[:, 0:W_MIX]
    x1_ref[...] = uc[:, W_MIX:2 * W_MIX]
    x2_ref[...] = uc[:, 2 * W_MIX:3 * W_MIX]

    q = jnp.dot(hb, win_ref[:, OFF_POOL:OFF_POOL + COL_POOL], preferred_element_type=f32)
    s2 = q + down(q, 1)
    s4 = s2 + down(s2, 2)
    s8 = s4 + down(s4, 4)
    s16 = s8 + down(s8, 8)
    tpos = i * tm + lax.broadcasted_iota(jnp.int32, (tm, 1), 0)
    pooled = []
    for g, (win, ssum) in enumerate(zip(POOL_WINDOWS, (s2, s4, s8, s16))):
        lo = win // 2
        hi = win - 1 - lo
        lanes = slice(g * POOL_GROUP, (g + 1) * POOL_GROUP)
        ws = ssum[:, lanes]
        if hi > 0:
            ws = up(ws, hi)
        cnt = (jnp.minimum(tpos + hi + 1, seq_len) - jnp.maximum(tpos - lo, 0)).astype(f32)
        p = ws[ctr] / cnt - q[ctr, lanes]
        pooled.append(jnp.dot(p.astype(bf16), pw_ref[g], preferred_element_type=f32))
    yb_in = jnp.concatenate(pooled, axis=1) * ps_ref[...]
    yb = jnp.dot(yb_in.astype(bf16), wbb_ref[...], preferred_element_type=f32)

    us = jnp.dot(hb, win_ref[:, OFF_SC:OFF_SC + COL_SC], preferred_element_type=f32)
    cx = us[:, W_MIX:2 * W_MIX] * us[:, 2 * W_MIX:3 * W_MIX]
    sw = scw_ref[...]
    dw = down(cx, 1) * sw[0:1] + cx * sw[1:2] + up(cx, 1) * sw[2:3]
    sc_out = (us[:, 0:W_MIX] * dw)[ctr]
    yc = jnp.dot(sc_out.astype(bf16), wbc_ref[...], preferred_element_type=f32)

    gt = jax.nn.sigmoid(jnp.dot(hcb, win_ref[:, OFF_GATE:PROJ_COLS], preferred_element_type=f32))
    g0_ref[...] = gt[:, 0:D_MODEL]
    rest_ref[...] = gt[:, D_MODEL:2 * D_MODEL] * yb + gt[:, 2 * D_MODEL:3 * D_MODEL] * yc


def _proj_call(x, sh, sc, g1, win, hyw, pw, ps, scw, wbb, wbc, tm):
    B, L, D = x.shape
    nt = L // tm
    hb = tm // HALO
    row = lambda b, i: (b, i, 0)
    vec = lambda b, i: (b, 0, 0)
    out_w = jax.ShapeDtypeStruct((B, L, W_MIX), f32)
    out_d = jax.ShapeDtypeStruct((B, L, D), f32)
    return pl.pallas_call(
        functools.partial(_proj_body, tm=tm, seq_len=L),
        grid=(B, nt),
        in_specs=[
            pl.BlockSpec((None, tm, D), row),
            pl.BlockSpec((None, HALO, D), lambda b, i: (b, jnp.maximum(i * hb - 1, 0), 0)),
            pl.BlockSpec((None, HALO, D), lambda b, i: (b, jnp.minimum((i + 1) * hb, L // HALO - 1), 0)),
            pl.BlockSpec((None, 1, D), vec),
            pl.BlockSpec((None, 1, D), vec),
            _const_spec((1, D)),
            _const_spec((D, PROJ_COLS)),
            _const_spec((3, COL_HY)),
            _const_spec((len(POOL_WINDOWS), POOL_GROUP, POOL_GROUP)),
            _const_spec((1, W_MIX)),
            _const_spec((3, W_MIX)),
            _const_spec((W_MIX, D)),
            _const_spec((W_MIX, D)),
        ],
        out_specs=[pl.BlockSpec((None, tm, W_MIX), row)] * 3 + [pl.BlockSpec((None, tm, D), row)] * 2,
        out_shape=[out_w, out_w, out_w, out_d, out_d],
        compiler_params=_cparams(("parallel", "arbitrary")),
        name="proj",
    )(x, x, x, sh, sc, g1, win, hyw, pw, ps, scw, wbb, wbc)


def _filter_body(ca_ref, sa_ref, cb_ref, sb_ref, w1_ref, b1_ref, w2_ref, b2_ref, fr_ref, wo0_ref, wo1_ref, dl_ref, g_ref,
                 o_ref, asum_ref, h_s, *, rpb, seq_len, n1c):
    j = pl.program_id(0)
    s = pl.program_id(1)
    L = seq_len
    n2 = lax.broadcasted_iota(jnp.int32, (DFT_N2, 1), 0)
    fwd = n2 < DFT_ROWS

    def slot(r):
        pos = (j * rpb + r) + n1c * n2
        return pos, jnp.where(fwd, pos, 2 * L - pos).astype(f32)

    @pl.when(s == 0)
    def _():
        lane = lax.broadcasted_iota(jnp.int32, (DFT_N2, LANES), 1)
        fr = fr_ref[...]
        cb, sb = cb_ref[...], sb_ref[...]
        for r in range(rpb):
            _, lag = slot(r)
            ca = ca_ref[r:r + 1, :]
            sa = jnp.where(fwd, sa_ref[r:r + 1, :], -sa_ref[r:r + 1, :])
            cos_t = ca * cb - sa * sb
            sin_t = sa * cb + ca * sb
            z = jnp.where(lane == 0, lag / (L - 1), jnp.where(lane <= HY_BANDS, cos_t, -sin_t))
            zz = jnp.concatenate([z[:DFT_ROWS], z[DFT_ROWS:]], axis=1)
            h = jnp.sin(fr * (jnp.dot(zz, w1_ref[...], preferred_element_type=f32, precision=HIGHEST) + b1_ref[...]))
            h_s[r] = jnp.sin(fr * (jnp.dot(h, w2_ref[...], preferred_element_type=f32, precision=HIGHEST) + b2_ref[...]))

    asum = jnp.zeros(asum_ref.shape[1:], f32)
    bs = []
    for r in range(rpb):
        pos, lag = slot(r)
        h = h_s[r]
        ho = jnp.concatenate(
            [jnp.dot(h, wo0_ref[...], preferred_element_type=f32, precision=HIGHEST),
             jnp.dot(h, wo1_ref[...], preferred_element_type=f32, precision=HIGHEST)], axis=0)
        k = jnp.where(pos == L, 0.0, ho * jnp.exp(-(lag / (L - 1)) * dl_ref[...]))
        asum = asum + jnp.sum(jnp.abs(k), axis=0, keepdims=True)
        bs.append(jnp.dot(g_ref[r], k.astype(bf16), preferred_element_type=f32))
    o_ref[...] = jnp.swapaxes(jnp.stack(bs, axis=0), 0, 1).reshape(o_ref.shape).astype(bf16)

    @pl.when(j == 0)
    def _():
        asum_ref[s] = asum

    @pl.when(j > 0)
    def _():
        asum_ref[s] += asum


def _filter_tables(L):
    n1c = 2 * L // DFT_N2
    bands = jnp.linspace(1e-4, HY_BANDS - 1, HY_BANDS, dtype=f32)
    brow = jnp.zeros((LANES,), f32).at[1:1 + HY_BANDS].set(bands).at[1 + HY_BANDS:1 + 2 * HY_BANDS].set(bands)
    used = (jnp.arange(LANES) >= 1) & (jnp.arange(LANES) <= 2 * HY_BANDS)
    n2 = jnp.arange(DFT_N2)
    part_a = jnp.arange(n1c).astype(f32)
    part_b = (n1c * jnp.where(n2 < DFT_ROWS, n2, DFT_N2 - n2)).astype(f32)

    def cs(part):
        ang = (2 * math.pi / L) * part[:, None] * brow[None, :]
        return jnp.where(used, jnp.cos(ang), 0.0), jnp.where(used, jnp.sin(ang), 0.0)

    return cs(part_a) + cs(part_b)


def _filter_call(L, g_fwd, w1p, b1p, w2p, b2p, frp, wo_f, wo_b, dl_row):
    n1c = 2 * L // DFT_N2
    rpb = min(DFT_RPB, n1c)
    cw = HY_ORDER * W_MIX
    cs = DFT_SLAB
    ns = cw // cs
    ca, sa, cb, sb = _filter_tables(L)
    bs, asum = pl.pallas_call(
        functools.partial(_filter_body, rpb=rpb, seq_len=L, n1c=n1c),
        grid=(n1c // rpb, ns),
        in_specs=[pl.BlockSpec((rpb, LANES), lambda j, s: (j, 0)), pl.BlockSpec((rpb, LANES), lambda j, s: (j, 0)),
                  _const_spec((DFT_N2, LANES)), _const_spec((DFT_N2, LANES)),
                  _const_spec((2 * LANES, LANES)), _const_spec((1, LANES)),
                  _const_spec((LANES, LANES)), _const_spec((1, LANES)), _const_spec((1, LANES)),
                  pl.BlockSpec((LANES, cs), lambda j, s: (0, s)),
                  pl.BlockSpec((LANES, cs), lambda j, s: (0, s)),
                  pl.BlockSpec((1, cs), lambda j, s: (0, s)),
                  pl.BlockSpec((rpb, 2 * DFT_N2, DFT_N2), lambda j, s: (j, 0, 0))],
        out_specs=[pl.BlockSpec((2, DFT_N2, rpb, cs), lambda j, s: (0, 0, j, s)),
                   pl.BlockSpec((ns, 1, cs), lambda j, s: (0, 0, 0))],
        out_shape=[jax.ShapeDtypeStruct((2, DFT_N2, n1c, cw), bf16), jax.ShapeDtypeStruct((ns, 1, cs), f32)],
        scratch_shapes=[pltpu.VMEM((rpb, DFT_ROWS, LANES), f32)],
        compiler_params=_cparams(("arbitrary", "arbitrary")),
        name="filt",
    )(ca, sa, cb, sb, w1p, b1p, w2p, b2p, frp, wo_f, wo_b, dl_row, g_fwd)
    return bs, asum.reshape(1, cw)


def _dft_tables(L):
    n = 2 * L
    n1c = n // DFT_N2
    k2 = jnp.arange(DFT_N2, dtype=jnp.int32)
    tw_ang = ((jnp.arange(n1c, dtype=jnp.int32)[:, None] * k2[None, :]) % n).astype(f32) * (2 * math.pi / n)
    f_ang = ((k2[:, None] * k2[None, :]) % DFT_N2).astype(f32) * (2 * math.pi / DFT_N2)
    twr, twi = jnp.cos(tw_ang)[:, :, None], -jnp.sin(tw_ang)[:, :, None]
    fr, fi = jnp.cos(f_ang)[None], -jnp.sin(f_ang)[None]
    gr = twr * fr - twi * fi
    gi = twr * fi + twi * fr
    g_fwd = jnp.concatenate([gr, gi], axis=1)
    g_inv = jnp.transpose(g_fwd[:, :, :DFT_ROWS], (0, 2, 1)) * (1.0 / n)
    a = jnp.arange(n1c, dtype=jnp.int32)
    s_ang = ((a[:, None] * a[None, :]) % n1c).astype(f32) * (2 * math.pi / n1c)
    eye = jnp.eye(DFT_ROWS // n1c, dtype=f32)
    sr = jnp.kron(eye, jnp.cos(s_ang))
    si = jnp.kron(eye, -jnp.sin(s_ang))
    m_fwd = jnp.block([[sr, -si], [si, sr]])
    m_inv = jnp.block([[sr, si], [-si, sr]])
    return dict(g_fwd=g_fwd.astype(bf16), g_inv=g_inv.astype(bf16), m_fwd=m_fwd.astype(bf16),
                m_inv=m_inv.astype(bf16), n1=n1c)


def _n1_major(a):
    return jnp.swapaxes(a, 0, 1)


def _n1_minor(mats, shape):
    return jnp.swapaxes(jnp.stack(mats, axis=0), 0, 1).reshape(shape)


def _fft1_body(x_ref, g_ref, o_ref, *, rpb):
    x = _n1_major(x_ref[...])
    bs = [jnp.dot(g_ref[r], x[r].astype(bf16), preferred_element_type=f32) for r in range(rpb)]
    o_ref[...] = _n1_minor(bs, o_ref.shape).astype(bf16)


def _fft1_call(x4, g_fwd):
    B, _, n1c, C = x4.shape
    rpb = min(DFT_RPB, n1c)
    cs = DFT_SLAB
    return pl.pallas_call(
        functools.partial(_fft1_body, rpb=rpb),
        grid=(B, n1c // rpb, C // cs),
        in_specs=[pl.BlockSpec((None, DFT_ROWS, rpb, cs), lambda b, j, s: (b, 0, j, s)),
                  pl.BlockSpec((rpb, 2 * DFT_N2, DFT_ROWS), lambda b, j, s: (j, 0, 0))],
        out_specs=pl.BlockSpec((None, 2, DFT_N2, rpb, cs), lambda b, j, s: (b, 0, 0, j, s)),
        out_shape=jax.ShapeDtypeStruct((B, 2, DFT_N2, n1c, C), bf16),
        compiler_params=_cparams(("parallel", "arbitrary", "arbitrary")),
        name="fft1",
    )(x4, g_fwd)


def _fft2_body(b_ref, k_ref, mf_ref, mi_ref, o_ref, *, nsub):
    for i in range(nsub):
        rows = slice(i * DFT_ROWS, (i + 1) * DFT_ROWS)
        bst = jnp.concatenate([b_ref[0, rows, :], b_ref[1, rows, :]], axis=0)
        xs = jnp.dot(mf_ref[...], bst, preferred_element_type=f32)
        xr, xi = xs[:DFT_ROWS], xs[DFT_ROWS:]
        kr, ki = k_ref[0, rows, :].astype(f32), k_ref[1, rows, :].astype(f32)
        ys = jnp.concatenate([xr * kr - xi * ki, xr * ki + xi * kr], axis=0).astype(bf16)
        cs = jnp.dot(mi_ref[...], ys, preferred_element_type=f32)
        o_ref[0, rows, :] = cs[:DFT_ROWS].astype(bf16)
        o_ref[1, rows, :] = cs[DFT_ROWS:].astype(bf16)


def _fft2_call(bs, ksp, order, m_fwd, m_inv):
    B, _, n, C = bs.shape
    rb = min(DFT_STAGE2_ROWS, n)
    blk = pl.BlockSpec((None, 2, rb, C), lambda j, b: (b, 0, j, 0))
    return pl.pallas_call(
        functools.partial(_fft2_body, nsub=rb // DFT_ROWS),
        grid=(n // rb, B),
        in_specs=[blk,
                  pl.BlockSpec((2, rb, C), lambda j, b: (0, j, order)),
                  _const_spec((2 * DFT_ROWS, 2 * DFT_ROWS)),
                  _const_spec((2 * DFT_ROWS, 2 * DFT_ROWS))],
        out_specs=blk,
        out_shape=jax.ShapeDtypeStruct(bs.shape, bf16),
        compiler_params=_cparams(("arbitrary", "arbitrary")),
        name="fft2",
    )(bs, ksp, m_fwd, m_inv)


def _fft2_filter_body(b_ref, asum_ref, mf_ref, o_ref, *, nsub):
    inv = 1.0 / asum_ref[...]
    for i in range(nsub):
        rows = slice(i * DFT_ROWS, (i + 1) * DFT_ROWS)
        bst = jnp.concatenate([b_ref[0, rows, :], b_ref[1, rows, :]], axis=0)
        xs = jnp.dot(mf_ref[...], bst, preferred_element_type=f32)
        o_ref[0, rows, :] = (xs[:DFT_ROWS] * inv).astype(bf16)
        o_ref[1, rows, :] = (xs[DFT_ROWS:] * inv).astype(bf16)


def _fft2_filter_call(bs, asum, m_fwd):
    _, n, C = bs.shape
    rb = min(DFT_STAGE2_ROWS, n)
    blk = pl.BlockSpec((2, rb, C), lambda j: (0, j, 0))
    return pl.pallas_call(
        functools.partial(_fft2_filter_body, nsub=rb // DFT_ROWS),
        grid=(n // rb,),
        in_specs=[blk, _const_spec((1, C)), _const_spec((2 * DFT_ROWS, 2 * DFT_ROWS))],
        out_specs=blk,
        out_shape=jax.ShapeDtypeStruct(bs.shape, bf16),
        compiler_params=_cparams(("arbitrary",)),
        name="fft2_filter",
    )(bs, asum, m_fwd)


def _fft3_body(c_ref, gi_ref, gate_ref, prev_ref, sk_ref, *rest, rpb, fuse_next):
    if fuse_next:
        gf_ref, z_ref, b_ref = rest
    else:
        (z_ref,) = rest
    cs = c_ref.shape[-1]
    c = _n1_major(c_ref[...].astype(f32).reshape(2 * DFT_N2, rpb, cs))
    gate = _n1_major(gate_ref[...])
    prev = _n1_major(prev_ref[...])
    sk = sk_ref[...]
    zs, bs = [], []
    for r in range(rpb):
        y = jnp.dot(gi_ref[r], c[r].astype(bf16), preferred_element_type=f32)
        z = gate[r] * (y + sk * prev[r])
        zs.append(z)
        if fuse_next:
            bs.append(jnp.dot(gf_ref[r], z.astype(bf16), preferred_element_type=f32))
    z_ref[...] = _n1_minor(zs, z_ref.shape)
    if fuse_next:
        b_ref[...] = _n1_minor(bs, b_ref.shape).astype(bf16)


def _fft3_call(cs5, g_inv, gate, prev, sk_row, g_fwd=None):
    B, _, _, n1c, C = cs5.shape
    rpb = min(DFT_RPB, n1c)
    cs = DFT_SLAB
    tblk = pl.BlockSpec((None, DFT_ROWS, rpb, cs), lambda b, j, s: (b, 0, j, s))
    sblk = pl.BlockSpec((None, 2, DFT_N2, rpb, cs), lambda b, j, s: (b, 0, 0, j, s))
    in_specs = [sblk, pl.BlockSpec((rpb, DFT_ROWS, 2 * DFT_N2), lambda b, j, s: (j, 0, 0)), tblk, tblk,
                pl.BlockSpec((1, cs), lambda b, j, s: (0, s))]
    args = [cs5, g_inv, gate, prev, sk_row]
    out_specs = [tblk]
    out_shape = [jax.ShapeDtypeStruct(gate.shape, f32)]
    fuse_next = g_fwd is not None
    if fuse_next:
        in_specs.append(pl.BlockSpec((rpb, 2 * DFT_N2, DFT_ROWS), lambda b, j, s: (j, 0, 0)))
        args.append(g_fwd)
        out_specs.append(sblk)
        out_shape.append(jax.ShapeDtypeStruct(cs5.shape, bf16))
    return pl.pallas_call(
        functools.partial(_fft3_body, rpb=rpb, fuse_next=fuse_next),
        grid=(B, n1c // rpb, C // cs),
        in_specs=in_specs,
        out_specs=out_specs,
        out_shape=out_shape,
        compiler_params=_cparams(("parallel", "arbitrary", "arbitrary")),
        name="fft3_next" if fuse_next else "fft3",
    )(*args)


def _filter_spectrum(L, tabs, filt_params):
    bs, asum = _filter_call(L, tabs["g_fwd"], *filt_params)
    return _fft2_filter_call(bs.reshape(2, 2 * L, bs.shape[-1]), asum, tabs["m_fwd"])


def _hyena_conv(v, x1, x2, ksp, skip, tabs):
    n1c = tabs["n1"]
    B, L, C = v.shape
    n = 2 * L
    v4, x14, x24 = (a.reshape(B, DFT_ROWS, n1c, C) for a in (v, x1, x2))
    s5 = (B, 2, DFT_N2, n1c, C)
    bs = _fft1_call(v4, tabs["g_fwd"])
    cs = _fft2_call(bs.reshape(B, 2, n, C), ksp, 0, tabs["m_fwd"], tabs["m_inv"])
    z1, bs = _fft3_call(cs.reshape(s5), tabs["g_inv"], x14, v4, skip[0:1], tabs["g_fwd"])
    cs = _fft2_call(bs.reshape(B, 2, n, C), ksp, 1, tabs["m_fwd"], tabs["m_inv"])
    (z2,) = _fft3_call(cs.reshape(s5), tabs["g_inv"], x24, z1, skip[1:2])
    return z2.reshape(B, L, C)


def _route(r):
    lane = lax.broadcasted_iota(jnp.int32, r.shape, 1)
    ninf = jnp.float32(-jnp.inf)
    big = jnp.int32(1 << 20)
    is_g = lane < N_GROUPS
    gmax = jnp.max(jnp.where(is_g, r, ninf), axis=-1, keepdims=True)
    gidx = jnp.min(jnp.where(jnp.logical_and(is_g, r == gmax), lane, big), axis=-1, keepdims=True)
    gw = 1.0 / jnp.sum(jnp.where(is_g, jnp.exp(r - gmax), 0.0), axis=-1, keepdims=True)
    e_lane = lane - N_GROUPS
    sel = jnp.logical_and(jnp.logical_and(e_lane >= 0, e_lane < N_EXPERTS), (e_lane >> 2) == gidx)
    le = jnp.where(sel, r, ninf)
    m1 = jnp.max(le, axis=-1, keepdims=True)
    i1 = jnp.min(jnp.where(le == m1, lane, big), axis=-1, keepdims=True)
    le2 = jnp.where(lane == i1, ninf, le)
    m2 = jnp.max(le2, axis=-1, keepdims=True)
    i2 = jnp.min(jnp.where(le2 == m2, lane, big), axis=-1, keepdims=True)
    e2 = jnp.exp(m2 - m1)
    den = 1.0 + e2
    comb = jnp.where(lane == i1, gw / den, jnp.where(lane == i2, gw * e2 / den, 0.0))
    return comb, gidx


def _mix_body(x_ref, z_ref, g0_ref, rest_ref, ga1_ref, sh2_ref, sc2_ref, n2g_ref, wba_ref, wout_ref, wr_ref, tri_ref,
              xo_ref, hs_ref, combs_ref, pmt_ref, cnt_ref):
    ya = jnp.dot(z_ref[...].astype(bf16), wba_ref[...], preferred_element_type=f32)
    merged = g0_ref[...] * ya + rest_ref[...]
    xo = x_ref[...] + ga1_ref[...] * jnp.dot(merged.astype(bf16), wout_ref[...], preferred_element_type=f32)
    xo_ref[...] = xo
    ms = jnp.mean(xo * xo, axis=-1, keepdims=True)
    h2 = xo * lax.rsqrt(ms + EPS) * n2g_ref[...]
    h2 = h2 * (1.0 + sc2_ref[...]) + sh2_ref[...]
    comb, gidx = _route(jnp.dot(h2, wr_ref[...], preferred_element_type=f32, precision=HIGHEST))

    tm = comb.shape[0]
    lane = lax.broadcasted_iota(jnp.int32, comb.shape, 1)
    onehot = (lane == gidx).astype(f32)
    cum = jnp.dot(tri_ref[...], onehot.astype(bf16), preferred_element_type=f32)
    tot8 = cum[tm - 8:tm, :]
    off8 = pltpu.roll(tot8, 1, 1) + pltpu.roll(tot8, 2, 1) + pltpu.roll(tot8, 3, 1)
    rank = jnp.sum(onehot * (off8[7:8, :] + cum - 1.0), axis=-1, keepdims=True)
    slot = lax.broadcasted_iota(jnp.int32, (tm, tm), 1).astype(f32)
    pmt = (slot == rank).astype(bf16)
    pmt_ref[...] = pmt
    tn = (((0,), (0,)), ((), ()))
    hs_ref[...] = lax.dot_general(pmt, h2.astype(bf16), tn, preferred_element_type=f32).astype(bf16)
    combs_ref[...] = lax.dot_general(pmt.astype(f32), comb, tn, preferred_element_type=f32, precision=HIGHEST)
    cnt_ref[...] = tot8[7:8, :].astype(jnp.int32)


def _mix_call(x, z, g0, rest, ga1, sh2, sc2, n2g, wba, wout, wr, tri, tm):
    B, L, D = x.shape
    nt = L // tm
    row = lambda b, i: (b, i, 0)
    vspec = pl.BlockSpec((None, 1, D), lambda b, i: (b, 0, 0))
    return pl.pallas_call(
        _mix_body,
        grid=(B, nt),
        in_specs=[
            pl.BlockSpec((None, tm, D), row),
            pl.BlockSpec((None, tm, W_MIX), row),
            pl.BlockSpec((None, tm, D), row),
            pl.BlockSpec((None, tm, D), row),
            vspec, vspec, vspec,
            _const_spec((1, D)),
            _const_spec((W_MIX, D)),
            _const_spec((D, D)),
            _const_spec((D, LANES)),
            _const_spec((tm, tm)),
        ],
        out_specs=[pl.BlockSpec((None, tm, D), row), pl.BlockSpec((None, tm, D), row),
                   pl.BlockSpec((None, tm, LANES), row), pl.BlockSpec((None, tm, tm), row),
                   pl.BlockSpec((None, None, 1, LANES), lambda b, i: (b, i, 0, 0))],
        out_shape=[jax.ShapeDtypeStruct((B, L, D), f32), jax.ShapeDtypeStruct((B, L, D), bf16),
                   jax.ShapeDtypeStruct((B, L, LANES), f32), jax.ShapeDtypeStruct((B, L, tm), bf16),
                   jax.ShapeDtypeStruct((B, nt, 1, LANES), jnp.int32)],
        compiler_params=_cparams(("parallel", "arbitrary")),
        name="mix",
    )(x, z, g0, rest, ga1, sh2, sc2, n2g, wba, wout, wr, tri)


def _experts_body(cnt_ref, hs_ref, combs_ref, ex_ref, w1_ref, w3_ref, w2_ref, ys_ref, acc_s, *, tm, sub):
    b, sup, g = pl.program_id(0), pl.program_id(1), pl.program_id(2)
    nchunk = tm // MOE_CHUNK

    @pl.when(g == 0)
    def _():
        acc_s[...] = jnp.zeros_like(acc_s)

    def tile(t, carry):
        base = ((b * pl.num_programs(1) + sup) * sub + t) * N_GROUPS
        lo = jnp.int32(0)
        for gg in range(N_GROUPS - 1):
            lo = lo + jnp.where(gg < g, cnt_ref[base + gg], 0)
        hi = lo + cnt_ref[base + g]
        for c in range(nchunk):
            @pl.when(jnp.logical_and(lo < (c + 1) * MOE_CHUNK, hi > c * MOE_CHUNK))
            def _():
                rows = pl.ds(pl.multiple_of(t * tm + c * MOE_CHUNK, MOE_CHUNK), MOE_CHUNK)
                h = hs_ref[rows, :]
                a = jnp.dot(h, w1_ref[...], preferred_element_type=f32)
                u = jnp.dot(h, w3_ref[...], preferred_element_type=f32)
                cw = jnp.dot(combs_ref[rows, :], ex_ref[...], preferred_element_type=f32, precision=HIGHEST)
                hid = (a * jax.nn.sigmoid(a) * u * cw).astype(bf16)
                acc_s[rows, :] += jnp.dot(hid, w2_ref[...], preferred_element_type=f32)
        return carry

    lax.fori_loop(0, sub, tile, 0)

    @pl.when(g == pl.num_programs(2) - 1)
    def _():
        ys_ref[...] = acc_s[...].astype(bf16)


def _experts_call(cnt, hs, combs, ex, w1, w3, w2, tm):
    B, L, D = hs.shape
    sub = min(MOE_SUB, L // tm)
    rows = sub * tm
    blk = lambda b, s, g, cnt: (b, s, 0)
    return pl.pallas_call(
        functools.partial(_experts_body, tm=tm, sub=sub),
        grid_spec=pltpu.PrefetchScalarGridSpec(
            num_scalar_prefetch=1,
            grid=(B, L // rows, N_GROUPS),
            in_specs=[
                pl.BlockSpec((None, rows, D), blk),
                pl.BlockSpec((None, rows, LANES), blk),
                pl.BlockSpec((None, LANES, GROUP_HID), lambda b, s, g, cnt: (g, 0, 0)),
                pl.BlockSpec((D, GROUP_HID), lambda b, s, g, cnt: (0, g)),
                pl.BlockSpec((D, GROUP_HID), lambda b, s, g, cnt: (0, g)),
                pl.BlockSpec((GROUP_HID, D), lambda b, s, g, cnt: (g, 0)),
            ],
            out_specs=pl.BlockSpec((None, rows, D), blk),
            scratch_shapes=[pltpu.VMEM((rows, D), f32)],
        ),
        out_shape=jax.ShapeDtypeStruct((B, L, D), bf16),
        compiler_params=_cparams(("arbitrary", "arbitrary", "arbitrary")),
        name="experts",
    )(cnt, hs, combs, ex, w1, w3, w2)


def _unsort_body(ys_ref, pmt_ref, xo_ref, ga2_ref, fg_ref, o_ref, *, final_norm):
    y = xo_ref[...] + ga2_ref[...] * jnp.dot(pmt_ref[...], ys_ref[...], preferred_element_type=f32)
    if final_norm:
        ms = jnp.mean(y * y, axis=-1, keepdims=True)
        y = y * lax.rsqrt(ms + EPS) * fg_ref[...]
    o_ref[...] = y


def _unsort_call(ys, pmt, xo, ga2, fg, tm, final_norm):
    B, L, D = xo.shape
    row = lambda b, i: (b, i, 0)
    return pl.pallas_call(
        functools.partial(_unsort_body, final_norm=final_norm),
        grid=(B, L // tm),
        in_specs=[pl.BlockSpec((None, tm, D), row), pl.BlockSpec((None, tm, tm), row), pl.BlockSpec((None, tm, D), row),
                  pl.BlockSpec((None, 1, D), lambda b, i: (b, 0, 0)), _const_spec((1, D))],
        out_specs=pl.BlockSpec((None, tm, D), row),
        out_shape=jax.ShapeDtypeStruct((B, L, D), f32),
        compiler_params=_cparams(("parallel", "arbitrary")),
        name="unsort",
    )(ys, pmt, xo, ga2, fg)


def _pad_to(a, shape):
    return jnp.pad(a, [(0, s - d) for d, s in zip(a.shape, shape)])


def _prep_layer(l, p):
    max_decay = math.log(HY_TARGET) / HY_FAST_DECAY
    min_decay = math.log(HY_TARGET) / HY_SLOW_DECAY
    deltas = jnp.abs(jnp.linspace(min_decay, max_decay, W_MIX, dtype=f32))
    router = jnp.concatenate([p["router_g"][l], p["router_e"][l]], axis=1)
    lanes = jnp.arange(LANES)[None, :, None]
    cols = jnp.arange(GROUP_HID)[None, None, :]
    grp = jnp.arange(N_GROUPS)[:, None, None]
    expand = (lanes == N_GROUPS + EXP_PER_GROUP * grp + cols // D_EXPERT).astype(f32)
    zh = jnp.zeros((HY_HID, HY_HID), f32)
    w1 = _pad_to(p["hy_w1"][l], (LANES, HY_HID))
    zw1 = jnp.zeros_like(w1)
    w1_pair = jnp.block([[w1, zw1], [zw1, w1]])
    w2_pair = jnp.block([[p["hy_w2"][l], zh], [zh, p["hy_w2"][l]]])
    pair = lambda a: jnp.concatenate([a, a])[None]
    cw = HY_ORDER * W_MIX
    wo = p["hy_w_out"][l]
    zwo = jnp.zeros((HY_HID, cw), f32)
    wo_f = jnp.concatenate([wo[:, :cw], zwo], axis=0)
    wo_b = jnp.concatenate([zwo, wo[:, cw:]], axis=0)
    return dict(
        norm1_g=p["norm1_g"][l][None], norm2_g=p["norm2_g"][l][None],
        w_in=p["w_in"][l].astype(bf16), hy_conv_w=p["hy_conv_w"][l], hy_skip=p["hy_skip"][l],
        pool_w=p["pool_w"][l].astype(bf16), pool_scale=p["pool_scale"][l][None], sc_conv_w=p["sc_conv_w"][l],
        w_br_a=p["w_br_a"][l].astype(bf16), w_br_b=p["w_br_b"][l].astype(bf16), w_br_c=p["w_br_c"][l].astype(bf16),
        w_out=p["w_out"][l].astype(bf16),
        router=_pad_to(router, (D_MODEL, LANES)), expand=expand,
        moe_w1=p["moe_w1"][l].astype(bf16), moe_w3=p["moe_w3"][l].astype(bf16), moe_w2=p["moe_w2"][l].astype(bf16),
        filt=(w1_pair, pair(p["hy_b1"][l]), w2_pair, pair(p["hy_b2"][l]), pair(p["hy_freq"][l]), wo_f, wo_b,
              pair(deltas)),
    )


def _tile(L, want):
    return want if L % want == 0 else L


def _encoder_layer(x, mod, lp, ksp, tabs, final_g, final_norm):
    B, L, D = x.shape
    sh1, sc1, ga1, sh2, sc2, ga2 = (m[:, None, :] for m in jnp.split(mod, 6, axis=-1))
    v, x1, x2, g0, rest = _proj_call(x, sh1, sc1, lp["norm1_g"], lp["w_in"], lp["hy_conv_w"], lp["pool_w"],
                                     lp["pool_scale"], lp["sc_conv_w"], lp["w_br_b"], lp["w_br_c"], _tile(L, 512))
    z = _hyena_conv(v, x1, x2, ksp, lp["hy_skip"], tabs)
    tm = _tile(L, MOE_TILE)
    tri = jnp.tri(tm, dtype=bf16)
    xo, hs, combs, pmt, cnt = _mix_call(x, z, g0, rest, ga1, sh2, sc2, lp["norm2_g"], lp["w_br_a"], lp["w_out"],
                                        lp["router"], tri, tm)
    ys = _experts_call(cnt[:, :, 0, :N_GROUPS].reshape(-1), hs, combs, lp["expand"], lp["moe_w1"], lp["moe_w3"],
                       lp["moe_w2"], tm)
    return _unsort_call(ys, pmt, xo, ga2, final_g, tm, final_norm)


def _forward(xs, cs, p, final_g):
    depth = p["w_in"].shape[0]
    nb = [c.shape[0] for c in cs]
    rows = -(-sum(nb) // 8) * 8
    c_all = _pad_to(jnp.concatenate(cs, axis=0), (rows, D_MODEL))
    lens = sorted({x.shape[1] for x in xs})
    tabs = {L: _dft_tables(L) for L in lens}
    fg = final_g[None]
    for l in range(depth):
        lp = _prep_layer(l, p)
        mod = _mod_call(c_all, p["ada_w"][l], p["ada_b"][l][None])
        ksp = {L: _filter_spectrum(L, tabs[L], lp["filt"]) for L in lens}
        off = 0
        out = []
        for x, n in zip(xs, nb):
            L = x.shape[1]
            out.append(_encoder_layer(x, mod[off:off + n], lp, ksp[L], tabs[L], fg, l == depth - 1))
            off += n
        xs = out
    return xs


def kernel(x_prompt, x_sample, c_prompt, c_sample, ada_w, ada_b, norm1_g, norm2_g, w_in, hy_conv_w, hy_skip, hy_w1, hy_b1, hy_w2, hy_b2, hy_w_out, hy_freq, pool_w, pool_scale, sc_conv_w, w_br_a, w_br_b, w_br_c, w_out, router_g, router_e, moe_w1, moe_w3, moe_w2, final_g):
    p = dict(ada_w=ada_w, ada_b=ada_b, norm1_g=norm1_g, norm2_g=norm2_g, w_in=w_in, hy_conv_w=hy_conv_w,
             hy_skip=hy_skip, hy_w1=hy_w1, hy_b1=hy_b1, hy_w2=hy_w2, hy_b2=hy_b2, hy_w_out=hy_w_out, hy_freq=hy_freq,
             pool_w=pool_w, pool_scale=pool_scale, sc_conv_w=sc_conv_w, w_br_a=w_br_a, w_br_b=w_br_b, w_br_c=w_br_c,
             w_out=w_out, router_g=router_g, router_e=router_e, moe_w1=moe_w1, moe_w3=moe_w3, moe_w2=moe_w2)
    y_prompt, y_sample = _forward([x_prompt, x_sample], [c_prompt, c_sample], p, final_g)
    return (y_prompt, y_sample)
```

```python
import functools
import math

import jax
import jax.numpy as jnp
from jax import lax
from jax.experimental import pallas as pl
from jax.experimental.pallas import tpu as pltpu

f32 = jnp.float32
bf16 = jnp.bfloat16
HIGHEST = lax.Precision.HIGHEST

D_MODEL = 1024
DEPTH = 2
W_MIX = 512
HY_ORDER = 2
HY_BANDS = 16
HY_HID = 64
HY_FAST_DECAY = 0.3
HY_SLOW_DECAY = 1.5
HY_TARGET = 1e-2
POOL_WINDOWS = (2, 4, 8, 16)
POOL_GROUP = W_MIX // len(POOL_WINDOWS)
COL_HY = 3 * W_MIX
COL_POOL = W_MIX
COL_SC = 3 * W_MIX
COL_GATE = 3 * D_MODEL
OFF_POOL = COL_HY
OFF_SC = COL_HY + COL_POOL
OFF_GATE = COL_HY + COL_POOL + COL_SC
PROJ_COLS = OFF_GATE + COL_GATE
N_GROUPS = 4
EXP_PER_GROUP = 4
N_EXPERTS = N_GROUPS * EXP_PER_GROUP
D_EXPERT = 256
GROUP_HID = EXP_PER_GROUP * D_EXPERT
EPS = 1e-6

HALO = 8
DFT_N2 = 256
DFT_ROWS = 128
DFT_RPB = 16
DFT_SLAB = 256
DFT_STAGE2_ROWS = 1024
MOE_TILE = 512
MOE_CHUNK = 128
MOE_SUB = 4
LANES = 128
VMEM_LIMIT = 56 * 1024 * 1024


def _cparams(sem):
    return pltpu.CompilerParams(dimension_semantics=sem, vmem_limit_bytes=VMEM_LIMIT)


def _const_spec(shape):
    nd = len(shape)
    return pl.BlockSpec(shape, lambda *_: (0,) * nd, pipeline_mode=pl.Buffered(1))


def _mod_body(c_ref, w_ref, b_ref, o_ref):
    c = c_ref[...]
    s = c * jax.nn.sigmoid(c)
    o_ref[...] = jnp.dot(s, w_ref[...], preferred_element_type=f32, precision=HIGHEST) + b_ref[...]


def _mod_call(c_all, ada_w, ada_b):
    rows = c_all.shape[0]
    tn = 1536
    return pl.pallas_call(
        _mod_body,
        grid=(6 * D_MODEL // tn,),
        in_specs=[pl.BlockSpec((rows, D_MODEL), lambda j: (0, 0)),
                  pl.BlockSpec((D_MODEL, tn), lambda j: (0, j)),
                  pl.BlockSpec((1, tn), lambda j: (0, j))],
        out_specs=pl.BlockSpec((rows, tn), lambda j: (0, j)),
        out_shape=jax.ShapeDtypeStruct((rows, 6 * D_MODEL), f32),
        compiler_params=_cparams(("arbitrary",)),
        name="mod",
    )(c_all, ada_w, ada_b)


def _proj_body(xm_ref, xp_ref, xn_ref, sh_ref, sc_ref, g_ref, win_ref, hyw_ref, pw_ref, ps_ref, scw_ref,
               wbb_ref, wbc_ref, v_ref, x1_ref, x2_ref, g0_ref, rest_ref, *, tm, seq_len):
    i = pl.program_id(1)
    nt = pl.num_programs(1)
    rt = tm + 2 * HALO
    ctr = slice(HALO, HALO + tm)

    def modulated(x):
        ms = jnp.mean(x * x, axis=-1, keepdims=True)
        h = x * lax.rsqrt(ms + EPS) * g_ref[...]
        return h * (1.0 + sc_ref[...]) + sh_ref[...]

    hp = jnp.where(i > 0, modulated(xp_ref[...]), 0.0)
    hn = jnp.where(i < nt - 1, modulated(xn_ref[...]), 0.0)
    hc = modulated(xm_ref[...])
    hb = jnp.concatenate([hp, hc, hn], axis=0).astype(bf16)
    hcb = hc.astype(bf16)

    def down(a, s):
        return pltpu.roll(a, s, 0)

    def up(a, s):
        return pltpu.roll(a, rt - s, 0)

    u = jnp.dot(hb, win_ref[:, 0:COL_HY], preferred_element_type=f32)
    w = hyw_ref[...]
    uc = (down(u, 1) * w[0:1] + u * w[1:2] + up(u, 1) * w[2:3])[ctr]
    v_ref[...] = uc[:, 0:W_MIX]
    x1_ref[...] = uc[:, W_MIX:2 * W_MIX]
    x2_ref[...] = uc[:, 2 * W_MIX:3 * W_MIX]

    q = jnp.dot(hb, win_ref[:, OFF_POOL:OFF_POOL + COL_POOL], preferred_element_type=f32)
    s2 = q + down(q, 1)
    s4 = s2 + down(s2, 2)
    s8 = s4 + down(s4, 4)
    s16 = s8 + down(s8, 8)
    tpos = i * tm + lax.broadcasted_iota(jnp.int32, (tm, 1), 0)
    pooled = []
    for g, (win, ssum) in enumerate(zip(POOL_WINDOWS, (s2, s4, s8, s16))):
        lo = win // 2
        hi = win - 1 - lo
        lanes = slice(g * POOL_GROUP, (g + 1) * POOL_GROUP)
        ws = ssum[:, lanes]
        if hi > 0:
            ws = up(ws, hi)
        cnt = (jnp.minimum(tpos + hi + 1, seq_len) - jnp.maximum(tpos - lo, 0)).astype(f32)
        p = ws[ctr] / cnt - q[ctr, lanes]
        pooled.append(jnp.dot(p.astype(bf16), pw_ref[g], preferred_element_type=f32))
    yb_in = jnp.concatenate(pooled, axis=1) * ps_ref[...]
    yb = jnp.dot(yb_in.astype(bf16), wbb_ref[...], preferred_element_type=f32)

    us = jnp.dot(hb, win_ref[:, OFF_SC:OFF_SC + COL_SC], preferred_element_type=f32)
    cx = us[:, W_MIX:2 * W_MIX] * us[:, 2 * W_MIX:3 * W_MIX]
    sw = scw_ref[...]
    dw = down(cx, 1) * sw[0:1] + cx * sw[1:2] + up(cx, 1) * sw[2:3]
    sc_out = (us[:, 0:W_MIX] * dw)[ctr]
    yc = jnp.dot(sc_out.astype(bf16), wbc_ref[...], preferred_element_type=f32)

    gt = 0.5 * jnp.tanh(0.5 * jnp.dot(hcb, win_ref[:, OFF_GATE:PROJ_COLS], preferred_element_type=f32)) + 0.5
    g0_ref[...] = gt[:, 0:D_MODEL]
    rest_ref[...] = gt[:, D_MODEL:2 * D_MODEL] * yb + gt[:, 2 * D_MODEL:3 * D_MODEL] * yc


def _proj_call(x, sh, sc, g1, win, hyw, pw, ps, scw, wbb, wbc, tm):
    B, L, D = x.shape
    nt = L // tm
    hb = tm // HALO
    row = lambda b, i: (b, i, 0)
    vec = lambda b, i: (b, 0, 0)
    out_w = jax.ShapeDtypeStruct((B, L, W_MIX), f32)
    out_d = jax.ShapeDtypeStruct((B, L, D), f32)
    return pl.pallas_call(
        functools.partial(_proj_body, tm=tm, seq_len=L),
        grid=(B, nt),
        in_specs=[
            pl.BlockSpec((None, tm, D), row),
            pl.BlockSpec((None, HALO, D), lambda b, i: (b, jnp.maximum(i * hb - 1, 0), 0)),
            pl.BlockSpec((None, HALO, D), lambda b, i: (b, jnp.minimum((i + 1) * hb, L // HALO - 1), 0)),
            pl.BlockSpec((None, 1, D), vec),
            pl.BlockSpec((None, 1, D), vec),
            _const_spec((1, D)),
            _const_spec((D, PROJ_COLS)),
            _const_spec((3, COL_HY)),
            _const_spec((len(POOL_WINDOWS), POOL_GROUP, POOL_GROUP)),
            _const_spec((1, W_MIX)),
            _const_spec((3, W_MIX)),
            _const_spec((W_MIX, D)),
            _const_spec((W_MIX, D)),
        ],
        out_specs=[pl.BlockSpec((None, tm, W_MIX), row)] * 3 + [pl.BlockSpec((None, tm, D), row)] * 2,
        out_shape=[out_w, out_w, out_w, out_d, out_d],
        compiler_params=_cparams(("parallel", "arbitrary")),
        name="proj",
    )(x, x, x, sh, sc, g1, win, hyw, pw, ps, scw, wbb, wbc)


def _filter_body(ca_ref, sa_ref, cb_ref, sb_ref, w1_ref, b1_ref, w2_ref, b2_ref, fr_ref, wo0_ref, wo1_ref, dl_ref, g_ref,
                 o_ref, asum_ref, h_s, *, rpb, seq_len, n1c):
    j = pl.program_id(0)
    s = pl.program_id(1)
    L = seq_len
    n2 = lax.broadcasted_iota(jnp.int32, (DFT_N2, 1), 0)
    fwd = n2 < DFT_ROWS

    def slot(r):
        pos = (j * rpb + r) + n1c * n2
        return pos, jnp.where(fwd, pos, 2 * L - pos).astype(f32)

    @pl.when(s == 0)
    def _():
        lane = lax.broadcasted_iota(jnp.int32, (DFT_N2, LANES), 1)
        fr = fr_ref[...]
        cb, sb = cb_ref[...], sb_ref[...]
        zs = []
        for r in range(rpb):
            _, lag = slot(r)
            ca = ca_ref[r:r + 1, :]
            sa = jnp.where(fwd, sa_ref[r:r + 1, :], -sa_ref[r:r + 1, :])
            cos_t = ca * cb - sa * sb
            sin_t = sa * cb + ca * sb
            z = jnp.where(lane == 0, lag / (L - 1), jnp.where(lane <= HY_BANDS, cos_t, -sin_t))
            zs.append(jnp.concatenate([z[:DFT_ROWS], z[DFT_ROWS:]], axis=1))
        zz = jnp.concatenate(zs, axis=0)
        h = jnp.sin(fr * (jnp.dot(zz, w1_ref[...], preferred_element_type=f32, precision=HIGHEST) + b1_ref[...]))
        h_s[...] = jnp.sin(fr * (jnp.dot(h, w2_ref[...], preferred_element_type=f32, precision=HIGHEST) + b2_ref[...]))

    hb = h_s[...].astype(bf16)
    ho_f = jnp.dot(hb, wo0_ref[...], preferred_element_type=f32)
    ho_b = jnp.dot(hb, wo1_ref[...], preferred_element_type=f32)
    asum = jnp.zeros(asum_ref.shape[1:], f32)
    bs = []
    for r in range(rpb):
        pos, lag = slot(r)
        rows = slice(r * DFT_ROWS, (r + 1) * DFT_ROWS)
        ho = jnp.concatenate([ho_f[rows], ho_b[rows]], axis=0)
        k = jnp.where(pos == L, 0.0, ho * jnp.exp(-(lag / (L - 1)) * dl_ref[...]))
        asum = asum + jnp.sum(jnp.abs(k), axis=0, keepdims=True)
        bs.append(jnp.dot(g_ref[r], k.astype(bf16), preferred_element_type=f32))
    o_ref[...] = jnp.swapaxes(jnp.stack(bs, axis=0), 0, 1).reshape(o_ref.shape).astype(bf16)

    @pl.when(j == 0)
    def _():
        asum_ref[s] = asum

    @pl.when(j > 0)
    def _():
        asum_ref[s] += asum


def _filter_tables(L):
    n1c = 2 * L // DFT_N2
    bands = jnp.linspace(1e-4, HY_BANDS - 1, HY_BANDS, dtype=f32)
    brow = jnp.zeros((LANES,), f32).at[1:1 + HY_BANDS].set(bands).at[1 + HY_BANDS:1 + 2 * HY_BANDS].set(bands)
    used = (jnp.arange(LANES) >= 1) & (jnp.arange(LANES) <= 2 * HY_BANDS)
    n2 = jnp.arange(DFT_N2)
    part_a = jnp.arange(n1c).astype(f32)
    part_b = (n1c * jnp.where(n2 < DFT_ROWS, n2, DFT_N2 - n2)).astype(f32)

    def cs(part):
        ang = (2 * math.pi / L) * part[:, None] * brow[None, :]
        return jnp.where(used, jnp.cos(ang), 0.0), jnp.where(used, jnp.sin(ang), 0.0)

    return cs(part_a) + cs(part_b)


def _filter_call(L, g_fwd, w1p, b1p, w2p, b2p, frp, wo_f, wo_b, dl_row):
    n1c = 2 * L // DFT_N2
    rpb = min(DFT_RPB, n1c)
    cw = HY_ORDER * W_MIX
    cs = DFT_SLAB
    ns = cw // cs
    ca, sa, cb, sb = _filter_tables(L)
    bs, asum = pl.pallas_call(
        functools.partial(_filter_body, rpb=rpb, seq_len=L, n1c=n1c),
        grid=(n1c // rpb, ns),
        in_specs=[pl.BlockSpec((rpb, LANES), lambda j, s: (j, 0)), pl.BlockSpec((rpb, LANES), lambda j, s: (j, 0)),
                  _const_spec((DFT_N2, LANES)), _const_spec((DFT_N2, LANES)),
                  _const_spec((2 * LANES, LANES)), _const_spec((1, LANES)),
                  _const_spec((LANES, LANES)), _const_spec((1, LANES)), _const_spec((1, LANES)),
                  pl.BlockSpec((LANES, cs), lambda j, s: (0, s)),
                  pl.BlockSpec((LANES, cs), lambda j, s: (0, s)),
                  pl.BlockSpec((1, cs), lambda j, s: (0, s)),
                  pl.BlockSpec((rpb, 2 * DFT_N2, DFT_N2), lambda j, s: (j, 0, 0))],
        out_specs=[pl.BlockSpec((2, DFT_N2, rpb, cs), lambda j, s: (0, 0, j, s)),
                   pl.BlockSpec((ns, 1, cs), lambda j, s: (0, 0, 0))],
        out_shape=[jax.ShapeDtypeStruct((2, DFT_N2, n1c, cw), bf16), jax.ShapeDtypeStruct((ns, 1, cs), f32)],
        scratch_shapes=[pltpu.VMEM((rpb * DFT_ROWS, LANES), f32)],
        compiler_params=_cparams(("arbitrary", "arbitrary")),
        name="filt",
    )(ca, sa, cb, sb, w1p, b1p, w2p, b2p, frp, wo_f, wo_b, dl_row, g_fwd)
    return bs, asum.reshape(1, cw)


def _dft_tables(L):
    n = 2 * L
    n1c = n // DFT_N2
    k2 = jnp.arange(DFT_N2, dtype=jnp.int32)
    tw_ang = ((jnp.arange(n1c, dtype=jnp.int32)[:, None] * k2[None, :]) % n).astype(f32) * (2 * math.pi / n)
    f_ang = ((k2[:, None] * k2[None, :]) % DFT_N2).astype(f32) * (2 * math.pi / DFT_N2)
    twr, twi = jnp.cos(tw_ang)[:, :, None], -jnp.sin(tw_ang)[:, :, None]
    fr, fi = jnp.cos(f_ang)[None], -jnp.sin(f_ang)[None]
    gr = twr * fr - twi * fi
    gi = twr * fi + twi * fr
    g_fwd = jnp.concatenate([gr, gi], axis=1)
    twr_t, twi_t = jnp.cos(tw_ang)[:, None, :], -jnp.sin(tw_ang)[:, None, :]
    fr_t, fi_t = fr[:, :DFT_ROWS, :], fi[:, :DFT_ROWS, :]
    g_inv = jnp.concatenate([twr_t * fr_t - twi_t * fi_t, twr_t * fi_t + twi_t * fr_t], axis=2) * (1.0 / n)
    a = jnp.arange(n1c, dtype=jnp.int32)
    s_ang = ((a[:, None] * a[None, :]) % n1c).astype(f32) * (2 * math.pi / n1c)
    eye = jnp.eye(DFT_ROWS // n1c, dtype=f32)
    sr = jnp.kron(eye, jnp.cos(s_ang))
    si = jnp.kron(eye, -jnp.sin(s_ang))
    m_fwd = jnp.block([[sr, -si], [si, sr]])
    m_inv = jnp.block([[sr, si], [-si, sr]])
    return dict(g_fwd=g_fwd.astype(bf16), g_inv=g_inv.astype(bf16), m_fwd=m_fwd.astype(bf16),
                m_inv=m_inv.astype(bf16), n1=n1c)


def _n1_major(a):
    return jnp.swapaxes(a, 0, 1)


def _n1_minor(mats, shape):
    return jnp.swapaxes(jnp.stack(mats, axis=0), 0, 1).reshape(shape)


def _fft1_body(x_ref, g_ref, o_ref, *, rpb):
    x = _n1_major(x_ref[...])
    bs = [jnp.dot(g_ref[r], x[r].astype(bf16), preferred_element_type=f32) for r in range(rpb)]
    o_ref[...] = _n1_minor(bs, o_ref.shape).astype(bf16)


def _fft1_call(x4, g_fwd):
    B, _, n1c, C = x4.shape
    rpb = min(DFT_RPB, n1c)
    cs = DFT_SLAB
    return pl.pallas_call(
        functools.partial(_fft1_body, rpb=rpb),
        grid=(B, n1c // rpb, C // cs),
        in_specs=[pl.BlockSpec((None, DFT_ROWS, rpb, cs), lambda b, j, s: (b, 0, j, s)),
                  pl.BlockSpec((rpb, 2 * DFT_N2, DFT_ROWS), lambda b, j, s: (j, 0, 0))],
        out_specs=pl.BlockSpec((None, 2, DFT_N2, rpb, cs), lambda b, j, s: (b, 0, 0, j, s)),
        out_shape=jax.ShapeDtypeStruct((B, 2, DFT_N2, n1c, C), bf16),
        compiler_params=_cparams(("parallel", "arbitrary", "arbitrary")),
        name="fft1",
    )(x4, g_fwd)


def _fft2_body(b_ref, k_ref, mf_ref, mi_ref, o_ref, *, nsub):
    for i in range(nsub):
        rows = slice(i * DFT_ROWS, (i + 1) * DFT_ROWS)
        bst = jnp.concatenate([b_ref[0, rows, :], b_ref[1, rows, :]], axis=0)
        xs = jnp.dot(mf_ref[...], bst, preferred_element_type=f32)
        xr, xi = xs[:DFT_ROWS], xs[DFT_ROWS:]
        kr, ki = k_ref[0, rows, :].astype(f32), k_ref[1, rows, :].astype(f32)
        ys = jnp.concatenate([xr * kr - xi * ki, xr * ki + xi * kr], axis=0).astype(bf16)
        cs = jnp.dot(mi_ref[...], ys, preferred_element_type=f32)
        o_ref[0, rows, :] = cs[:DFT_ROWS].astype(bf16)
        o_ref[1, rows, :] = cs[DFT_ROWS:].astype(bf16)


def _fft2_call(bs, ksp, order, m_fwd, m_inv):
    B, _, n, C = bs.shape
    rb = min(DFT_STAGE2_ROWS, n)
    blk = pl.BlockSpec((None, 2, rb, C), lambda j, b: (b, 0, j, 0))
    return pl.pallas_call(
        functools.partial(_fft2_body, nsub=rb // DFT_ROWS),
        grid=(n // rb, B),
        in_specs=[blk,
                  pl.BlockSpec((2, rb, C), lambda j, b: (0, j, order)),
                  _const_spec((2 * DFT_ROWS, 2 * DFT_ROWS)),
                  _const_spec((2 * DFT_ROWS, 2 * DFT_ROWS))],
        out_specs=blk,
        out_shape=jax.ShapeDtypeStruct(bs.shape, bf16),
        compiler_params=_cparams(("arbitrary", "arbitrary")),
        name="fft2",
    )(bs, ksp, m_fwd, m_inv)


def _fft2_filter_body(b_ref, asum_ref, mf_ref, o_ref, *, nsub):
    inv = 1.0 / asum_ref[...]
    for i in range(nsub):
        rows = slice(i * DFT_ROWS, (i + 1) * DFT_ROWS)
        bst = jnp.concatenate([b_ref[0, rows, :], b_ref[1, rows, :]], axis=0)
        xs = jnp.dot(mf_ref[...], bst, preferred_element_type=f32)
        o_ref[0, rows, :] = (xs[:DFT_ROWS] * inv).astype(bf16)
        o_ref[1, rows, :] = (xs[DFT_ROWS:] * inv).astype(bf16)


def _fft2_filter_call(bs, asum, m_fwd):
    _, n, C = bs.shape
    rb = min(DFT_STAGE2_ROWS, n)
    blk = pl.BlockSpec((2, rb, C), lambda j: (0, j, 0))
    return pl.pallas_call(
        functools.partial(_fft2_filter_body, nsub=rb // DFT_ROWS),
        grid=(n // rb,),
        in_specs=[blk, _const_spec((1, C)), _const_spec((2 * DFT_ROWS, 2 * DFT_ROWS))],
        out_specs=blk,
        out_shape=jax.ShapeDtypeStruct(bs.shape, bf16),
        compiler_params=_cparams(("arbitrary",)),
        name="fft2_filter",
    )(bs, asum, m_fwd)


def _fft3_body(c_ref, gi_ref, gate_ref, prev_ref, sk_ref, *rest, rpb, fuse_next):
    if fuse_next:
        gf_ref, z_ref, b_ref = rest
    else:
        (z_ref,) = rest
    cs = c_ref.shape[-1]
    c = _n1_major(c_ref[...].astype(f32).reshape(2 * DFT_N2, rpb, cs))
    gate = _n1_major(gate_ref[...])
    prev = _n1_major(prev_ref[...])
    sk = sk_ref[...]
    zs, bs = [], []
    for r in range(rpb):
        y = jnp.dot(gi_ref[r], c[r].astype(bf16), preferred_element_type=f32)
        z = gate[r] * (y + sk * prev[r])
        zs.append(z)
        if fuse_next:
            bs.append(jnp.dot(gf_ref[r], z.astype(bf16), preferred_element_type=f32))
    z_ref[...] = _n1_minor(zs, z_ref.shape)
    if fuse_next:
        b_ref[...] = _n1_minor(bs, b_ref.shape).astype(bf16)


def _fft3_call(cs5, g_inv, gate, prev, sk_row, g_fwd=None):
    B, _, _, n1c, C = cs5.shape
    rpb = min(DFT_RPB, n1c)
    cs = DFT_SLAB
    tblk = pl.BlockSpec((None, DFT_ROWS, rpb, cs), lambda b, j, s: (b, 0, j, s))
    sblk = pl.BlockSpec((None, 2, DFT_N2, rpb, cs), lambda b, j, s: (b, 0, 0, j, s))
    in_specs = [sblk, pl.BlockSpec((rpb, DFT_ROWS, 2 * DFT_N2), lambda b, j, s: (j, 0, 0)), tblk, tblk,
                pl.BlockSpec((1, cs), lambda b, j, s: (0, s))]
    args = [cs5, g_inv, gate, prev, sk_row]
    out_specs = [tblk]
    out_shape = [jax.ShapeDtypeStruct(gate.shape, f32)]
    fuse_next = g_fwd is not None
    if fuse_next:
        in_specs.append(pl.BlockSpec((rpb, 2 * DFT_N2, DFT_ROWS), lambda b, j, s: (j, 0, 0)))
        args.append(g_fwd)
        out_specs.append(sblk)
        out_shape.append(jax.ShapeDtypeStruct(cs5.shape, bf16))
    return pl.pallas_call(
        functools.partial(_fft3_body, rpb=rpb, fuse_next=fuse_next),
        grid=(B, n1c // rpb, C // cs),
        in_specs=in_specs,
        out_specs=out_specs,
        out_shape=out_shape,
        compiler_params=_cparams(("parallel", "arbitrary", "arbitrary")),
        name="fft3_next" if fuse_next else "fft3",
    )(*args)


def _filter_spectrum(L, tabs, filt_params):
    bs, asum = _filter_call(L, tabs["g_fwd"], *filt_params)
    return _fft2_filter_call(bs.reshape(2, 2 * L, bs.shape[-1]), asum, tabs["m_fwd"])


def _hyena_conv(v, x1, x2, ksp, skip, tabs):
    n1c = tabs["n1"]
    B, L, C = v.shape
    n = 2 * L
    v4, x14, x24 = (a.reshape(B, DFT_ROWS, n1c, C) for a in (v, x1, x2))
    s5 = (B, 2, DFT_N2, n1c, C)
    bs = _fft1_call(v4, tabs["g_fwd"])
    cs = _fft2_call(bs.reshape(B, 2, n, C), ksp, 0, tabs["m_fwd"], tabs["m_inv"])
    z1, bs = _fft3_call(cs.reshape(s5), tabs["g_inv"], x14, v4, skip[0:1], tabs["g_fwd"])
    cs = _fft2_call(bs.reshape(B, 2, n, C), ksp, 1, tabs["m_fwd"], tabs["m_inv"])
    (z2,) = _fft3_call(cs.reshape(s5), tabs["g_inv"], x24, z1, skip[1:2])
    return z2.reshape(B, L, C)


def _route(r):
    lane = lax.broadcasted_iota(jnp.int32, r.shape, 1)
    ninf = jnp.float32(-jnp.inf)
    big = jnp.int32(1 << 20)
    is_g = lane < N_GROUPS
    gmax = jnp.max(jnp.where(is_g, r, ninf), axis=-1, keepdims=True)
    gidx = jnp.min(jnp.where(jnp.logical_and(is_g, r == gmax), lane, big), axis=-1, keepdims=True)
    gw = 1.0 / jnp.sum(jnp.where(is_g, jnp.exp(r - gmax), 0.0), axis=-1, keepdims=True)
    e_lane = lane - N_GROUPS
    sel = jnp.logical_and(jnp.logical_and(e_lane >= 0, e_lane < N_EXPERTS), (e_lane >> 2) == gidx)
    le = jnp.where(sel, r, ninf)
    m1 = jnp.max(le, axis=-1, keepdims=True)
    i1 = jnp.min(jnp.where(le == m1, lane, big), axis=-1, keepdims=True)
    le2 = jnp.where(lane == i1, ninf, le)
    m2 = jnp.max(le2, axis=-1, keepdims=True)
    i2 = jnp.min(jnp.where(le2 == m2, lane, big), axis=-1, keepdims=True)
    e2 = jnp.exp(m2 - m1)
    den = 1.0 + e2
    comb = jnp.where(lane == i1, gw / den, jnp.where(lane == i2, gw * e2 / den, 0.0))
    return comb, gidx


def _mix_body(x_ref, z_ref, g0_ref, rest_ref, ga1_ref, sh2_ref, sc2_ref, n2g_ref, wba_ref, wout_ref, wr_ref, tri_ref,
              xo_ref, hs_ref, combs_ref, pmt_ref, cnt_ref):
    ya = jnp.dot(z_ref[...].astype(bf16), wba_ref[...], preferred_element_type=f32)
    merged = g0_ref[...] * ya + rest_ref[...]
    xo = x_ref[...] + ga1_ref[...] * jnp.dot(merged.astype(bf16), wout_ref[...], preferred_element_type=f32)
    xo_ref[...] = xo
    ms = jnp.mean(xo * xo, axis=-1, keepdims=True)
    h2 = xo * lax.rsqrt(ms + EPS) * n2g_ref[...]
    h2 = h2 * (1.0 + sc2_ref[...]) + sh2_ref[...]
    h_hi = h2.astype(bf16)
    h_lo = (h2 - h_hi.astype(f32)).astype(bf16)
    p_hi = jnp.dot(h_hi, wr_ref[...], preferred_element_type=f32)
    p_lo = jnp.dot(h_lo, wr_ref[:, 0:LANES], preferred_element_type=f32)
    comb, gidx = _route(p_hi[:, 0:LANES] + p_hi[:, LANES:2 * LANES] + p_lo)

    tm = comb.shape[0]
    lane = lax.broadcasted_iota(jnp.int32, comb.shape, 1)
    onehot = (lane == gidx).astype(f32)
    cum = jnp.dot(tri_ref[...], onehot.astype(bf16), preferred_element_type=f32)
    tot8 = cum[tm - 8:tm, :]
    off8 = pltpu.roll(tot8, 1, 1) + pltpu.roll(tot8, 2, 1) + pltpu.roll(tot8, 3, 1)
    rank = jnp.sum(onehot * (off8[7:8, :] + cum - 1.0), axis=-1, keepdims=True)
    slot = lax.broadcasted_iota(jnp.int32, (tm, tm), 1).astype(f32)
    pmt = (slot == rank).astype(bf16)
    pmt_ref[...] = pmt
    both = jnp.concatenate([h_hi, comb.astype(bf16)], axis=1)
    srt = lax.dot_general(pmt, both, (((0,), (0,)), ((), ())), preferred_element_type=f32).astype(bf16)
    d = h2.shape[1]
    hs_ref[...] = srt[:, 0:d]
    combs_ref[...] = srt[:, d:d + LANES]
    cnt_ref[...] = tot8[7:8, :].astype(jnp.int32)


def _mix_call(x, z, g0, rest, ga1, sh2, sc2, n2g, wba, wout, wr, tri, tm):
    B, L, D = x.shape
    nt = L // tm
    row = lambda b, i: (b, i, 0)
    vspec = pl.BlockSpec((None, 1, D), lambda b, i: (b, 0, 0))
    return pl.pallas_call(
        _mix_body,
        grid=(B, nt),
        in_specs=[
            pl.BlockSpec((None, tm, D), row),
            pl.BlockSpec((None, tm, W_MIX), row),
            pl.BlockSpec((None, tm, D), row),
            pl.BlockSpec((None, tm, D), row),
            vspec, vspec, vspec,
            _const_spec((1, D)),
            _const_spec((W_MIX, D)),
            _const_spec((D, D)),
            _const_spec((D, 2 * LANES)),
            _const_spec((tm, tm)),
        ],
        out_specs=[pl.BlockSpec((None, tm, D), row), pl.BlockSpec((None, tm, D), row),
                   pl.BlockSpec((None, tm, LANES), row), pl.BlockSpec((None, tm, tm), row),
                   pl.BlockSpec((None, None, 1, LANES), lambda b, i: (b, i, 0, 0))],
        out_shape=[jax.ShapeDtypeStruct((B, L, D), f32), jax.ShapeDtypeStruct((B, L, D), bf16),
                   jax.ShapeDtypeStruct((B, L, LANES), bf16), jax.ShapeDtypeStruct((B, L, tm), bf16),
                   jax.ShapeDtypeStruct((B, nt, 1, LANES), jnp.int32)],
        compiler_params=_cparams(("parallel", "arbitrary")),
        name="mix",
    )(x, z, g0, rest, ga1, sh2, sc2, n2g, wba, wout, wr, tri)


def _experts_body(cnt_ref, hs_ref, combs_ref, ex_ref, w1_ref, w3_ref, w2_ref, ys_ref, acc_s, *, tm, sub):
    b, sup, g = pl.program_id(0), pl.program_id(1), pl.program_id(2)
    nchunk = tm // MOE_CHUNK

    @pl.when(g == 0)
    def _():
        acc_s[...] = jnp.zeros_like(acc_s)

    def tile(t, carry):
        base = ((b * pl.num_programs(1) + sup) * sub + t) * N_GROUPS
        lo = jnp.int32(0)
        for gg in range(N_GROUPS - 1):
            lo = lo + jnp.where(gg < g, cnt_ref[base + gg], 0)
        hi = lo + cnt_ref[base + g]
        for c in range(nchunk):
            @pl.when(jnp.logical_and(lo < (c + 1) * MOE_CHUNK, hi > c * MOE_CHUNK))
            def _():
                rows = pl.ds(pl.multiple_of(t * tm + c * MOE_CHUNK, MOE_CHUNK), MOE_CHUNK)
                h = hs_ref[rows, :]
                a = jnp.dot(h, w1_ref[...], preferred_element_type=f32)
                u = jnp.dot(h, w3_ref[...], preferred_element_type=f32)
                cw = jnp.dot(combs_ref[rows, :], ex_ref[...], preferred_element_type=f32)
                hid = (a * jax.nn.sigmoid(a) * u * cw).astype(bf16)
                acc_s[rows, :] += jnp.dot(hid, w2_ref[...], preferred_element_type=f32)
        return carry

    lax.fori_loop(0, sub, tile, 0)

    @pl.when(g == pl.num_programs(2) - 1)
    def _():
        ys_ref[...] = acc_s[...].astype(bf16)


def _experts_call(cnt, hs, combs, ex, w1, w3, w2, tm):
    B, L, D = hs.shape
    sub = min(MOE_SUB, L // tm)
    rows = sub * tm
    blk = lambda b, s, g, cnt: (b, s, 0)
    return pl.pallas_call(
        functools.partial(_experts_body, tm=tm, sub=sub),
        grid_spec=pltpu.PrefetchScalarGridSpec(
            num_scalar_prefetch=1,
            grid=(B, L // rows, N_GROUPS),
            in_specs=[
                pl.BlockSpec((None, rows, D), blk),
                pl.BlockSpec((None, rows, LANES), blk),
                pl.BlockSpec((None, LANES, GROUP_HID), lambda b, s, g, cnt: (g, 0, 0)),
                pl.BlockSpec((D, GROUP_HID), lambda b, s, g, cnt: (0, g)),
                pl.BlockSpec((D, GROUP_HID), lambda b, s, g, cnt: (0, g)),
                pl.BlockSpec((GROUP_HID, D), lambda b, s, g, cnt: (g, 0)),
            ],
            out_specs=pl.BlockSpec((None, rows, D), blk),
            scratch_shapes=[pltpu.VMEM((rows, D), f32)],
        ),
        out_shape=jax.ShapeDtypeStruct((B, L, D), bf16),
        compiler_params=_cparams(("arbitrary", "arbitrary", "arbitrary")),
        name="experts",
    )(cnt, hs, combs, ex, w1, w3, w2)


def _unsort_body(ys_ref, pmt_ref, xo_ref, ga2_ref, fg_ref, o_ref, *, final_norm):
    y = xo_ref[...] + ga2_ref[...] * jnp.dot(pmt_ref[...], ys_ref[...], preferred_element_type=f32)
    if final_norm:
        ms = jnp.mean(y * y, axis=-1, keepdims=True)
        y = y * lax.rsqrt(ms + EPS) * fg_ref[...]
    o_ref[...] = y


def _unsort_call(ys, pmt, xo, ga2, fg, tm, final_norm):
    B, L, D = xo.shape
    row = lambda b, i: (b, i, 0)
    return pl.pallas_call(
        functools.partial(_unsort_body, final_norm=final_norm),
        grid=(B, L // tm),
        in_specs=[pl.BlockSpec((None, tm, D), row), pl.BlockSpec((None, tm, tm), row), pl.BlockSpec((None, tm, D), row),
                  pl.BlockSpec((None, 1, D), lambda b, i: (b, 0, 0)), _const_spec((1, D))],
        out_specs=pl.BlockSpec((None, tm, D), row),
        out_shape=jax.ShapeDtypeStruct((B, L, D), f32),
        compiler_params=_cparams(("parallel", "arbitrary")),
        name="unsort",
    )(ys, pmt, xo, ga2, fg)


def _pad_to(a, shape):
    return jnp.pad(a, [(0, s - d) for d, s in zip(a.shape, shape)])


def _prep_layer(l, p):
    max_decay = math.log(HY_TARGET) / HY_FAST_DECAY
    min_decay = math.log(HY_TARGET) / HY_SLOW_DECAY
    deltas = jnp.abs(jnp.linspace(min_decay, max_decay, W_MIX, dtype=f32))
    router = jnp.concatenate([p["router_g"][l], p["router_e"][l]], axis=1)
    lanes = jnp.arange(LANES)[None, :, None]
    cols = jnp.arange(GROUP_HID)[None, None, :]
    grp = jnp.arange(N_GROUPS)[:, None, None]
    expand = (lanes == N_GROUPS + EXP_PER_GROUP * grp + cols // D_EXPERT).astype(bf16)
    router = _pad_to(router, (D_MODEL, LANES))
    router_hi = router.astype(bf16)
    router_lo = (router - router_hi.astype(f32)).astype(bf16)
    zh = jnp.zeros((HY_HID, HY_HID), f32)
    w1 = _pad_to(p["hy_w1"][l], (LANES, HY_HID))
    zw1 = jnp.zeros_like(w1)
    w1_pair = jnp.block([[w1, zw1], [zw1, w1]])
    w2_pair = jnp.block([[p["hy_w2"][l], zh], [zh, p["hy_w2"][l]]])
    pair = lambda a: jnp.concatenate([a, a])[None]
    cw = HY_ORDER * W_MIX
    wo = p["hy_w_out"][l]
    zwo = jnp.zeros((HY_HID, cw), f32)
    wo_f = jnp.concatenate([wo[:, :cw], zwo], axis=0).astype(bf16)
    wo_b = jnp.concatenate([zwo, wo[:, cw:]], axis=0).astype(bf16)
    return dict(
        norm1_g=p["norm1_g"][l][None], norm2_g=p["norm2_g"][l][None],
        w_in=p["w_in"][l].astype(bf16), hy_conv_w=p["hy_conv_w"][l], hy_skip=p["hy_skip"][l],
        pool_w=p["pool_w"][l].astype(bf16), pool_scale=p["pool_scale"][l][None], sc_conv_w=p["sc_conv_w"][l],
        w_br_a=p["w_br_a"][l].astype(bf16), w_br_b=p["w_br_b"][l].astype(bf16), w_br_c=p["w_br_c"][l].astype(bf16),
        w_out=p["w_out"][l].astype(bf16),
        router=jnp.concatenate([router_hi, router_lo], axis=1), expand=expand,
        moe_w1=p["moe_w1"][l].astype(bf16), moe_w3=p["moe_w3"][l].astype(bf16), moe_w2=p["moe_w2"][l].astype(bf16),
        filt=(w1_pair, pair(p["hy_b1"][l]), w2_pair, pair(p["hy_b2"][l]), pair(p["hy_freq"][l]), wo_f, wo_b,
              pair(deltas)),
    )


def _tile(L, want):
    return want if L % want == 0 else L


def _encoder_layer(x, mod, lp, ksp, tabs, final_g, final_norm):
    B, L, D = x.shape
    sh1, sc1, ga1, sh2, sc2, ga2 = (m[:, None, :] for m in jnp.split(mod, 6, axis=-1))
    v, x1, x2, g0, rest = _proj_call(x, sh1, sc1, lp["norm1_g"], lp["w_in"], lp["hy_conv_w"], lp["pool_w"],
                                     lp["pool_scale"], lp["sc_conv_w"], lp["w_br_b"], lp["w_br_c"], _tile(L, 512))
    z = _hyena_conv(v, x1, x2, ksp, lp["hy_skip"], tabs)
    tm = _tile(L, MOE_TILE)
    tri = jnp.tri(tm, dtype=bf16)
    xo, hs, combs, pmt, cnt = _mix_call(x, z, g0, rest, ga1, sh2, sc2, lp["norm2_g"], lp["w_br_a"], lp["w_out"],
                                        lp["router"], tri, tm)
    ys = _experts_call(cnt[:, :, 0, :N_GROUPS].reshape(-1), hs, combs, lp["expand"], lp["moe_w1"], lp["moe_w3"],
                       lp["moe_w2"], tm)
    return _unsort_call(ys, pmt, xo, ga2, final_g, tm, final_norm)


def _forward(xs, cs, p, final_g):
    depth = p["w_in"].shape[0]
    nb = [c.shape[0] for c in cs]
    rows = -(-sum(nb) // 8) * 8
    c_all = _pad_to(jnp.concatenate(cs, axis=0), (rows, D_MODEL))
    lens = sorted({x.shape[1] for x in xs})
    tabs = {L: _dft_tables(L) for L in lens}
    fg = final_g[None]
    for l in range(depth):
        lp = _prep_layer(l, p)
        mod = _mod_call(c_all, p["ada_w"][l], p["ada_b"][l][None])
        ksp = {L: _filter_spectrum(L, tabs[L], lp["filt"]) for L in lens}
        off = 0
        out = []
        for x, n in zip(xs, nb):
            L = x.shape[1]
            out.append(_encoder_layer(x, mod[off:off + n], lp, ksp[L], tabs[L], fg, l == depth - 1))
            off += n
        xs = out
    return xs


def kernel(x_prompt, x_sample, c_prompt, c_sample, ada_w, ada_b, norm1_g, norm2_g, w_in, hy_conv_w, hy_skip, hy_w1, hy_b1, hy_w2, hy_b2, hy_w_out, hy_freq, pool_w, pool_scale, sc_conv_w, w_br_a, w_br_b, w_br_c, w_out, router_g, router_e, moe_w1, moe_w3, moe_w2, final_g):
    p = dict(ada_w=ada_w, ada_b=ada_b, norm1_g=norm1_g, norm2_g=norm2_g, w_in=w_in, hy_conv_w=hy_conv_w,
             hy_skip=hy_skip, hy_w1=hy_w1, hy_b1=hy_b1, hy_w2=hy_w2, hy_b2=hy_b2, hy_w_out=hy_w_out, hy_freq=hy_freq,
             pool_w=pool_w, pool_scale=pool_scale, sc_conv_w=sc_conv_w, w_br_a=w_br_a, w_br_b=w_br_b, w_br_c=w_br_c,
             w_out=w_out, router_g=router_g, router_e=router_e, moe_w1=moe_w1, moe_w3=moe_w3, moe_w2=moe_w2)
    y_prompt, y_sample = _forward([x_prompt, x_sample], [c_prompt, c_sample], p, final_g)
    return (y_prompt, y_sample)
```

```python
import functools
import math

import jax
import jax.numpy as jnp
from jax import lax
from jax.experimental import pallas as pl
from jax.experimental.pallas import tpu as pltpu

f32 = jnp.float32
bf16 = jnp.bfloat16
HIGHEST = lax.Precision.HIGHEST

D_MODEL = 1024
DEPTH = 2
W_MIX = 512
HY_ORDER = 2
HY_BANDS = 16
HY_HID = 64
HY_FAST_DECAY = 0.3
HY_SLOW_DECAY = 1.5
HY_TARGET = 1e-2
POOL_WINDOWS = (2, 4, 8, 16)
POOL_GROUP = W_MIX // len(POOL_WINDOWS)
COL_HY = 3 * W_MIX
COL_POOL = W_MIX
COL_SC = 3 * W_MIX
COL_GATE = 3 * D_MODEL
OFF_POOL = COL_HY
OFF_SC = COL_HY + COL_POOL
OFF_GATE = COL_HY + COL_POOL + COL_SC
PROJ_COLS = OFF_GATE + COL_GATE
N_GROUPS = 4
EXP_PER_GROUP = 4
N_EXPERTS = N_GROUPS * EXP_PER_GROUP
D_EXPERT = 256
GROUP_HID = EXP_PER_GROUP * D_EXPERT
EPS = 1e-6

HALO = 8
DFT_N2 = 256
DFT_ROWS = 128
DFT_RPB = 16
DFT_SLAB = 256
DFT_STAGE2_ROWS = 1024
MOE_TILE = 512
MOE_CHUNK = 128
MOE_SUB = 2
LANES = 128
VMEM_LIMIT = 56 * 1024 * 1024


def _cparams(sem):
    return pltpu.CompilerParams(dimension_semantics=sem, vmem_limit_bytes=VMEM_LIMIT)


def _const_spec(shape):
    nd = len(shape)
    return pl.BlockSpec(shape, lambda *_: (0,) * nd, pipeline_mode=pl.Buffered(1))


def _mod_body(c_ref, w_ref, b_ref, o_ref):
    c = c_ref[...]
    s = c * jax.nn.sigmoid(c)
    o_ref[...] = jnp.dot(s, w_ref[...], preferred_element_type=f32, precision=HIGHEST) + b_ref[...]


def _mod_call(c_all, ada_w, ada_b):
    rows = c_all.shape[0]
    depth = ada_w.shape[0]
    tn = 1536
    return pl.pallas_call(
        _mod_body,
        grid=(depth, 6 * D_MODEL // tn),
        in_specs=[pl.BlockSpec((rows, D_MODEL), lambda l, j: (0, 0)),
                  pl.BlockSpec((None, D_MODEL, tn), lambda l, j: (l, 0, j)),
                  pl.BlockSpec((None, 1, tn), lambda l, j: (l, 0, j))],
        out_specs=pl.BlockSpec((None, rows, tn), lambda l, j: (l, 0, j)),
        out_shape=jax.ShapeDtypeStruct((depth, rows, 6 * D_MODEL), f32),
        compiler_params=_cparams(("arbitrary", "arbitrary")),
        name="mod",
    )(c_all, ada_w, ada_b[:, None, :])


def _proj_body(xm_ref, xp_ref, xn_ref, sh_ref, sc_ref, g_ref, win_ref, hyw_ref, pw_ref, ps_ref, scw_ref,
               wbb_ref, wbc_ref, v_ref, x1_ref, x2_ref, g0_ref, rest_ref, *, tm, seq_len):
    i = pl.program_id(1)
    nt = pl.num_programs(1)
    rt = tm + 2 * HALO
    ctr = slice(HALO, HALO + tm)

    def modulated(x):
        ms = jnp.mean(x * x, axis=-1, keepdims=True)
        h = x * lax.rsqrt(ms + EPS) * g_ref[...]
        return h * (1.0 + sc_ref[...]) + sh_ref[...]

    hp = jnp.where(i > 0, modulated(xp_ref[...]), 0.0)
    hn = jnp.where(i < nt - 1, modulated(xn_ref[...]), 0.0)
    hc = modulated(xm_ref[...])
    hb = jnp.concatenate([hp, hc, hn], axis=0).astype(bf16)
    hcb = hc.astype(bf16)

    def down(a, s):
        return pltpu.roll(a, s, 0)

    def up(a, s):
        return pltpu.roll(a, rt - s, 0)

    u = jnp.dot(hb, win_ref[:, 0:COL_HY], preferred_element_type=f32)
    w = hyw_ref[...]
    uc = (down(u, 1) * w[0:1] + u * w[1:2] + up(u, 1) * w[2:3])[ctr]
    v_ref[...] = uc[:, 0:W_MIX]
    x1_ref[...] = uc[:, W_MIX:2 * W_MIX]
    x2_ref[...] = uc[:, 2 * W_MIX:3 * W_MIX]

    q = jnp.dot(hb, win_ref[:, OFF_POOL:OFF_POOL + COL_POOL], preferred_element_type=f32)
    s2 = q + down(q, 1)
    s4 = s2 + down(s2, 2)
    s8 = s4 + down(s4, 4)
    s16 = s8 + down(s8, 8)
    tpos = i * tm + lax.broadcasted_iota(jnp.int32, (tm, 1), 0)
    pooled = []
    for g, (win, ssum) in enumerate(zip(POOL_WINDOWS, (s2, s4, s8, s16))):
        lo = win // 2
        hi = win - 1 - lo
        lanes = slice(g * POOL_GROUP, (g + 1) * POOL_GROUP)
        ws = ssum[:, lanes]
        if hi > 0:
            ws = up(ws, hi)
        cnt = (jnp.minimum(tpos + hi + 1, seq_len) - jnp.maximum(tpos - lo, 0)).astype(f32)
        p = ws[ctr] / cnt - q[ctr, lanes]
        pooled.append(jnp.dot(p.astype(bf16), pw_ref[g], preferred_element_type=f32))
    yb_in = jnp.concatenate(pooled, axis=1) * ps_ref[...]
    yb = jnp.dot(yb_in.astype(bf16), wbb_ref[...], preferred_element_type=f32)

    us = jnp.dot(hb, win_ref[:, OFF_SC:OFF_SC + COL_SC], preferred_element_type=f32)
    cx = us[:, W_MIX:2 * W_MIX] * us[:, 2 * W_MIX:3 * W_MIX]
    sw = scw_ref[...]
    dw = down(cx, 1) * sw[0:1] + cx * sw[1:2] + up(cx, 1) * sw[2:3]
    sc_out = (us[:, 0:W_MIX] * dw)[ctr]
    yc = jnp.dot(sc_out.astype(bf16), wbc_ref[...], preferred_element_type=f32)

    gt = 0.5 * jnp.tanh(0.5 * jnp.dot(hcb, win_ref[:, OFF_GATE:PROJ_COLS], preferred_element_type=f32)) + 0.5
    g0_ref[...] = gt[:, 0:D_MODEL]
    rest_ref[...] = gt[:, D_MODEL:2 * D_MODEL] * yb + gt[:, 2 * D_MODEL:3 * D_MODEL] * yc


def _proj_call(x, sh, sc, g1, win, hyw, pw, ps, scw, wbb, wbc, tm):
    B, L, D = x.shape
    nt = L // tm
    hb = tm // HALO
    row = lambda b, i: (b, i, 0)
    vec = lambda b, i: (b, 0, 0)
    out_w = jax.ShapeDtypeStruct((B, L, W_MIX), f32)
    out_d = jax.ShapeDtypeStruct((B, L, D), f32)
    return pl.pallas_call(
        functools.partial(_proj_body, tm=tm, seq_len=L),
        grid=(B, nt),
        in_specs=[
            pl.BlockSpec((None, tm, D), row),
            pl.BlockSpec((None, HALO, D), lambda b, i: (b, jnp.maximum(i * hb - 1, 0), 0)),
            pl.BlockSpec((None, HALO, D), lambda b, i: (b, jnp.minimum((i + 1) * hb, L // HALO - 1), 0)),
            pl.BlockSpec((None, 1, D), vec),
            pl.BlockSpec((None, 1, D), vec),
            _const_spec((1, D)),
            _const_spec((D, PROJ_COLS)),
            _const_spec((3, COL_HY)),
            _const_spec((len(POOL_WINDOWS), POOL_GROUP, POOL_GROUP)),
            _const_spec((1, W_MIX)),
            _const_spec((3, W_MIX)),
            _const_spec((W_MIX, D)),
            _const_spec((W_MIX, D)),
        ],
        out_specs=[pl.BlockSpec((None, tm, W_MIX), row)] * 3 + [pl.BlockSpec((None, tm, D), row)] * 2,
        out_shape=[out_w, out_w, out_w, out_d, out_d],
        compiler_params=_cparams(("parallel", "arbitrary")),
        name="proj",
    )(x, x, x, sh, sc, g1, win, hyw, pw, ps, scw, wbb, wbc)


def _filter_body(ca_ref, sa_ref, cb_ref, sb_ref, w1_ref, b1_ref, w2_ref, b2_ref, fr_ref, wo0_ref, wo1_ref, dl_ref, g_ref,
                 o_ref, asum_ref, h_s, *, rpb, seq_len, n1c):
    j = pl.program_id(0)
    s = pl.program_id(1)
    L = seq_len
    n2 = lax.broadcasted_iota(jnp.int32, (DFT_N2, 1), 0)
    fwd = n2 < DFT_ROWS

    def slot(r):
        pos = (j * rpb + r) + n1c * n2
        return pos, jnp.where(fwd, pos, 2 * L - pos).astype(f32)

    @pl.when(s == 0)
    def _():
        lane = lax.broadcasted_iota(jnp.int32, (DFT_N2, LANES), 1)
        fr = fr_ref[...]
        cb, sb = cb_ref[...], sb_ref[...]
        zs = []
        for r in range(rpb):
            _, lag = slot(r)
            ca = ca_ref[r:r + 1, :]
            sa = jnp.where(fwd, sa_ref[r:r + 1, :], -sa_ref[r:r + 1, :])
            cos_t = ca * cb - sa * sb
            sin_t = sa * cb + ca * sb
            z = jnp.where(lane == 0, lag / (L - 1), jnp.where(lane <= HY_BANDS, cos_t, -sin_t))
            zs.append(jnp.concatenate([z[:DFT_ROWS], z[DFT_ROWS:]], axis=1))
        zz = jnp.concatenate(zs, axis=0)
        h = jnp.sin(fr * (jnp.dot(zz, w1_ref[...], preferred_element_type=f32, precision=HIGHEST) + b1_ref[...]))
        h_s[...] = jnp.sin(fr * (jnp.dot(h, w2_ref[...], preferred_element_type=f32, precision=HIGHEST) + b2_ref[...]))

    hb = h_s[...].astype(bf16)
    ho_f = jnp.dot(hb, wo0_ref[...], preferred_element_type=f32)
    ho_b = jnp.dot(hb, wo1_ref[...], preferred_element_type=f32)
    asum = jnp.zeros(asum_ref.shape[1:], f32)
    bs = []
    for r in range(rpb):
        pos, lag = slot(r)
        rows = slice(r * DFT_ROWS, (r + 1) * DFT_ROWS)
        ho = jnp.concatenate([ho_f[rows], ho_b[rows]], axis=0)
        k = jnp.where(pos == L, 0.0, ho * jnp.exp(-(lag / (L - 1)) * dl_ref[...]))
        asum = asum + jnp.sum(jnp.abs(k), axis=0, keepdims=True)
        bs.append(jnp.dot(g_ref[r], k.astype(bf16), preferred_element_type=f32))
    o_ref[...] = jnp.swapaxes(jnp.stack(bs, axis=0), 0, 1).reshape(o_ref.shape).astype(bf16)

    @pl.when(j == 0)
    def _():
        asum_ref[s] = asum

    @pl.when(j > 0)
    def _():
        asum_ref[s] += asum


def _filter_tables(L):
    n1c = 2 * L // DFT_N2
    bands = jnp.linspace(1e-4, HY_BANDS - 1, HY_BANDS, dtype=f32)
    brow = jnp.zeros((LANES,), f32).at[1:1 + HY_BANDS].set(bands).at[1 + HY_BANDS:1 + 2 * HY_BANDS].set(bands)
    used = (jnp.arange(LANES) >= 1) & (jnp.arange(LANES) <= 2 * HY_BANDS)
    n2 = jnp.arange(DFT_N2)
    part_a = jnp.arange(n1c).astype(f32)
    part_b = (n1c * jnp.where(n2 < DFT_ROWS, n2, DFT_N2 - n2)).astype(f32)

    def cs(part):
        ang = (2 * math.pi / L) * part[:, None] * brow[None, :]
        return jnp.where(used, jnp.cos(ang), 0.0), jnp.where(used, jnp.sin(ang), 0.0)

    return cs(part_a) + cs(part_b)


def _filter_call(L, g_fwd, w1p, b1p, w2p, b2p, frp, wo_f, wo_b, dl_row):
    n1c = 2 * L // DFT_N2
    rpb = min(DFT_RPB, n1c)
    cw = HY_ORDER * W_MIX
    cs = DFT_SLAB
    ns = cw // cs
    ca, sa, cb, sb = _filter_tables(L)
    bs, asum = pl.pallas_call(
        functools.partial(_filter_body, rpb=rpb, seq_len=L, n1c=n1c),
        grid=(n1c // rpb, ns),
        in_specs=[pl.BlockSpec((rpb, LANES), lambda j, s: (j, 0)), pl.BlockSpec((rpb, LANES), lambda j, s: (j, 0)),
                  _const_spec((DFT_N2, LANES)), _const_spec((DFT_N2, LANES)),
                  _const_spec((2 * LANES, LANES)), _const_spec((1, LANES)),
                  _const_spec((LANES, LANES)), _const_spec((1, LANES)), _const_spec((1, LANES)),
                  pl.BlockSpec((LANES, cs), lambda j, s: (0, s)),
                  pl.BlockSpec((LANES, cs), lambda j, s: (0, s)),
                  pl.BlockSpec((1, cs), lambda j, s: (0, s)),
                  pl.BlockSpec((rpb, 2 * DFT_N2, DFT_N2), lambda j, s: (j, 0, 0))],
        out_specs=[pl.BlockSpec((2, DFT_N2, rpb, cs), lambda j, s: (0, 0, j, s)),
                   pl.BlockSpec((ns, 1, cs), lambda j, s: (0, 0, 0))],
        out_shape=[jax.ShapeDtypeStruct((2, DFT_N2, n1c, cw), bf16), jax.ShapeDtypeStruct((ns, 1, cs), f32)],
        scratch_shapes=[pltpu.VMEM((rpb * DFT_ROWS, LANES), f32)],
        compiler_params=_cparams(("arbitrary", "arbitrary")),
        name="filt",
    )(ca, sa, cb, sb, w1p, b1p, w2p, b2p, frp, wo_f, wo_b, dl_row, g_fwd)
    return bs, asum.reshape(1, cw)


def _dft_tables(L):
    n = 2 * L
    n1c = n // DFT_N2
    k2 = jnp.arange(DFT_N2, dtype=jnp.int32)
    tw_ang = ((jnp.arange(n1c, dtype=jnp.int32)[:, None] * k2[None, :]) % n).astype(f32) * (2 * math.pi / n)
    f_ang = ((k2[:, None] * k2[None, :]) % DFT_N2).astype(f32) * (2 * math.pi / DFT_N2)
    twr, twi = jnp.cos(tw_ang)[:, :, None], -jnp.sin(tw_ang)[:, :, None]
    fr, fi = jnp.cos(f_ang)[None], -jnp.sin(f_ang)[None]
    gr = twr * fr - twi * fi
    gi = twr * fi + twi * fr
    g_fwd = jnp.concatenate([gr, gi], axis=1)
    twr_t, twi_t = jnp.cos(tw_ang)[:, None, :], -jnp.sin(tw_ang)[:, None, :]
    fr_t, fi_t = fr[:, :DFT_ROWS, :], fi[:, :DFT_ROWS, :]
    g_inv = jnp.concatenate([twr_t * fr_t - twi_t * fi_t, twr_t * fi_t + twi_t * fr_t], axis=2) * (1.0 / n)
    a = jnp.arange(n1c, dtype=jnp.int32)
    s_ang = ((a[:, None] * a[None, :]) % n1c).astype(f32) * (2 * math.pi / n1c)
    eye = jnp.eye(DFT_ROWS // n1c, dtype=f32)
    sr = jnp.kron(eye, jnp.cos(s_ang))
    si = jnp.kron(eye, -jnp.sin(s_ang))
    m_fwd = jnp.block([[sr, -si], [si, sr]])
    m_inv = jnp.block([[sr, si], [-si, sr]])
    return dict(g_fwd=g_fwd.astype(bf16), g_inv=g_inv.astype(bf16), m_fwd=m_fwd.astype(bf16),
                m_inv=m_inv.astype(bf16), n1=n1c)


def _n1_major(a):
    return jnp.swapaxes(a, 0, 1)


def _n1_minor(mats, shape):
    return jnp.swapaxes(jnp.stack(mats, axis=0), 0, 1).reshape(shape)


def _fft1_body(x_ref, g_ref, o_ref, *, rpb):
    x = _n1_major(x_ref[...])
    bs = [jnp.dot(g_ref[r], x[r].astype(bf16), preferred_element_type=f32) for r in range(rpb)]
    o_ref[...] = _n1_minor(bs, o_ref.shape).astype(bf16)


def _fft1_call(x4, g_fwd):
    B, _, n1c, C = x4.shape
    rpb = min(DFT_RPB, n1c)
    cs = DFT_SLAB
    return pl.pallas_call(
        functools.partial(_fft1_body, rpb=rpb),
        grid=(B, n1c // rpb, C // cs),
        in_specs=[pl.BlockSpec((None, DFT_ROWS, rpb, cs), lambda b, j, s: (b, 0, j, s)),
                  pl.BlockSpec((rpb, 2 * DFT_N2, DFT_ROWS), lambda b, j, s: (j, 0, 0))],
        out_specs=pl.BlockSpec((None, 2, DFT_N2, rpb, cs), lambda b, j, s: (b, 0, 0, j, s)),
        out_shape=jax.ShapeDtypeStruct((B, 2, DFT_N2, n1c, C), bf16),
        compiler_params=_cparams(("parallel", "arbitrary", "arbitrary")),
        name="fft1",
    )(x4, g_fwd)


def _fft2_body(b_ref, kb_ref, asum_ref, mf_ref, mi_ref, o_ref, k_s, *, nsub):
    def stacked(ref, rows):
        return jnp.concatenate([ref[0, rows, :], ref[1, rows, :]], axis=0)

    @pl.when(pl.program_id(1) == 0)
    def _():
        inv = 1.0 / asum_ref[...]
        for i in range(nsub):
            rows = slice(i * DFT_ROWS, (i + 1) * DFT_ROWS)
            ks = jnp.dot(mf_ref[...], stacked(kb_ref, rows), preferred_element_type=f32)
            k_s[0, rows, :] = ks[:DFT_ROWS] * inv
            k_s[1, rows, :] = ks[DFT_ROWS:] * inv

    for i in range(nsub):
        rows = slice(i * DFT_ROWS, (i + 1) * DFT_ROWS)
        xs = jnp.dot(mf_ref[...], stacked(b_ref, rows), preferred_element_type=f32)
        xr, xi = xs[:DFT_ROWS], xs[DFT_ROWS:]
        kr, ki = k_s[0, rows, :], k_s[1, rows, :]
        ys = jnp.concatenate([xr * kr - xi * ki, xr * ki + xi * kr], axis=0).astype(bf16)
        cs = jnp.dot(mi_ref[...], ys, preferred_element_type=f32)
        o_ref[0, rows, :] = cs[:DFT_ROWS].astype(bf16)
        o_ref[1, rows, :] = cs[DFT_ROWS:].astype(bf16)


def _fft2_call(bs, kb, asum, order, m_fwd, m_inv):
    B, _, n, C = bs.shape
    rb = min(DFT_STAGE2_ROWS, n)
    blk = pl.BlockSpec((None, 2, rb, C), lambda j, b: (b, 0, j, 0))
    return pl.pallas_call(
        functools.partial(_fft2_body, nsub=rb // DFT_ROWS),
        grid=(n // rb, B),
        in_specs=[blk,
                  pl.BlockSpec((2, rb, C), lambda j, b: (0, j, order)),
                  pl.BlockSpec((1, C), lambda j, b: (0, order)),
                  _const_spec((2 * DFT_ROWS, 2 * DFT_ROWS)),
                  _const_spec((2 * DFT_ROWS, 2 * DFT_ROWS))],
        out_specs=blk,
        out_shape=jax.ShapeDtypeStruct(bs.shape, bf16),
        scratch_shapes=[pltpu.VMEM((2, rb, C), f32)],
        compiler_params=_cparams(("arbitrary", "arbitrary")),
        name="fft2",
    )(bs, kb, asum, m_fwd, m_inv)


def _fft3_body(c_ref, gi_ref, gate_ref, prev_ref, sk_ref, *rest, rpb, fuse_next):
    if fuse_next:
        gf_ref, z_ref, b_ref = rest
    else:
        (z_ref,) = rest
    cs = c_ref.shape[-1]
    c = _n1_major(c_ref[...].astype(f32).reshape(2 * DFT_N2, rpb, cs))
    gate = _n1_major(gate_ref[...])
    prev = _n1_major(prev_ref[...])
    sk = sk_ref[...]
    zs, bs = [], []
    for r in range(rpb):
        y = jnp.dot(gi_ref[r], c[r].astype(bf16), preferred_element_type=f32)
        z = gate[r] * (y + sk * prev[r])
        zs.append(z)
        if fuse_next:
            bs.append(jnp.dot(gf_ref[r], z.astype(bf16), preferred_element_type=f32))
    z_ref[...] = _n1_minor(zs, z_ref.shape)
    if fuse_next:
        b_ref[...] = _n1_minor(bs, b_ref.shape).astype(bf16)


def _fft3_call(cs5, g_inv, gate, prev, sk_row, g_fwd=None):
    B, _, _, n1c, C = cs5.shape
    rpb = min(DFT_RPB, n1c)
    cs = DFT_SLAB
    tblk = pl.BlockSpec((None, DFT_ROWS, rpb, cs), lambda b, j, s: (b, 0, j, s))
    sblk = pl.BlockSpec((None, 2, DFT_N2, rpb, cs), lambda b, j, s: (b, 0, 0, j, s))
    in_specs = [sblk, pl.BlockSpec((rpb, DFT_ROWS, 2 * DFT_N2), lambda b, j, s: (j, 0, 0)), tblk, tblk,
                pl.BlockSpec((1, cs), lambda b, j, s: (0, s))]
    args = [cs5, g_inv, gate, prev, sk_row]
    out_specs = [tblk]
    out_shape = [jax.ShapeDtypeStruct(gate.shape, f32)]
    fuse_next = g_fwd is not None
    if fuse_next:
        in_specs.append(pl.BlockSpec((rpb, 2 * DFT_N2, DFT_ROWS), lambda b, j, s: (j, 0, 0)))
        args.append(g_fwd)
        out_specs.append(sblk)
        out_shape.append(jax.ShapeDtypeStruct(cs5.shape, bf16))
    return pl.pallas_call(
        functools.partial(_fft3_body, rpb=rpb, fuse_next=fuse_next),
        grid=(B, n1c // rpb, C // cs),
        in_specs=in_specs,
        out_specs=out_specs,
        out_shape=out_shape,
        compiler_params=_cparams(("parallel", "arbitrary", "arbitrary")),
        name="fft3_next" if fuse_next else "fft3",
    )(*args)


def _filter_spectrum(L, tabs, filt_params):
    bs, asum = _filter_call(L, tabs["g_fwd"], *filt_params)
    return bs.reshape(2, 2 * L, bs.shape[-1]), asum


def _hyena_conv(v, x1, x2, ksp, skip, tabs):
    n1c = tabs["n1"]
    B, L, C = v.shape
    n = 2 * L
    kb, asum = ksp
    v4, x14, x24 = (a.reshape(B, DFT_ROWS, n1c, C) for a in (v, x1, x2))
    s5 = (B, 2, DFT_N2, n1c, C)
    bs = _fft1_call(v4, tabs["g_fwd"])
    cs = _fft2_call(bs.reshape(B, 2, n, C), kb, asum, 0, tabs["m_fwd"], tabs["m_inv"])
    z1, bs = _fft3_call(cs.reshape(s5), tabs["g_inv"], x14, v4, skip[0:1], tabs["g_fwd"])
    cs = _fft2_call(bs.reshape(B, 2, n, C), kb, asum, 1, tabs["m_fwd"], tabs["m_inv"])
    (z2,) = _fft3_call(cs.reshape(s5), tabs["g_inv"], x24, z1, skip[1:2])
    return z2.reshape(B, L, C)


def _route(r):
    lane = lax.broadcasted_iota(jnp.int32, r.shape, 1)
    ninf = jnp.float32(-jnp.inf)
    big = jnp.int32(1 << 20)
    is_g = lane < N_GROUPS
    gmax = jnp.max(jnp.where(is_g, r, ninf), axis=-1, keepdims=True)
    gidx = jnp.min(jnp.where(jnp.logical_and(is_g, r == gmax), lane, big), axis=-1, keepdims=True)
    gw = 1.0 / jnp.sum(jnp.where(is_g, jnp.exp(r - gmax), 0.0), axis=-1, keepdims=True)
    e_lane = lane - N_GROUPS
    sel = jnp.logical_and(jnp.logical_and(e_lane >= 0, e_lane < N_EXPERTS), (e_lane >> 2) == gidx)
    le = jnp.where(sel, r, ninf)
    m1 = jnp.max(le, axis=-1, keepdims=True)
    i1 = jnp.min(jnp.where(le == m1, lane, big), axis=-1, keepdims=True)
    le2 = jnp.where(lane == i1, ninf, le)
    m2 = jnp.max(le2, axis=-1, keepdims=True)
    i2 = jnp.min(jnp.where(le2 == m2, lane, big), axis=-1, keepdims=True)
    e2 = jnp.exp(m2 - m1)
    den = 1.0 + e2
    comb = jnp.where(lane == i1, gw / den, jnp.where(lane == i2, gw * e2 / den, 0.0))
    return comb, gidx


def _mix_body(x_ref, z_ref, g0_ref, rest_ref, ga1_ref, sh2_ref, sc2_ref, n2g_ref, wba_ref, wout_ref, wr_ref, tri_ref,
              xo_ref, hs_ref, combs_ref, pmt_ref, cnt_ref):
    ya = jnp.dot(z_ref[...].astype(bf16), wba_ref[...], preferred_element_type=f32)
    merged = g0_ref[...] * ya + rest_ref[...]
    xo = x_ref[...] + ga1_ref[...] * jnp.dot(merged.astype(bf16), wout_ref[...], preferred_element_type=f32)
    xo_ref[...] = xo
    ms = jnp.mean(xo * xo, axis=-1, keepdims=True)
    h2 = xo * lax.rsqrt(ms + EPS) * n2g_ref[...]
    h2 = h2 * (1.0 + sc2_ref[...]) + sh2_ref[...]
    h_hi = h2.astype(bf16)
    h_lo = (h2 - h_hi.astype(f32)).astype(bf16)
    p_hi = jnp.dot(h_hi, wr_ref[...], preferred_element_type=f32)
    p_lo = jnp.dot(h_lo, wr_ref[:, 0:LANES], preferred_element_type=f32)
    comb, gidx = _route(p_hi[:, 0:LANES] + p_hi[:, LANES:2 * LANES] + p_lo)

    tm = comb.shape[0]
    lane = lax.broadcasted_iota(jnp.int32, comb.shape, 1)
    onehot = (lane == gidx).astype(f32)
    cum = jnp.dot(tri_ref[...], onehot.astype(bf16), preferred_element_type=f32)
    tot8 = cum[tm - 8:tm, :]
    off8 = pltpu.roll(tot8, 1, 1) + pltpu.roll(tot8, 2, 1) + pltpu.roll(tot8, 3, 1)
    rank = jnp.sum(onehot * (off8[7:8, :] + cum - 1.0), axis=-1, keepdims=True)
    slot = lax.broadcasted_iota(jnp.int32, (tm, tm), 1).astype(f32)
    pmt = (slot == rank).astype(bf16)
    pmt_ref[...] = pmt
    both = jnp.concatenate([h_hi, comb.astype(bf16)], axis=1)
    srt = lax.dot_general(pmt, both, (((0,), (0,)), ((), ())), preferred_element_type=f32).astype(bf16)
    d = h2.shape[1]
    hs_ref[...] = srt[:, 0:d]
    combs_ref[...] = srt[:, d:d + LANES]
    cnt_ref[...] = tot8[7:8, :].astype(jnp.int32)


def _mix_call(x, z, g0, rest, ga1, sh2, sc2, n2g, wba, wout, wr, tri, tm):
    B, L, D = x.shape
    nt = L // tm
    row = lambda b, i: (b, i, 0)
    vspec = pl.BlockSpec((None, 1, D), lambda b, i: (b, 0, 0))
    return pl.pallas_call(
        _mix_body,
        grid=(B, nt),
        in_specs=[
            pl.BlockSpec((None, tm, D), row),
            pl.BlockSpec((None, tm, W_MIX), row),
            pl.BlockSpec((None, tm, D), row),
            pl.BlockSpec((None, tm, D), row),
            vspec, vspec, vspec,
            _const_spec((1, D)),
            _const_spec((W_MIX, D)),
            _const_spec((D, D)),
            _const_spec((D, 2 * LANES)),
            _const_spec((tm, tm)),
        ],
        out_specs=[pl.BlockSpec((None, tm, D), row), pl.BlockSpec((None, tm, D), row),
                   pl.BlockSpec((None, tm, LANES), row), pl.BlockSpec((None, tm, tm), row),
                   pl.BlockSpec((None, None, 1, LANES), lambda b, i: (b, i, 0, 0))],
        out_shape=[jax.ShapeDtypeStruct((B, L, D), f32), jax.ShapeDtypeStruct((B, L, D), bf16),
                   jax.ShapeDtypeStruct((B, L, LANES), bf16), jax.ShapeDtypeStruct((B, L, tm), bf16),
                   jax.ShapeDtypeStruct((B, nt, 1, LANES), jnp.int32)],
        compiler_params=_cparams(("parallel", "arbitrary")),
        name="mix",
    )(x, z, g0, rest, ga1, sh2, sc2, n2g, wba, wout, wr, tri)


def _experts_body(cnt_ref, hs_ref, combs_ref, pmt_ref, xo_ref, ga2_ref, ex_ref, w1_ref, w3_ref, w2_ref, fg_ref, o_ref,
                  acc_s, *, tm, sub, final_norm):
    b, sup, g = pl.program_id(0), pl.program_id(1), pl.program_id(2)
    nchunk = tm // MOE_CHUNK

    @pl.when(g == 0)
    def _():
        acc_s[...] = jnp.zeros_like(acc_s)

    def tile(t, carry):
        base = ((b * pl.num_programs(1) + sup) * sub + t) * N_GROUPS
        lo = jnp.int32(0)
        for gg in range(N_GROUPS - 1):
            lo = lo + jnp.where(gg < g, cnt_ref[base + gg], 0)
        hi = lo + cnt_ref[base + g]
        for c in range(nchunk):
            @pl.when(jnp.logical_and(lo < (c + 1) * MOE_CHUNK, hi > c * MOE_CHUNK))
            def _():
                rows = pl.ds(pl.multiple_of(t * tm + c * MOE_CHUNK, MOE_CHUNK), MOE_CHUNK)
                h = hs_ref[rows, :]
                a = jnp.dot(h, w1_ref[...], preferred_element_type=f32)
                u = jnp.dot(h, w3_ref[...], preferred_element_type=f32)
                cw = jnp.dot(combs_ref[rows, :], ex_ref[...], preferred_element_type=f32)
                hid = (a * jax.nn.sigmoid(a) * u * cw).astype(bf16)
                acc_s[rows, :] += jnp.dot(hid, w2_ref[...], preferred_element_type=f32)
        return carry

    lax.fori_loop(0, sub, tile, 0)

    @pl.when(g == pl.num_programs(2) - 1)
    def _():
        for t in range(sub):
            rows = slice(t * tm, (t + 1) * tm)
            y2 = jnp.dot(pmt_ref[rows, :], acc_s[rows, :].astype(bf16), preferred_element_type=f32)
            y = xo_ref[rows, :] + ga2_ref[...] * y2
            if final_norm:
                ms = jnp.mean(y * y, axis=-1, keepdims=True)
                y = y * lax.rsqrt(ms + EPS) * fg_ref[...]
            o_ref[rows, :] = y


def _experts_call(cnt, hs, combs, pmt, xo, ga2, ex, w1, w3, w2, fg, tm, final_norm):
    B, L, D = hs.shape
    sub = min(MOE_SUB, L // tm)
    rows = sub * tm
    blk = lambda b, s, g, cnt: (b, s, 0)
    return pl.pallas_call(
        functools.partial(_experts_body, tm=tm, sub=sub, final_norm=final_norm),
        grid_spec=pltpu.PrefetchScalarGridSpec(
            num_scalar_prefetch=1,
            grid=(B, L // rows, N_GROUPS),
            in_specs=[
                pl.BlockSpec((None, rows, D), blk),
                pl.BlockSpec((None, rows, LANES), blk),
                pl.BlockSpec((None, rows, tm), blk),
                pl.BlockSpec((None, rows, D), blk),
                pl.BlockSpec((None, 1, D), lambda b, s, g, cnt: (b, 0, 0)),
                pl.BlockSpec((None, LANES, GROUP_HID), lambda b, s, g, cnt: (g, 0, 0)),
                pl.BlockSpec((D, GROUP_HID), lambda b, s, g, cnt: (0, g)),
                pl.BlockSpec((D, GROUP_HID), lambda b, s, g, cnt: (0, g)),
                pl.BlockSpec((GROUP_HID, D), lambda b, s, g, cnt: (g, 0)),
                pl.BlockSpec((1, D), lambda b, s, g, cnt: (0, 0)),
            ],
            out_specs=pl.BlockSpec((None, rows, D), blk),
            scratch_shapes=[pltpu.VMEM((rows, D), f32)],
        ),
        out_shape=jax.ShapeDtypeStruct((B, L, D), f32),
        compiler_params=_cparams(("arbitrary", "arbitrary", "arbitrary")),
        name="experts",
    )(cnt, hs, combs, pmt, xo, ga2, ex, w1, w3, w2, fg)


def _pad_to(a, shape):
    return jnp.pad(a, [(0, s - d) for d, s in zip(a.shape, shape)])


def _prep_layer(l, p):
    max_decay = math.log(HY_TARGET) / HY_FAST_DECAY
    min_decay = math.log(HY_TARGET) / HY_SLOW_DECAY
    deltas = jnp.abs(jnp.linspace(min_decay, max_decay, W_MIX, dtype=f32))
    router = jnp.concatenate([p["router_g"][l], p["router_e"][l]], axis=1)
    lanes = jnp.arange(LANES)[None, :, None]
    cols = jnp.arange(GROUP_HID)[None, None, :]
    grp = jnp.arange(N_GROUPS)[:, None, None]
    expand = (lanes == N_GROUPS + EXP_PER_GROUP * grp + cols // D_EXPERT).astype(bf16)
    router = _pad_to(router, (D_MODEL, LANES))
    router_hi = router.astype(bf16)
    router_lo = (router - router_hi.astype(f32)).astype(bf16)
    zh = jnp.zeros((HY_HID, HY_HID), f32)
    w1 = _pad_to(p["hy_w1"][l], (LANES, HY_HID))
    zw1 = jnp.zeros_like(w1)
    w1_pair = jnp.block([[w1, zw1], [zw1, w1]])
    w2_pair = jnp.block([[p["hy_w2"][l], zh], [zh, p["hy_w2"][l]]])
    pair = lambda a: jnp.concatenate([a, a])[None]
    cw = HY_ORDER * W_MIX
    wo = p["hy_w_out"][l]
    zwo = jnp.zeros((HY_HID, cw), f32)
    wo_f = jnp.concatenate([wo[:, :cw], zwo], axis=0).astype(bf16)
    wo_b = jnp.concatenate([zwo, wo[:, cw:]], axis=0).astype(bf16)
    return dict(
        norm1_g=p["norm1_g"][l][None], norm2_g=p["norm2_g"][l][None],
        w_in=p["w_in"][l].astype(bf16), hy_conv_w=p["hy_conv_w"][l], hy_skip=p["hy_skip"][l],
        pool_w=p["pool_w"][l].astype(bf16), pool_scale=p["pool_scale"][l][None], sc_conv_w=p["sc_conv_w"][l],
        w_br_a=p["w_br_a"][l].astype(bf16), w_br_b=p["w_br_b"][l].astype(bf16), w_br_c=p["w_br_c"][l].astype(bf16),
        w_out=p["w_out"][l].astype(bf16),
        router=jnp.concatenate([router_hi, router_lo], axis=1), expand=expand,
        moe_w1=p["moe_w1"][l].astype(bf16), moe_w3=p["moe_w3"][l].astype(bf16), moe_w2=p["moe_w2"][l].astype(bf16),
        filt=(w1_pair, pair(p["hy_b1"][l]), w2_pair, pair(p["hy_b2"][l]), pair(p["hy_freq"][l]), wo_f, wo_b,
              pair(deltas)),
    )


def _tile(L, want):
    return want if L % want == 0 else L


def _encoder_layer(x, mod, lp, ksp, tabs, final_g, final_norm):
    B, L, D = x.shape
    sh1, sc1, ga1, sh2, sc2, ga2 = (m[:, None, :] for m in jnp.split(mod, 6, axis=-1))
    v, x1, x2, g0, rest = _proj_call(x, sh1, sc1, lp["norm1_g"], lp["w_in"], lp["hy_conv_w"], lp["pool_w"],
                                     lp["pool_scale"], lp["sc_conv_w"], lp["w_br_b"], lp["w_br_c"], _tile(L, 512))
    z = _hyena_conv(v, x1, x2, ksp, lp["hy_skip"], tabs)
    tm = _tile(L, MOE_TILE)
    tri = jnp.tri(tm, dtype=bf16)
    xo, hs, combs, pmt, cnt = _mix_call(x, z, g0, rest, ga1, sh2, sc2, lp["norm2_g"], lp["w_br_a"], lp["w_out"],
                                        lp["router"], tri, tm)
    return _experts_call(cnt[:, :, 0, :N_GROUPS].reshape(-1), hs, combs, pmt, xo, ga2, lp["expand"], lp["moe_w1"],
                         lp["moe_w3"], lp["moe_w2"], final_g, tm, final_norm)


def _forward(xs, cs, p, final_g):
    depth = p["w_in"].shape[0]
    nb = [c.shape[0] for c in cs]
    rows = -(-sum(nb) // 8) * 8
    c_all = _pad_to(jnp.concatenate(cs, axis=0), (rows, D_MODEL))
    lens = sorted({x.shape[1] for x in xs})
    tabs = {L: _dft_tables(L) for L in lens}
    fg = final_g[None]
    mods = _mod_call(c_all, p["ada_w"], p["ada_b"])
    for l in range(depth):
        lp = _prep_layer(l, p)
        mod = mods[l]
        ksp = {L: _filter_spectrum(L, tabs[L], lp["filt"]) for L in lens}
        off = 0
        out = []
        for x, n in zip(xs, nb):
            L = x.shape[1]
            out.append(_encoder_layer(x, mod[off:off + n], lp, ksp[L], tabs[L], fg, l == depth - 1))
            off += n
        xs = out
    return xs


def kernel(x_prompt, x_sample, c_prompt, c_sample, ada_w, ada_b, norm1_g, norm2_g, w_in, hy_conv_w, hy_skip, hy_w1, hy_b1, hy_w2, hy_b2, hy_w_out, hy_freq, pool_w, pool_scale, sc_conv_w, w_br_a, w_br_b, w_br_c, w_out, router_g, router_e, moe_w1, moe_w3, moe_w2, final_g):
    p = dict(ada_w=ada_w, ada_b=ada_b, norm1_g=norm1_g, norm2_g=norm2_g, w_in=w_in, hy_conv_w=hy_conv_w,
             hy_skip=hy_skip, hy_w1=hy_w1, hy_b1=hy_b1, hy_w2=hy_w2, hy_b2=hy_b2, hy_w_out=hy_w_out, hy_freq=hy_freq,
             pool_w=pool_w, pool_scale=pool_scale, sc_conv_w=sc_conv_w, w_br_a=w_br_a, w_br_b=w_br_b, w_br_c=w_br_c,
             w_out=w_out, router_g=router_g, router_e=router_e, moe_w1=moe_w1, moe_w3=moe_w3, moe_w2=moe_w2)
    y_prompt, y_sample = _forward([x_prompt, x_sample], [c_prompt, c_sample], p, final_g)
    return (y_prompt, y_sample)
```

```python
import functools
import math

import jax
import jax.numpy as jnp
from jax import lax
from jax.experimental import pallas as pl
from jax.experimental.pallas import tpu as pltpu

f32 = jnp.float32
bf16 = jnp.bfloat16
HIGHEST = lax.Precision.HIGHEST

D_MODEL = 1024
DEPTH = 2
W_MIX = 512
HY_ORDER = 2
HY_BANDS = 16
HY_HID = 64
HY_FAST_DECAY = 0.3
HY_SLOW_DECAY = 1.5
HY_TARGET = 1e-2
POOL_WINDOWS = (2, 4, 8, 16)
POOL_GROUP = W_MIX // len(POOL_WINDOWS)
COL_HY = 3 * W_MIX
COL_POOL = W_MIX
COL_SC = 3 * W_MIX
COL_GATE = 3 * D_MODEL
OFF_POOL = COL_HY
OFF_SC = COL_HY + COL_POOL
OFF_GATE = COL_HY + COL_POOL + COL_SC
PROJ_COLS = OFF_GATE + COL_GATE
N_GROUPS = 4
EXP_PER_GROUP = 4
N_EXPERTS = N_GROUPS * EXP_PER_GROUP
D_EXPERT = 256
GROUP_HID = EXP_PER_GROUP * D_EXPERT
EPS = 1e-6

HALO = 8
DFT_N2 = 256
DFT_ROWS = 128
DFT_RPB = 16
DFT_SLAB = 256
DFT_STAGE2_ROWS = 1024
MOE_TILE = 512
MOE_CHUNK = 128
MOE_WINDOW = 256
MOE_SUB = 2
LANES = 128
VMEM_LIMIT = 56 * 1024 * 1024


def _cparams(sem):
    return pltpu.CompilerParams(dimension_semantics=sem, vmem_limit_bytes=VMEM_LIMIT)


def _const_spec(shape):
    nd = len(shape)
    return pl.BlockSpec(shape, lambda *_: (0,) * nd, pipeline_mode=pl.Buffered(1))


def _mod_body(c_ref, w_ref, b_ref, o_ref):
    c = c_ref[...]
    s = c * jax.nn.sigmoid(c)
    o_ref[...] = jnp.dot(s, w_ref[...], preferred_element_type=f32, precision=HIGHEST) + b_ref[...]


def _mod_call(c_all, ada_w, ada_b):
    rows = c_all.shape[0]
    depth = ada_w.shape[0]
    tn = 1536
    return pl.pallas_call(
        _mod_body,
        grid=(depth, 6 * D_MODEL // tn),
        in_specs=[pl.BlockSpec((rows, D_MODEL), lambda l, j: (0, 0)),
                  pl.BlockSpec((None, D_MODEL, tn), lambda l, j: (l, 0, j)),
                  pl.BlockSpec((None, 1, tn), lambda l, j: (l, 0, j))],
        out_specs=pl.BlockSpec((None, rows, tn), lambda l, j: (l, 0, j)),
        out_shape=jax.ShapeDtypeStruct((depth, rows, 6 * D_MODEL), f32),
        compiler_params=_cparams(("arbitrary", "arbitrary")),
        name="mod",
    )(c_all, ada_w, ada_b[:, None, :])


def _proj_body(xm_ref, xp_ref, xn_ref, sh_ref, sc_ref, g_ref, win_ref, hyw_ref, pw_ref, ps_ref, scw_ref,
               wbb_ref, wbc_ref, v_ref, x1_ref, x2_ref, g0_ref, rest_ref, *, tm, seq_len):
    i = pl.program_id(1)
    nt = pl.num_programs(1)
    rt = tm + 2 * HALO
    ctr = slice(HALO, HALO + tm)

    def modulated(x):
        ms = jnp.mean(x * x, axis=-1, keepdims=True)
        h = x * lax.rsqrt(ms + EPS) * g_ref[...]
        return h * (1.0 + sc_ref[...]) + sh_ref[...]

    hp = jnp.where(i > 0, modulated(xp_ref[...]), 0.0)
    hn = jnp.where(i < nt - 1, modulated(xn_ref[...]), 0.0)
    hc = modulated(xm_ref[...])
    hb = jnp.concatenate([hp, hc, hn], axis=0).astype(bf16)
    hcb = hc.astype(bf16)

    def down(a, s):
        return pltpu.roll(a, s, 0)

    def up(a, s):
        return pltpu.roll(a, rt - s, 0)

    u = jnp.dot(hb, win_ref[:, 0:COL_HY], preferred_element_type=f32)
    w = hyw_ref[...]
    uc = (down(u, 1) * w[0:1] + u * w[1:2] + up(u, 1) * w[2:3])[ctr]
    v_ref[...] = uc[:, 0:W_MIX]
    x1_ref[...] = uc[:, W_MIX:2 * W_MIX]
    x2_ref[...] = uc[:, 2 * W_MIX:3 * W_MIX]

    q = jnp.dot(hb, win_ref[:, OFF_POOL:OFF_POOL + COL_POOL], preferred_element_type=f32)
    s2 = q + down(q, 1)
    s4 = s2 + down(s2, 2)
    s8 = s4 + down(s4, 4)
    s16 = s8 + down(s8, 8)
    tpos = i * tm + lax.broadcasted_iota(jnp.int32, (tm, 1), 0)
    pooled = []
    for g, (win, ssum) in enumerate(zip(POOL_WINDOWS, (s2, s4, s8, s16))):
        lo = win // 2
        hi = win - 1 - lo
        lanes = slice(g * POOL_GROUP, (g + 1) * POOL_GROUP)
        ws = ssum[:, lanes]
        if hi > 0:
            ws = up(ws, hi)
        cnt = (jnp.minimum(tpos + hi + 1, seq_len) - jnp.maximum(tpos - lo, 0)).astype(f32)
        p = ws[ctr] / cnt - q[ctr, lanes]
        pooled.append(jnp.dot(p.astype(bf16), pw_ref[g], preferred_element_type=f32))
    yb_in = jnp.concatenate(pooled, axis=1) * ps_ref[...]
    yb = jnp.dot(yb_in.astype(bf16), wbb_ref[...], preferred_element_type=f32)

    us = jnp.dot(hb, win_ref[:, OFF_SC:OFF_SC + COL_SC], preferred_element_type=f32)
    cx = us[:, W_MIX:2 * W_MIX] * us[:, 2 * W_MIX:3 * W_MIX]
    sw = scw_ref[...]
    dw = down(cx, 1) * sw[0:1] + cx * sw[1:2] + up(cx, 1) * sw[2:3]
    sc_out = (us[:, 0:W_MIX] * dw)[ctr]
    yc = jnp.dot(sc_out.astype(bf16), wbc_ref[...], preferred_element_type=f32)

    gt = 0.5 * jnp.tanh(0.5 * jnp.dot(hcb, win_ref[:, OFF_GATE:PROJ_COLS], preferred_element_type=f32)) + 0.5
    g0_ref[...] = gt[:, 0:D_MODEL]
    rest_ref[...] = gt[:, D_MODEL:2 * D_MODEL] * yb + gt[:, 2 * D_MODEL:3 * D_MODEL] * yc


def _proj_call(x, sh, sc, g1, win, hyw, pw, ps, scw, wbb, wbc, tm):
    B, L, D = x.shape
    nt = L // tm
    hb = tm // HALO
    row = lambda b, i: (b, i, 0)
    vec = lambda b, i: (b, 0, 0)
    out_w = jax.ShapeDtypeStruct((B, L, W_MIX), f32)
    out_d = jax.ShapeDtypeStruct((B, L, D), f32)
    return pl.pallas_call(
        functools.partial(_proj_body, tm=tm, seq_len=L),
        grid=(B, nt),
        in_specs=[
            pl.BlockSpec((None, tm, D), row),
            pl.BlockSpec((None, HALO, D), lambda b, i: (b, jnp.maximum(i * hb - 1, 0), 0)),
            pl.BlockSpec((None, HALO, D), lambda b, i: (b, jnp.minimum((i + 1) * hb, L // HALO - 1), 0)),
            pl.BlockSpec((None, 1, D), vec),
            pl.BlockSpec((None, 1, D), vec),
            _const_spec((1, D)),
            _const_spec((D, PROJ_COLS)),
            _const_spec((3, COL_HY)),
            _const_spec((len(POOL_WINDOWS), POOL_GROUP, POOL_GROUP)),
            _const_spec((1, W_MIX)),
            _const_spec((3, W_MIX)),
            _const_spec((W_MIX, D)),
            _const_spec((W_MIX, D)),
        ],
        out_specs=[pl.BlockSpec((None, tm, W_MIX), row)] * 3 + [pl.BlockSpec((None, tm, D), row)] * 2,
        out_shape=[out_w, out_w, out_w, out_d, out_d],
        compiler_params=_cparams(("parallel", "arbitrary")),
        name="proj",
    )(x, x, x, sh, sc, g1, win, hyw, pw, ps, scw, wbb, wbc)


def _filter_body(ca_ref, sa_ref, cb_ref, sb_ref, w1_ref, b1_ref, w2_ref, b2_ref, fr_ref, wo0_ref, wo1_ref, dl_ref, g_ref,
                 o_ref, asum_ref, h_s, *, rpb, seq_len, n1c):
    j = pl.program_id(0)
    s = pl.program_id(1)
    L = seq_len
    n2 = lax.broadcasted_iota(jnp.int32, (DFT_N2, 1), 0)
    fwd = n2 < DFT_ROWS

    def slot(r):
        pos = (j * rpb + r) + n1c * n2
        return pos, jnp.where(fwd, pos, 2 * L - pos).astype(f32)

    @pl.when(s == 0)
    def _():
        lane = lax.broadcasted_iota(jnp.int32, (DFT_N2, LANES), 1)
        fr = fr_ref[...]
        cb, sb = cb_ref[...], sb_ref[...]
        zs = []
        for r in range(rpb):
            _, lag = slot(r)
            ca = ca_ref[r:r + 1, :]
            sa = jnp.where(fwd, sa_ref[r:r + 1, :], -sa_ref[r:r + 1, :])
            cos_t = ca * cb - sa * sb
            sin_t = sa * cb + ca * sb
            z = jnp.where(lane == 0, lag / (L - 1), jnp.where(lane <= HY_BANDS, cos_t, -sin_t))
            zs.append(jnp.concatenate([z[:DFT_ROWS], z[DFT_ROWS:]], axis=1))
        zz = jnp.concatenate(zs, axis=0)
        h = jnp.sin(fr * (jnp.dot(zz, w1_ref[...], preferred_element_type=f32, precision=HIGHEST) + b1_ref[...]))
        h_s[...] = jnp.sin(fr * (jnp.dot(h, w2_ref[...], preferred_element_type=f32, precision=HIGHEST) + b2_ref[...]))

    hb = h_s[...].astype(bf16)
    ho_f = jnp.dot(hb, wo0_ref[...], preferred_element_type=f32)
    ho_b = jnp.dot(hb, wo1_ref[...], preferred_element_type=f32)
    asum = jnp.zeros(asum_ref.shape[1:], f32)
    bs = []
    for r in range(rpb):
        pos, lag = slot(r)
        rows = slice(r * DFT_ROWS, (r + 1) * DFT_ROWS)
        ho = jnp.concatenate([ho_f[rows], ho_b[rows]], axis=0)
        k = jnp.where(pos == L, 0.0, ho * jnp.exp(-(lag / (L - 1)) * dl_ref[...]))
        asum = asum + jnp.sum(jnp.abs(k), axis=0, keepdims=True)
        bs.append(jnp.dot(g_ref[r], k.astype(bf16), preferred_element_type=f32))
    o_ref[...] = jnp.swapaxes(jnp.stack(bs, axis=0), 0, 1).reshape(o_ref.shape).astype(bf16)

    @pl.when(j == 0)
    def _():
        asum_ref[s] = asum

    @pl.when(j > 0)
    def _():
        asum_ref[s] += asum


def _filter_tables(L):
    n1c = 2 * L // DFT_N2
    bands = jnp.linspace(1e-4, HY_BANDS - 1, HY_BANDS, dtype=f32)
    brow = jnp.zeros((LANES,), f32).at[1:1 + HY_BANDS].set(bands).at[1 + HY_BANDS:1 + 2 * HY_BANDS].set(bands)
    used = (jnp.arange(LANES) >= 1) & (jnp.arange(LANES) <= 2 * HY_BANDS)
    n2 = jnp.arange(DFT_N2)
    part_a = jnp.arange(n1c).astype(f32)
    part_b = (n1c * jnp.where(n2 < DFT_ROWS, n2, DFT_N2 - n2)).astype(f32)

    def cs(part):
        ang = (2 * math.pi / L) * part[:, None] * brow[None, :]
        return jnp.where(used, jnp.cos(ang), 0.0), jnp.where(used, jnp.sin(ang), 0.0)

    return cs(part_a) + cs(part_b)


def _filter_call(L, g_fwd, w1p, b1p, w2p, b2p, frp, wo_f, wo_b, dl_row):
    n1c = 2 * L // DFT_N2
    rpb = min(DFT_RPB, n1c)
    cw = HY_ORDER * W_MIX
    cs = DFT_SLAB
    ns = cw // cs
    ca, sa, cb, sb = _filter_tables(L)
    bs, asum = pl.pallas_call(
        functools.partial(_filter_body, rpb=rpb, seq_len=L, n1c=n1c),
        grid=(n1c // rpb, ns),
        in_specs=[pl.BlockSpec((rpb, LANES), lambda j, s: (j, 0)), pl.BlockSpec((rpb, LANES), lambda j, s: (j, 0)),
                  _const_spec((DFT_N2, LANES)), _const_spec((DFT_N2, LANES)),
                  _const_spec((2 * LANES, LANES)), _const_spec((1, LANES)),
                  _const_spec((LANES, LANES)), _const_spec((1, LANES)), _const_spec((1, LANES)),
                  pl.BlockSpec((LANES, cs), lambda j, s: (0, s)),
                  pl.BlockSpec((LANES, cs), lambda j, s: (0, s)),
                  pl.BlockSpec((1, cs), lambda j, s: (0, s)),
                  pl.BlockSpec((rpb, 2 * DFT_N2, DFT_N2), lambda j, s: (j, 0, 0))],
        out_specs=[pl.BlockSpec((2, DFT_N2, rpb, cs), lambda j, s: (0, 0, j, s)),
                   pl.BlockSpec((ns, 1, cs), lambda j, s: (0, 0, 0))],
        out_shape=[jax.ShapeDtypeStruct((2, DFT_N2, n1c, cw), bf16), jax.ShapeDtypeStruct((ns, 1, cs), f32)],
        scratch_shapes=[pltpu.VMEM((rpb * DFT_ROWS, LANES), f32)],
        compiler_params=_cparams(("arbitrary", "arbitrary")),
        name="filt",
    )(ca, sa, cb, sb, w1p, b1p, w2p, b2p, frp, wo_f, wo_b, dl_row, g_fwd)
    return bs, asum.reshape(1, cw)


def _dft_tables(L):
    n = 2 * L
    n1c = n // DFT_N2
    k2 = jnp.arange(DFT_N2, dtype=jnp.int32)
    tw_ang = ((jnp.arange(n1c, dtype=jnp.int32)[:, None] * k2[None, :]) % n).astype(f32) * (2 * math.pi / n)
    f_ang = ((k2[:, None] * k2[None, :]) % DFT_N2).astype(f32) * (2 * math.pi / DFT_N2)
    twr, twi = jnp.cos(tw_ang)[:, :, None], -jnp.sin(tw_ang)[:, :, None]
    fr, fi = jnp.cos(f_ang)[None], -jnp.sin(f_ang)[None]
    gr = twr * fr - twi * fi
    gi = twr * fi + twi * fr
    g_fwd = jnp.concatenate([gr, gi], axis=1)
    twr_t, twi_t = jnp.cos(tw_ang)[:, None, :], -jnp.sin(tw_ang)[:, None, :]
    fr_t, fi_t = fr[:, :DFT_ROWS, :], fi[:, :DFT_ROWS, :]
    g_inv = jnp.concatenate([twr_t * fr_t - twi_t * fi_t, twr_t * fi_t + twi_t * fr_t], axis=2) * (1.0 / n)
    a = jnp.arange(n1c, dtype=jnp.int32)
    s_ang = ((a[:, None] * a[None, :]) % n1c).astype(f32) * (2 * math.pi / n1c)
    eye = jnp.eye(DFT_ROWS // n1c, dtype=f32)
    sr = jnp.kron(eye, jnp.cos(s_ang))
    si = jnp.kron(eye, -jnp.sin(s_ang))
    m_fwd = jnp.block([[sr, -si], [si, sr]])
    m_inv = jnp.block([[sr, si], [-si, sr]])
    return dict(g_fwd=g_fwd.astype(bf16), g_inv=g_inv.astype(bf16), m_fwd=m_fwd.astype(bf16),
                m_inv=m_inv.astype(bf16), n1=n1c)


def _n1_major(a):
    return jnp.swapaxes(a, 0, 1)


def _n1_minor(mats, shape):
    return jnp.swapaxes(jnp.stack(mats, axis=0), 0, 1).reshape(shape)


def _fft1_body(x_ref, g_ref, o_ref, *, rpb):
    x = _n1_major(x_ref[...])
    bs = [jnp.dot(g_ref[r], x[r].astype(bf16), preferred_element_type=f32) for r in range(rpb)]
    o_ref[...] = _n1_minor(bs, o_ref.shape).astype(bf16)


def _fft1_call(x4, g_fwd):
    B, _, n1c, C = x4.shape
    rpb = min(DFT_RPB, n1c)
    cs = DFT_SLAB
    return pl.pallas_call(
        functools.partial(_fft1_body, rpb=rpb),
        grid=(B, n1c // rpb, C // cs),
        in_specs=[pl.BlockSpec((None, DFT_ROWS, rpb, cs), lambda b, j, s: (b, 0, j, s)),
                  pl.BlockSpec((rpb, 2 * DFT_N2, DFT_ROWS), lambda b, j, s: (j, 0, 0))],
        out_specs=pl.BlockSpec((None, 2, DFT_N2, rpb, cs), lambda b, j, s: (b, 0, 0, j, s)),
        out_shape=jax.ShapeDtypeStruct((B, 2, DFT_N2, n1c, C), bf16),
        compiler_params=_cparams(("parallel", "arbitrary", "arbitrary")),
        name="fft1",
    )(x4, g_fwd)


def _fft2_body(b_ref, kb_ref, asum_ref, mf_ref, mi_ref, o_ref, k_s, *, nsub):
    def stacked(ref, rows):
        return jnp.concatenate([ref[0, rows, :], ref[1, rows, :]], axis=0)

    @pl.when(pl.program_id(1) == 0)
    def _():
        inv = 1.0 / asum_ref[...]
        for i in range(nsub):
            rows = slice(i * DFT_ROWS, (i + 1) * DFT_ROWS)
            ks = jnp.dot(mf_ref[...], stacked(kb_ref, rows), preferred_element_type=f32)
            k_s[0, rows, :] = ks[:DFT_ROWS] * inv
            k_s[1, rows, :] = ks[DFT_ROWS:] * inv

    for i in range(nsub):
        rows = slice(i * DFT_ROWS, (i + 1) * DFT_ROWS)
        xs = jnp.dot(mf_ref[...], stacked(b_ref, rows), preferred_element_type=f32)
        xr, xi = xs[:DFT_ROWS], xs[DFT_ROWS:]
        kr, ki = k_s[0, rows, :], k_s[1, rows, :]
        ys = jnp.concatenate([xr * kr - xi * ki, xr * ki + xi * kr], axis=0).astype(bf16)
        cs = jnp.dot(mi_ref[...], ys, preferred_element_type=f32)
        o_ref[0, rows, :] = cs[:DFT_ROWS].astype(bf16)
        o_ref[1, rows, :] = cs[DFT_ROWS:].astype(bf16)


def _fft2_call(bs, kb, asum, order, m_fwd, m_inv):
    B, _, n, C = bs.shape
    rb = min(DFT_STAGE2_ROWS, n)
    blk = pl.BlockSpec((None, 2, rb, C), lambda j, b: (b, 0, j, 0))
    return pl.pallas_call(
        functools.partial(_fft2_body, nsub=rb // DFT_ROWS),
        grid=(n // rb, B),
        in_specs=[blk,
                  pl.BlockSpec((2, rb, C), lambda j, b: (0, j, order)),
                  pl.BlockSpec((1, C), lambda j, b: (0, order)),
                  _const_spec((2 * DFT_ROWS, 2 * DFT_ROWS)),
                  _const_spec((2 * DFT_ROWS, 2 * DFT_ROWS))],
        out_specs=blk,
        out_shape=jax.ShapeDtypeStruct(bs.shape, bf16),
        scratch_shapes=[pltpu.VMEM((2, rb, C), f32)],
        compiler_params=_cparams(("arbitrary", "arbitrary")),
        name="fft2",
    )(bs, kb, asum, m_fwd, m_inv)


def _fft3_body(c_ref, gi_ref, gate_ref, prev_ref, sk_ref, *rest, rpb, fuse_next):
    if fuse_next:
        gf_ref, z_ref, b_ref = rest
    else:
        (z_ref,) = rest
    cs = c_ref.shape[-1]
    c = _n1_major(c_ref[...].astype(f32).reshape(2 * DFT_N2, rpb, cs))
    gate = _n1_major(gate_ref[...])
    prev = _n1_major(prev_ref[...])
    sk = sk_ref[...]
    zs, bs = [], []
    for r in range(rpb):
        y = jnp.dot(gi_ref[r], c[r].astype(bf16), preferred_element_type=f32)
        z = gate[r] * (y + sk * prev[r])
        zs.append(z)
        if fuse_next:
            bs.append(jnp.dot(gf_ref[r], z.astype(bf16), preferred_element_type=f32))
    z_ref[...] = _n1_minor(zs, z_ref.shape)
    if fuse_next:
        b_ref[...] = _n1_minor(bs, b_ref.shape).astype(bf16)


def _fft3_call(cs5, g_inv, gate, prev, sk_row, g_fwd=None):
    B, _, _, n1c, C = cs5.shape
    rpb = min(DFT_RPB, n1c)
    cs = DFT_SLAB
    tblk = pl.BlockSpec((None, DFT_ROWS, rpb, cs), lambda b, j, s: (b, 0, j, s))
    sblk = pl.BlockSpec((None, 2, DFT_N2, rpb, cs), lambda b, j, s: (b, 0, 0, j, s))
    in_specs = [sblk, pl.BlockSpec((rpb, DFT_ROWS, 2 * DFT_N2), lambda b, j, s: (j, 0, 0)), tblk, tblk,
                pl.BlockSpec((1, cs), lambda b, j, s: (0, s))]
    args = [cs5, g_inv, gate, prev, sk_row]
    out_specs = [tblk]
    out_shape = [jax.ShapeDtypeStruct(gate.shape, f32)]
    fuse_next = g_fwd is not None
    if fuse_next:
        in_specs.append(pl.BlockSpec((rpb, 2 * DFT_N2, DFT_ROWS), lambda b, j, s: (j, 0, 0)))
        args.append(g_fwd)
        out_specs.append(sblk)
        out_shape.append(jax.ShapeDtypeStruct(cs5.shape, bf16))
    return pl.pallas_call(
        functools.partial(_fft3_body, rpb=rpb, fuse_next=fuse_next),
        grid=(B, n1c // rpb, C // cs),
        in_specs=in_specs,
        out_specs=out_specs,
        out_shape=out_shape,
        compiler_params=_cparams(("parallel", "arbitrary", "arbitrary")),
        name="fft3_next" if fuse_next else "fft3",
    )(*args)


def _filter_spectrum(L, tabs, filt_params):
    bs, asum = _filter_call(L, tabs["g_fwd"], *filt_params)
    return bs.reshape(2, 2 * L, bs.shape[-1]), asum


def _hyena_conv(v, x1, x2, ksp, skip, tabs):
    n1c = tabs["n1"]
    B, L, C = v.shape
    n = 2 * L
    kb, asum = ksp
    v4, x14, x24 = (a.reshape(B, DFT_ROWS, n1c, C) for a in (v, x1, x2))
    s5 = (B, 2, DFT_N2, n1c, C)
    bs = _fft1_call(v4, tabs["g_fwd"])
    cs = _fft2_call(bs.reshape(B, 2, n, C), kb, asum, 0, tabs["m_fwd"], tabs["m_inv"])
    z1, bs = _fft3_call(cs.reshape(s5), tabs["g_inv"], x14, v4, skip[0:1], tabs["g_fwd"])
    cs = _fft2_call(bs.reshape(B, 2, n, C), kb, asum, 1, tabs["m_fwd"], tabs["m_inv"])
    (z2,) = _fft3_call(cs.reshape(s5), tabs["g_inv"], x24, z1, skip[1:2])
    return z2.reshape(B, L, C)


def _route(r):
    lane = lax.broadcasted_iota(jnp.int32, r.shape, 1)
    ninf = jnp.float32(-jnp.inf)
    big = jnp.int32(1 << 20)
    is_g = lane < N_GROUPS
    gmax = jnp.max(jnp.where(is_g, r, ninf), axis=-1, keepdims=True)
    gidx = jnp.min(jnp.where(jnp.logical_and(is_g, r == gmax), lane, big), axis=-1, keepdims=True)
    gw = 1.0 / jnp.sum(jnp.where(is_g, jnp.exp(r - gmax), 0.0), axis=-1, keepdims=True)
    e_lane = lane - N_GROUPS
    sel = jnp.logical_and(jnp.logical_and(e_lane >= 0, e_lane < N_EXPERTS), (e_lane >> 2) == gidx)
    le = jnp.where(sel, r, ninf)
    m1 = jnp.max(le, axis=-1, keepdims=True)
    i1 = jnp.min(jnp.where(le == m1, lane, big), axis=-1, keepdims=True)
    le2 = jnp.where(lane == i1, ninf, le)
    m2 = jnp.max(le2, axis=-1, keepdims=True)
    i2 = jnp.min(jnp.where(le2 == m2, lane, big), axis=-1, keepdims=True)
    e2 = jnp.exp(m2 - m1)
    den = 1.0 + e2
    comb = jnp.where(lane == i1, gw / den, jnp.where(lane == i2, gw * e2 / den, 0.0))
    return comb, gidx


def _mix_body(x_ref, z_ref, g0_ref, rest_ref, ga1_ref, sh2_ref, sc2_ref, n2g_ref, wba_ref, wout_ref, wr_ref, tri_ref,
              xo_ref, hs_ref, combs_ref, pmt_ref, cnt_ref):
    ya = jnp.dot(z_ref[...].astype(bf16), wba_ref[...], preferred_element_type=f32)
    merged = g0_ref[...] * ya + rest_ref[...]
    xo = x_ref[...] + ga1_ref[...] * jnp.dot(merged.astype(bf16), wout_ref[...], preferred_element_type=f32)
    xo_ref[...] = xo
    ms = jnp.mean(xo * xo, axis=-1, keepdims=True)
    h2 = xo * lax.rsqrt(ms + EPS) * n2g_ref[...]
    h2 = h2 * (1.0 + sc2_ref[...]) + sh2_ref[...]
    h_hi = h2.astype(bf16)
    h_lo = (h2 - h_hi.astype(f32)).astype(bf16)
    p_hi = jnp.dot(h_hi, wr_ref[...], preferred_element_type=f32)
    p_lo = jnp.dot(h_lo, wr_ref[:, 0:LANES], preferred_element_type=f32)
    comb, gidx = _route(p_hi[:, 0:LANES] + p_hi[:, LANES:2 * LANES] + p_lo)

    tm = comb.shape[0]
    lane = lax.broadcasted_iota(jnp.int32, comb.shape, 1)
    onehot = (lane == gidx).astype(f32)
    cum = jnp.dot(tri_ref[...], onehot.astype(bf16), preferred_element_type=f32)
    tot8 = cum[tm - 8:tm, :]
    off8 = pltpu.roll(tot8, 1, 1) + pltpu.roll(tot8, 2, 1) + pltpu.roll(tot8, 3, 1)
    rank = jnp.sum(onehot * (off8[7:8, :] + cum - 1.0), axis=-1, keepdims=True)
    slot = lax.broadcasted_iota(jnp.int32, (tm, tm), 1).astype(f32)
    pmt = (slot == rank).astype(bf16)
    pmt_ref[...] = pmt
    both = jnp.concatenate([h_hi, comb.astype(bf16)], axis=1)
    srt = lax.dot_general(pmt, both, (((0,), (0,)), ((), ())), preferred_element_type=f32).astype(bf16)
    d = h2.shape[1]
    hs_ref[...] = srt[:, 0:d]
    combs_ref[...] = srt[:, d:d + LANES]
    cnt_ref[...] = tot8[7:8, :].astype(jnp.int32)


def _mix_call(x, z, g0, rest, ga1, sh2, sc2, n2g, wba, wout, wr, tri, tm):
    B, L, D = x.shape
    nt = L // tm
    row = lambda b, i: (b, i, 0)
    vspec = pl.BlockSpec((None, 1, D), lambda b, i: (b, 0, 0))
    return pl.pallas_call(
        _mix_body,
        grid=(B, nt),
        in_specs=[
            pl.BlockSpec((None, tm, D), row),
            pl.BlockSpec((None, tm, W_MIX), row),
            pl.BlockSpec((None, tm, D), row),
            pl.BlockSpec((None, tm, D), row),
            vspec, vspec, vspec,
            _const_spec((1, D)),
            _const_spec((W_MIX, D)),
            _const_spec((D, D)),
            _const_spec((D, 2 * LANES)),
            _const_spec((tm, tm)),
        ],
        out_specs=[pl.BlockSpec((None, tm, D), row), pl.BlockSpec((None, tm, D), row),
                   pl.BlockSpec((None, tm, LANES), row), pl.BlockSpec((None, tm, tm), row),
                   pl.BlockSpec((None, None, 1, LANES), lambda b, i: (b, i, 0, 0))],
        out_shape=[jax.ShapeDtypeStruct((B, L, D), f32), jax.ShapeDtypeStruct((B, L, D), bf16),
                   jax.ShapeDtypeStruct((B, L, LANES), bf16), jax.ShapeDtypeStruct((B, L, tm), bf16),
                   jax.ShapeDtypeStruct((B, nt, 1, LANES), jnp.int32)],
        compiler_params=_cparams(("parallel", "arbitrary")),
        name="mix",
    )(x, z, g0, rest, ga1, sh2, sc2, n2g, wba, wout, wr, tri)


def _experts_body(cnt_ref, hs_ref, combs_ref, pmt_ref, xo_ref, ga2_ref, ex_ref, w1_ref, w3_ref, w2_ref, fg_ref, o_ref,
                  acc_s, *, tm, sub, final_norm):
    b, sup, g = pl.program_id(0), pl.program_id(1), pl.program_id(2)
    nchunk = tm // MOE_CHUNK

    @pl.when(g == 0)
    def _():
        acc_s[...] = jnp.zeros_like(acc_s)

    def tile(t, carry):
        base = ((b * pl.num_programs(1) + sup) * sub + t) * N_GROUPS
        lo = jnp.int32(0)
        for gg in range(N_GROUPS - 1):
            lo = lo + jnp.where(gg < g, cnt_ref[base + gg], 0)
        hi = lo + cnt_ref[base + g]

        def run(start, nrows):
            rows = pl.ds(pl.multiple_of(t * tm + start, MOE_CHUNK), nrows)
            h = hs_ref[rows, :]
            a = jnp.dot(h, w1_ref[...], preferred_element_type=f32)
            u = jnp.dot(h, w3_ref[...], preferred_element_type=f32)
            cw = jnp.dot(combs_ref[rows, :], ex_ref[...], preferred_element_type=f32)
            silu = 0.5 * a * (jnp.tanh(0.5 * a) + 1.0)
            acc_s[rows, :] += jnp.dot((silu * u * cw).astype(bf16), w2_ref[...], preferred_element_type=f32)

        win = jnp.minimum(lo - (lo & (MOE_CHUNK - 1)), tm - MOE_WINDOW)

        @pl.when(hi > lo)
        def _():
            run(win, MOE_WINDOW)

        for c in range(nchunk):
            @pl.when(jnp.logical_and(c * MOE_CHUNK >= win + MOE_WINDOW, hi > c * MOE_CHUNK))
            def _():
                run(c * MOE_CHUNK, MOE_CHUNK)
        return carry

    lax.fori_loop(0, sub, tile, 0)

    @pl.when(g == pl.num_programs(2) - 1)
    def _():
        for t in range(sub):
            rows = slice(t * tm, (t + 1) * tm)
            y2 = jnp.dot(pmt_ref[rows, :], acc_s[rows, :].astype(bf16), preferred_element_type=f32)
            y = xo_ref[rows, :] + ga2_ref[...] * y2
            if final_norm:
                ms = jnp.mean(y * y, axis=-1, keepdims=True)
                y = y * lax.rsqrt(ms + EPS) * fg_ref[...]
            o_ref[rows, :] = y


def _experts_call(cnt, hs, combs, pmt, xo, ga2, ex, w1, w3, w2, fg, tm, final_norm):
    B, L, D = hs.shape
    sub = min(MOE_SUB, L // tm)
    rows = sub * tm
    blk = lambda b, s, g, cnt: (b, s, 0)
    return pl.pallas_call(
        functools.partial(_experts_body, tm=tm, sub=sub, final_norm=final_norm),
        grid_spec=pltpu.PrefetchScalarGridSpec(
            num_scalar_prefetch=1,
            grid=(B, L // rows, N_GROUPS),
            in_specs=[
                pl.BlockSpec((None, rows, D), blk),
                pl.BlockSpec((None, rows, LANES), blk),
                pl.BlockSpec((None, rows, tm), blk),
                pl.BlockSpec((None, rows, D), blk),
                pl.BlockSpec((None, 1, D), lambda b, s, g, cnt: (b, 0, 0)),
                pl.BlockSpec((None, LANES, GROUP_HID), lambda b, s, g, cnt: (g, 0, 0)),
                pl.BlockSpec((D, GROUP_HID), lambda b, s, g, cnt: (0, g)),
                pl.BlockSpec((D, GROUP_HID), lambda b, s, g, cnt: (0, g)),
                pl.BlockSpec((GROUP_HID, D), lambda b, s, g, cnt: (g, 0)),
                pl.BlockSpec((1, D), lambda b, s, g, cnt: (0, 0)),
            ],
            out_specs=pl.BlockSpec((None, rows, D), blk),
            scratch_shapes=[pltpu.VMEM((rows, D), f32)],
        ),
        out_shape=jax.ShapeDtypeStruct((B, L, D), f32),
        compiler_params=_cparams(("arbitrary", "arbitrary", "arbitrary")),
        name="experts",
    )(cnt, hs, combs, pmt, xo, ga2, ex, w1, w3, w2, fg)


def _pad_to(a, shape):
    return jnp.pad(a, [(0, s - d) for d, s in zip(a.shape, shape)])


def _prep_layer(l, p):
    max_decay = math.log(HY_TARGET) / HY_FAST_DECAY
    min_decay = math.log(HY_TARGET) / HY_SLOW_DECAY
    deltas = jnp.abs(jnp.linspace(min_decay, max_decay, W_MIX, dtype=f32))
    router = jnp.concatenate([p["router_g"][l], p["router_e"][l]], axis=1)
    lanes = jnp.arange(LANES)[None, :, None]
    cols = jnp.arange(GROUP_HID)[None, None, :]
    grp = jnp.arange(N_GROUPS)[:, None, None]
    expand = (lanes == N_GROUPS + EXP_PER_GROUP * grp + cols // D_EXPERT).astype(bf16)
    router = _pad_to(router, (D_MODEL, LANES))
    router_hi = router.astype(bf16)
    router_lo = (router - router_hi.astype(f32)).astype(bf16)
    zh = jnp.zeros((HY_HID, HY_HID), f32)
    w1 = _pad_to(p["hy_w1"][l], (LANES, HY_HID))
    zw1 = jnp.zeros_like(w1)
    w1_pair = jnp.block([[w1, zw1], [zw1, w1]])
    w2_pair = jnp.block([[p["hy_w2"][l], zh], [zh, p["hy_w2"][l]]])
    pair = lambda a: jnp.concatenate([a, a])[None]
    cw = HY_ORDER * W_MIX
    wo = p["hy_w_out"][l]
    zwo = jnp.zeros((HY_HID, cw), f32)
    wo_f = jnp.concatenate([wo[:, :cw], zwo], axis=0).astype(bf16)
    wo_b = jnp.concatenate([zwo, wo[:, cw:]], axis=0).astype(bf16)
    return dict(
        norm1_g=p["norm1_g"][l][None], norm2_g=p["norm2_g"][l][None],
        w_in=p["w_in"][l].astype(bf16), hy_conv_w=p["hy_conv_w"][l], hy_skip=p["hy_skip"][l],
        pool_w=p["pool_w"][l].astype(bf16), pool_scale=p["pool_scale"][l][None], sc_conv_w=p["sc_conv_w"][l],
        w_br_a=p["w_br_a"][l].astype(bf16), w_br_b=p["w_br_b"][l].astype(bf16), w_br_c=p["w_br_c"][l].astype(bf16),
        w_out=p["w_out"][l].astype(bf16),
        router=jnp.concatenate([router_hi, router_lo], axis=1), expand=expand,
        moe_w1=p["moe_w1"][l].astype(bf16), moe_w3=p["moe_w3"][l].astype(bf16), moe_w2=p["moe_w2"][l].astype(bf16),
        filt=(w1_pair, pair(p["hy_b1"][l]), w2_pair, pair(p["hy_b2"][l]), pair(p["hy_freq"][l]), wo_f, wo_b,
              pair(deltas)),
    )


def _tile(L, want):
    return want if L % want == 0 else L


def _encoder_layer(x, mod, lp, ksp, tabs, final_g, final_norm):
    B, L, D = x.shape
    sh1, sc1, ga1, sh2, sc2, ga2 = (m[:, None, :] for m in jnp.split(mod, 6, axis=-1))
    v, x1, x2, g0, rest = _proj_call(x, sh1, sc1, lp["norm1_g"], lp["w_in"], lp["hy_conv_w"], lp["pool_w"],
                                     lp["pool_scale"], lp["sc_conv_w"], lp["w_br_b"], lp["w_br_c"], _tile(L, 512))
    z = _hyena_conv(v, x1, x2, ksp, lp["hy_skip"], tabs)
    tm = _tile(L, MOE_TILE)
    tri = jnp.tri(tm, dtype=bf16)
    xo, hs, combs, pmt, cnt = _mix_call(x, z, g0, rest, ga1, sh2, sc2, lp["norm2_g"], lp["w_br_a"], lp["w_out"],
                                        lp["router"], tri, tm)
    return _experts_call(cnt[:, :, 0, :N_GROUPS].reshape(-1), hs, combs, pmt, xo, ga2, lp["expand"], lp["moe_w1"],
                         lp["moe_w3"], lp["moe_w2"], final_g, tm, final_norm)


def _forward(xs, cs, p, final_g):
    depth = p["w_in"].shape[0]
    nb = [c.shape[0] for c in cs]
    rows = -(-sum(nb) // 8) * 8
    c_all = _pad_to(jnp.concatenate(cs, axis=0), (rows, D_MODEL))
    lens = sorted({x.shape[1] for x in xs})
    tabs = {L: _dft_tables(L) for L in lens}
    fg = final_g[None]
    mods = _mod_call(c_all, p["ada_w"], p["ada_b"])
    for l in range(depth):
        lp = _prep_layer(l, p)
        mod = mods[l]
        ksp = {L: _filter_spectrum(L, tabs[L], lp["filt"]) for L in lens}
        off = 0
        out = []
        for x, n in zip(xs, nb):
            L = x.shape[1]
            out.append(_encoder_layer(x, mod[off:off + n], lp, ksp[L], tabs[L], fg, l == depth - 1))
            off += n
        xs = out
    return xs


def kernel(x_prompt, x_sample, c_prompt, c_sample, ada_w, ada_b, norm1_g, norm2_g, w_in, hy_conv_w, hy_skip, hy_w1, hy_b1, hy_w2, hy_b2, hy_w_out, hy_freq, pool_w, pool_scale, sc_conv_w, w_br_a, w_br_b, w_br_c, w_out, router_g, router_e, moe_w1, moe_w3, moe_w2, final_g):
    p = dict(ada_w=ada_w, ada_b=ada_b, norm1_g=norm1_g, norm2_g=norm2_g, w_in=w_in, hy_conv_w=hy_conv_w,
             hy_skip=hy_skip, hy_w1=hy_w1, hy_b1=hy_b1, hy_w2=hy_w2, hy_b2=hy_b2, hy_w_out=hy_w_out, hy_freq=hy_freq,
             pool_w=pool_w, pool_scale=pool_scale, sc_conv_w=sc_conv_w, w_br_a=w_br_a, w_br_b=w_br_b, w_br_c=w_br_c,
             w_out=w_out, router_g=router_g, router_e=router_e, moe_w1=moe_w1, moe_w3=moe_w3, moe_w2=moe_w2)
    y_prompt, y_sample = _forward([x_prompt, x_sample], [c_prompt, c_sample], p, final_g)
    return (y_prompt, y_sample)
```

```python
import functools
import math

import jax
import jax.numpy as jnp
from jax import lax
from jax.experimental import pallas as pl
from jax.experimental.pallas import tpu as pltpu

f32 = jnp.float32
bf16 = jnp.bfloat16
HIGHEST = lax.Precision.HIGHEST

D_MODEL = 1024
DEPTH = 2
W_MIX = 512
HY_ORDER = 2
HY_BANDS = 16
HY_HID = 64
HY_FAST_DECAY = 0.3
HY_SLOW_DECAY = 1.5
HY_TARGET = 1e-2
POOL_WINDOWS = (2, 4, 8, 16)
POOL_GROUP = W_MIX // len(POOL_WINDOWS)
COL_HY = 3 * W_MIX
COL_POOL = W_MIX
COL_SC = 3 * W_MIX
COL_GATE = 3 * D_MODEL
OFF_POOL = COL_HY
OFF_SC = COL_HY + COL_POOL
OFF_GATE = COL_HY + COL_POOL + COL_SC
PROJ_COLS = OFF_GATE + COL_GATE
N_GROUPS = 4
EXP_PER_GROUP = 4
N_EXPERTS = N_GROUPS * EXP_PER_GROUP
D_EXPERT = 256
GROUP_HID = EXP_PER_GROUP * D_EXPERT
EPS = 1e-6

HALO = 8
DFT_N2 = 256
DFT_ROWS = 128
DFT_RPB = 16
DFT_SLAB = 256
DFT_STAGE2_BLOCKS = 17
MOE_TILE = 512
MOE_CHUNK = 128
MOE_WINDOW = 256
MOE_SUB = 2
LANES = 128
VMEM_LIMIT = 56 * 1024 * 1024


def _cparams(sem):
    return pltpu.CompilerParams(dimension_semantics=sem, vmem_limit_bytes=VMEM_LIMIT)


def _const_spec(shape):
    nd = len(shape)
    return pl.BlockSpec(shape, lambda *_: (0,) * nd, pipeline_mode=pl.Buffered(1))


def _mod_body(c_ref, w_ref, b_ref, o_ref):
    c = c_ref[...]
    s = c * jax.nn.sigmoid(c)
    o_ref[...] = jnp.dot(s, w_ref[...], preferred_element_type=f32, precision=HIGHEST) + b_ref[...]


def _mod_call(c_all, ada_w, ada_b):
    rows = c_all.shape[0]
    depth = ada_w.shape[0]
    tn = 1536
    return pl.pallas_call(
        _mod_body,
        grid=(depth, 6 * D_MODEL // tn),
        in_specs=[pl.BlockSpec((rows, D_MODEL), lambda l, j: (0, 0)),
                  pl.BlockSpec((None, D_MODEL, tn), lambda l, j: (l, 0, j)),
                  pl.BlockSpec((None, 1, tn), lambda l, j: (l, 0, j))],
        out_specs=pl.BlockSpec((None, rows, tn), lambda l, j: (l, 0, j)),
        out_shape=jax.ShapeDtypeStruct((depth, rows, 6 * D_MODEL), f32),
        compiler_params=_cparams(("arbitrary", "arbitrary")),
        name="mod",
    )(c_all, ada_w, ada_b[:, None, :])


def _proj_body(xm_ref, xp_ref, xn_ref, sh_ref, sc_ref, g_ref, win_ref, hyw_ref, pw_ref, ps_ref, scw_ref,
               wbb_ref, wbc_ref, v_ref, x1_ref, x2_ref, g0_ref, rest_ref, *, tm, seq_len):
    i = pl.program_id(1)
    nt = pl.num_programs(1)
    rt = tm + 2 * HALO
    ctr = slice(HALO, HALO + tm)

    def modulated(x):
        ms = jnp.mean(x * x, axis=-1, keepdims=True)
        h = x * lax.rsqrt(ms + EPS) * g_ref[...]
        return h * (1.0 + sc_ref[...]) + sh_ref[...]

    hp = jnp.where(i > 0, modulated(xp_ref[...]), 0.0)
    hn = jnp.where(i < nt - 1, modulated(xn_ref[...]), 0.0)
    hc = modulated(xm_ref[...])
    hb = jnp.concatenate([hp, hc, hn], axis=0).astype(bf16)
    hcb = hc.astype(bf16)

    def down(a, s):
        return pltpu.roll(a, s, 0)

    def up(a, s):
        return pltpu.roll(a, rt - s, 0)

    u = jnp.dot(hb, win_ref[:, 0:COL_HY], preferred_element_type=f32)
    w = hyw_ref[...]
    uc = (down(u, 1) * w[0:1] + u * w[1:2] + up(u, 1) * w[2:3])[ctr]
    v_ref[...] = uc[:, 0:W_MIX]
    x1_ref[...] = uc[:, W_MIX:2 * W_MIX]
    x2_ref[...] = uc[:, 2 * W_MIX:3 * W_MIX]

    q = jnp.dot(hb, win_ref[:, OFF_POOL:OFF_POOL + COL_POOL], preferred_element_type=f32)
    s2 = q + down(q, 1)
    s4 = s2 + down(s2, 2)
    s8 = s4 + down(s4, 4)
    s16 = s8 + down(s8, 8)
    tpos = i * tm + lax.broadcasted_iota(jnp.int32, (tm, 1), 0)
    pooled = []
    for g, (win, ssum) in enumerate(zip(POOL_WINDOWS, (s2, s4, s8, s16))):
        lo = win // 2
        hi = win - 1 - lo
        lanes = slice(g * POOL_GROUP, (g + 1) * POOL_GROUP)
        ws = ssum[:, lanes]
        if hi > 0:
            ws = up(ws, hi)
        cnt = (jnp.minimum(tpos + hi + 1, seq_len) - jnp.maximum(tpos - lo, 0)).astype(f32)
        p = ws[ctr] / cnt - q[ctr, lanes]
        pooled.append(jnp.dot(p.astype(bf16), pw_ref[g], preferred_element_type=f32))
    yb_in = jnp.concatenate(pooled, axis=1) * ps_ref[...]
    yb = jnp.dot(yb_in.astype(bf16), wbb_ref[...], preferred_element_type=f32)

    us = jnp.dot(hb, win_ref[:, OFF_SC:OFF_SC + COL_SC], preferred_element_type=f32)
    cx = us[:, W_MIX:2 * W_MIX] * us[:, 2 * W_MIX:3 * W_MIX]
    sw = scw_ref[...]
    dw = down(cx, 1) * sw[0:1] + cx * sw[1:2] + up(cx, 1) * sw[2:3]
    sc_out = (us[:, 0:W_MIX] * dw)[ctr]
    yc = jnp.dot(sc_out.astype(bf16), wbc_ref[...], preferred_element_type=f32)

    gt = 0.5 * jnp.tanh(0.5 * jnp.dot(hcb, win_ref[:, OFF_GATE:PROJ_COLS], preferred_element_type=f32)) + 0.5
    g0_ref[...] = gt[:, 0:D_MODEL]
    rest_ref[...] = gt[:, D_MODEL:2 * D_MODEL] * yb + gt[:, 2 * D_MODEL:3 * D_MODEL] * yc


def _proj_call(x, sh, sc, g1, win, hyw, pw, ps, scw, wbb, wbc, tm):
    B, L, D = x.shape
    nt = L // tm
    hb = tm // HALO
    row = lambda b, i: (b, i, 0)
    vec = lambda b, i: (b, 0, 0)
    out_w = jax.ShapeDtypeStruct((B, L, W_MIX), f32)
    out_d = jax.ShapeDtypeStruct((B, L, D), f32)
    return pl.pallas_call(
        functools.partial(_proj_body, tm=tm, seq_len=L),
        grid=(B, nt),
        in_specs=[
            pl.BlockSpec((None, tm, D), row),
            pl.BlockSpec((None, HALO, D), lambda b, i: (b, jnp.maximum(i * hb - 1, 0), 0)),
            pl.BlockSpec((None, HALO, D), lambda b, i: (b, jnp.minimum((i + 1) * hb, L // HALO - 1), 0)),
            pl.BlockSpec((None, 1, D), vec),
            pl.BlockSpec((None, 1, D), vec),
            _const_spec((1, D)),
            _const_spec((D, PROJ_COLS)),
            _const_spec((3, COL_HY)),
            _const_spec((len(POOL_WINDOWS), POOL_GROUP, POOL_GROUP)),
            _const_spec((1, W_MIX)),
            _const_spec((3, W_MIX)),
            _const_spec((W_MIX, D)),
            _const_spec((W_MIX, D)),
        ],
        out_specs=[pl.BlockSpec((None, tm, W_MIX), row)] * 3 + [pl.BlockSpec((None, tm, D), row)] * 2,
        out_shape=[out_w, out_w, out_w, out_d, out_d],
        compiler_params=_cparams(("parallel", "arbitrary")),
        name="proj",
    )(x, x, x, sh, sc, g1, win, hyw, pw, ps, scw, wbb, wbc)


def _filter_body(ca_ref, sa_ref, cb_ref, sb_ref, w1_ref, b1_ref, w2_ref, b2_ref, fr_ref, wo0_ref, wo1_ref, dl_ref, g_ref,
                 o_ref, asum_ref, h_s, *, rpb, seq_len, n1c):
    j = pl.program_id(0)
    s = pl.program_id(1)
    L = seq_len
    n2 = lax.broadcasted_iota(jnp.int32, (DFT_N2, 1), 0)
    fwd = n2 < DFT_ROWS

    def slot(r):
        pos = (j * rpb + r) + n1c * n2
        return pos, jnp.where(fwd, pos, 2 * L - pos).astype(f32)

    @pl.when(s == 0)
    def _():
        lane = lax.broadcasted_iota(jnp.int32, (DFT_N2, LANES), 1)
        fr = fr_ref[...]
        cb, sb = cb_ref[...], sb_ref[...]
        zs = []
        for r in range(rpb):
            _, lag = slot(r)
            ca = ca_ref[r:r + 1, :]
            sa = jnp.where(fwd, sa_ref[r:r + 1, :], -sa_ref[r:r + 1, :])
            cos_t = ca * cb - sa * sb
            sin_t = sa * cb + ca * sb
            z = jnp.where(lane == 0, lag / (L - 1), jnp.where(lane <= HY_BANDS, cos_t, -sin_t))
            zs.append(jnp.concatenate([z[:DFT_ROWS], z[DFT_ROWS:]], axis=1))
        zz = jnp.concatenate(zs, axis=0)
        h = jnp.sin(fr * (jnp.dot(zz, w1_ref[...], preferred_element_type=f32, precision=HIGHEST) + b1_ref[...]))
        h_s[...] = jnp.sin(fr * (jnp.dot(h, w2_ref[...], preferred_element_type=f32, precision=HIGHEST) + b2_ref[...]))

    hb = h_s[...].astype(bf16)
    ho_f = jnp.dot(hb, wo0_ref[...], preferred_element_type=f32)
    ho_b = jnp.dot(hb, wo1_ref[...], preferred_element_type=f32)
    asum = jnp.zeros(asum_ref.shape[1:], f32)
    bs = []
    for r in range(rpb):
        pos, lag = slot(r)
        rows = slice(r * DFT_ROWS, (r + 1) * DFT_ROWS)
        ho = jnp.concatenate([ho_f[rows], ho_b[rows]], axis=0)
        k = jnp.where(pos == L, 0.0, ho * jnp.exp(-(lag / (L - 1)) * dl_ref[...]))
        asum = asum + jnp.sum(jnp.abs(k), axis=0, keepdims=True)
        bs.append(jnp.dot(g_ref[r], k.astype(bf16), preferred_element_type=f32))
    o_ref[...] = jnp.swapaxes(jnp.stack(bs, axis=0), 0, 1).reshape(o_ref.shape).astype(bf16)

    @pl.when(j == 0)
    def _():
        asum_ref[s] = asum

    @pl.when(j > 0)
    def _():
        asum_ref[s] += asum


def _filter_tables(L):
    n1c = 2 * L // DFT_N2
    bands = jnp.linspace(1e-4, HY_BANDS - 1, HY_BANDS, dtype=f32)
    brow = jnp.zeros((LANES,), f32).at[1:1 + HY_BANDS].set(bands).at[1 + HY_BANDS:1 + 2 * HY_BANDS].set(bands)
    used = (jnp.arange(LANES) >= 1) & (jnp.arange(LANES) <= 2 * HY_BANDS)
    n2 = jnp.arange(DFT_N2)
    part_a = jnp.arange(n1c).astype(f32)
    part_b = (n1c * jnp.where(n2 < DFT_ROWS, n2, DFT_N2 - n2)).astype(f32)

    def cs(part):
        ang = (2 * math.pi / L) * part[:, None] * brow[None, :]
        return jnp.where(used, jnp.cos(ang), 0.0), jnp.where(used, jnp.sin(ang), 0.0)

    return cs(part_a) + cs(part_b)


def _filter_call(L, g_fwd, w1p, b1p, w2p, b2p, frp, wo_f, wo_b, dl_row):
    n1c = 2 * L // DFT_N2
    rpb = min(DFT_RPB, n1c)
    cw = HY_ORDER * W_MIX
    cs = DFT_SLAB
    ns = cw // cs
    ca, sa, cb, sb = _filter_tables(L)
    kh = g_fwd.shape[1] // 2
    bs, asum = pl.pallas_call(
        functools.partial(_filter_body, rpb=rpb, seq_len=L, n1c=n1c),
        grid=(n1c // rpb, ns),
        in_specs=[pl.BlockSpec((rpb, LANES), lambda j, s: (j, 0)), pl.BlockSpec((rpb, LANES), lambda j, s: (j, 0)),
                  _const_spec((DFT_N2, LANES)), _const_spec((DFT_N2, LANES)),
                  _const_spec((2 * LANES, LANES)), _const_spec((1, LANES)),
                  _const_spec((LANES, LANES)), _const_spec((1, LANES)), _const_spec((1, LANES)),
                  pl.BlockSpec((LANES, cs), lambda j, s: (0, s)),
                  pl.BlockSpec((LANES, cs), lambda j, s: (0, s)),
                  pl.BlockSpec((1, cs), lambda j, s: (0, s)),
                  pl.BlockSpec((rpb, 2 * kh, DFT_N2), lambda j, s: (j, 0, 0))],
        out_specs=[pl.BlockSpec((2, kh, rpb, cs), lambda j, s: (0, 0, j, s)),
                   pl.BlockSpec((ns, 1, cs), lambda j, s: (0, 0, 0))],
        out_shape=[jax.ShapeDtypeStruct((2, kh, n1c, cw), bf16), jax.ShapeDtypeStruct((ns, 1, cs), f32)],
        scratch_shapes=[pltpu.VMEM((rpb * DFT_ROWS, LANES), f32)],
        compiler_params=_cparams(("arbitrary", "arbitrary")),
        name="filt",
    )(ca, sa, cb, sb, w1p, b1p, w2p, b2p, frp, wo_f, wo_b, dl_row, g_fwd)
    return bs, asum.reshape(1, cw)


def _dft_half_rows(n1c):
    step = max(8, DFT_ROWS // n1c)
    return -(-(DFT_ROWS + 1) // step) * step


def _dft_tables(L):
    n = 2 * L
    n1c = n // DFT_N2
    kh = _dft_half_rows(n1c)
    k2 = jnp.arange(DFT_N2, dtype=jnp.int32)
    tw_ang = ((jnp.arange(n1c, dtype=jnp.int32)[:, None] * k2[None, :]) % n).astype(f32) * (2 * math.pi / n)
    f_ang = ((k2[:, None] * k2[None, :]) % DFT_N2).astype(f32) * (2 * math.pi / DFT_N2)
    twr, twi = jnp.cos(tw_ang)[:, :kh, None], -jnp.sin(tw_ang)[:, :kh, None]
    fr, fi = jnp.cos(f_ang)[None, :kh], -jnp.sin(f_ang)[None, :kh]
    gr = twr * fr - twi * fi
    gi = twr * fi + twi * fr
    g_fwd = jnp.concatenate([gr, gi], axis=1)
    kk = jnp.arange(kh)
    wgt = jnp.where(kk > DFT_ROWS, 0.0, jnp.where((kk == 0) | (kk == DFT_ROWS), 1.0, 2.0)) * (1.0 / n)
    twr_t, twi_t = jnp.cos(tw_ang)[:, None, :kh] * wgt, -jnp.sin(tw_ang)[:, None, :kh] * wgt
    fr_t = jnp.cos(f_ang)[None, :DFT_ROWS, :kh]
    fi_t = -jnp.sin(f_ang)[None, :DFT_ROWS, :kh]
    g_inv = jnp.concatenate([twr_t * fr_t - twi_t * fi_t, twr_t * fi_t + twi_t * fr_t], axis=2)
    a = jnp.arange(n1c, dtype=jnp.int32)
    s_ang = ((a[:, None] * a[None, :]) % n1c).astype(f32) * (2 * math.pi / n1c)
    eye = jnp.eye(DFT_ROWS // n1c, dtype=f32)
    sr = jnp.kron(eye, jnp.cos(s_ang))
    si = jnp.kron(eye, -jnp.sin(s_ang))
    m_fwd = jnp.block([[sr, -si], [si, sr]])
    m_inv = jnp.block([[sr, si], [-si, sr]])
    return dict(g_fwd=g_fwd.astype(bf16), g_inv=g_inv.astype(bf16), m_fwd=m_fwd.astype(bf16),
                m_inv=m_inv.astype(bf16), n1=n1c, kh=kh)


def _n1_major(a):
    return jnp.swapaxes(a, 0, 1)


def _n1_minor(mats, shape):
    return jnp.swapaxes(jnp.stack(mats, axis=0), 0, 1).reshape(shape)


def _fft1_body(x_ref, g_ref, o_ref, *, rpb):
    x = _n1_major(x_ref[...])
    bs = [jnp.dot(g_ref[r], x[r].astype(bf16), preferred_element_type=f32) for r in range(rpb)]
    o_ref[...] = _n1_minor(bs, o_ref.shape).astype(bf16)


def _fft1_call(x4, g_fwd):
    B, _, n1c, C = x4.shape
    rpb = min(DFT_RPB, n1c)
    cs = DFT_SLAB
    kh = g_fwd.shape[1] // 2
    return pl.pallas_call(
        functools.partial(_fft1_body, rpb=rpb),
        grid=(B, n1c // rpb, C // cs),
        in_specs=[pl.BlockSpec((None, DFT_ROWS, rpb, cs), lambda b, j, s: (b, 0, j, s)),
                  pl.BlockSpec((rpb, 2 * kh, DFT_ROWS), lambda b, j, s: (j, 0, 0))],
        out_specs=pl.BlockSpec((None, 2, kh, rpb, cs), lambda b, j, s: (b, 0, 0, j, s)),
        out_shape=jax.ShapeDtypeStruct((B, 2, kh, n1c, C), bf16),
        compiler_params=_cparams(("parallel", "arbitrary", "arbitrary")),
        name="fft1",
    )(x4, g_fwd)


def _fft2_body(b_ref, kb_ref, asum_ref, mf_ref, mi_ref, o_ref, k_s, *, nsub):
    def block_rows(i):
        return pl.ds(pl.multiple_of(i * DFT_ROWS, DFT_ROWS), DFT_ROWS)

    def stacked(ref, rows):
        return jnp.concatenate([ref[0, rows, :], ref[1, rows, :]], axis=0)

    @pl.when(pl.program_id(1) == 0)
    def _():
        inv = 1.0 / asum_ref[...]

        def filt_block(i, carry):
            rows = block_rows(i)
            ks = jnp.dot(mf_ref[...], stacked(kb_ref, rows), preferred_element_type=f32)
            k_s[0, rows, :] = ks[:DFT_ROWS] * inv
            k_s[1, rows, :] = ks[DFT_ROWS:] * inv
            return carry

        lax.fori_loop(0, nsub, filt_block, 0)

    def data_block(i, carry):
        rows = block_rows(i)
        xs = jnp.dot(mf_ref[...], stacked(b_ref, rows), preferred_element_type=f32)
        xr, xi = xs[:DFT_ROWS], xs[DFT_ROWS:]
        kr, ki = k_s[0, rows, :], k_s[1, rows, :]
        ys = jnp.concatenate([xr * kr - xi * ki, xr * ki + xi * kr], axis=0).astype(bf16)
        cs = jnp.dot(mi_ref[...], ys, preferred_element_type=f32)
        o_ref[0, rows, :] = cs[:DFT_ROWS].astype(bf16)
        o_ref[1, rows, :] = cs[DFT_ROWS:].astype(bf16)
        return carry

    lax.fori_loop(0, nsub, data_block, 0)


def _fft2_call(bs, kb, asum, order, m_fwd, m_inv):
    B, _, n, C = bs.shape
    blocks = n // DFT_ROWS
    per_step = max(d for d in range(1, min(DFT_STAGE2_BLOCKS, blocks) + 1) if blocks % d == 0)
    rb = per_step * DFT_ROWS
    blk = pl.BlockSpec((None, 2, rb, C), lambda j, b: (b, 0, j, 0))
    return pl.pallas_call(
        functools.partial(_fft2_body, nsub=per_step),
        grid=(n // rb, B),
        in_specs=[blk,
                  pl.BlockSpec((2, rb, C), lambda j, b: (0, j, order)),
                  pl.BlockSpec((1, C), lambda j, b: (0, order)),
                  _const_spec((2 * DFT_ROWS, 2 * DFT_ROWS)),
                  _const_spec((2 * DFT_ROWS, 2 * DFT_ROWS))],
        out_specs=blk,
        out_shape=jax.ShapeDtypeStruct(bs.shape, bf16),
        scratch_shapes=[pltpu.VMEM((2, rb, C), f32)],
        compiler_params=_cparams(("arbitrary", "arbitrary")),
        name="fft2",
    )(bs, kb, asum, m_fwd, m_inv)


def _fft3_body(c_ref, gi_ref, gate_ref, prev_ref, sk_ref, *rest, rpb, fuse_next):
    if fuse_next:
        gf_ref, z_ref, b_ref = rest
    else:
        (z_ref,) = rest
    cs = c_ref.shape[-1]
    c = _n1_major(c_ref[...].astype(f32).reshape(2 * c_ref.shape[1], rpb, cs))
    gate = _n1_major(gate_ref[...])
    prev = _n1_major(prev_ref[...])
    sk = sk_ref[...]
    zs, bs = [], []
    for r in range(rpb):
        y = jnp.dot(gi_ref[r], c[r].astype(bf16), preferred_element_type=f32)
        z = gate[r] * (y + sk * prev[r])
        zs.append(z)
        if fuse_next:
            bs.append(jnp.dot(gf_ref[r], z.astype(bf16), preferred_element_type=f32))
    z_ref[...] = _n1_minor(zs, z_ref.shape)
    if fuse_next:
        b_ref[...] = _n1_minor(bs, b_ref.shape).astype(bf16)


def _fft3_call(cs5, g_inv, gate, prev, sk_row, g_fwd=None):
    B, _, kh, n1c, C = cs5.shape
    rpb = min(DFT_RPB, n1c)
    cs = DFT_SLAB
    tblk = pl.BlockSpec((None, DFT_ROWS, rpb, cs), lambda b, j, s: (b, 0, j, s))
    sblk = pl.BlockSpec((None, 2, kh, rpb, cs), lambda b, j, s: (b, 0, 0, j, s))
    in_specs = [sblk, pl.BlockSpec((rpb, DFT_ROWS, 2 * kh), lambda b, j, s: (j, 0, 0)), tblk, tblk,
                pl.BlockSpec((1, cs), lambda b, j, s: (0, s))]
    args = [cs5, g_inv, gate, prev, sk_row]
    out_specs = [tblk]
    out_shape = [jax.ShapeDtypeStruct(gate.shape, f32)]
    fuse_next = g_fwd is not None
    if fuse_next:
        in_specs.append(pl.BlockSpec((rpb, 2 * kh, DFT_ROWS), lambda b, j, s: (j, 0, 0)))
        args.append(g_fwd)
        out_specs.append(sblk)
        out_shape.append(jax.ShapeDtypeStruct(cs5.shape, bf16))
    return pl.pallas_call(
        functools.partial(_fft3_body, rpb=rpb, fuse_next=fuse_next),
        grid=(B, n1c // rpb, C // cs),
        in_specs=in_specs,
        out_specs=out_specs,
        out_shape=out_shape,
        compiler_params=_cparams(("parallel", "arbitrary", "arbitrary")),
        name="fft3_next" if fuse_next else "fft3",
    )(*args)


def _filter_spectrum(L, tabs, filt_params):
    bs, asum = _filter_call(L, tabs["g_fwd"], *filt_params)
    return bs.reshape(2, tabs["kh"] * tabs["n1"], bs.shape[-1]), asum


def _hyena_conv(v, x1, x2, ksp, skip, tabs):
    n1c, kh = tabs["n1"], tabs["kh"]
    B, L, C = v.shape
    n = kh * n1c
    kb, asum = ksp
    v4, x14, x24 = (a.reshape(B, DFT_ROWS, n1c, C) for a in (v, x1, x2))
    s5 = (B, 2, kh, n1c, C)
    bs = _fft1_call(v4, tabs["g_fwd"])
    cs = _fft2_call(bs.reshape(B, 2, n, C), kb, asum, 0, tabs["m_fwd"], tabs["m_inv"])
    z1, bs = _fft3_call(cs.reshape(s5), tabs["g_inv"], x14, v4, skip[0:1], tabs["g_fwd"])
    cs = _fft2_call(bs.reshape(B, 2, n, C), kb, asum, 1, tabs["m_fwd"], tabs["m_inv"])
    (z2,) = _fft3_call(cs.reshape(s5), tabs["g_inv"], x24, z1, skip[1:2])
    return z2.reshape(B, L, C)


def _route(r):
    lane = lax.broadcasted_iota(jnp.int32, r.shape, 1)
    ninf = jnp.float32(-jnp.inf)
    big = jnp.int32(1 << 20)
    is_g = lane < N_GROUPS
    gmax = jnp.max(jnp.where(is_g, r, ninf), axis=-1, keepdims=True)
    gidx = jnp.min(jnp.where(jnp.logical_and(is_g, r == gmax), lane, big), axis=-1, keepdims=True)
    gw = 1.0 / jnp.sum(jnp.where(is_g, jnp.exp(r - gmax), 0.0), axis=-1, keepdims=True)
    e_lane = lane - N_GROUPS
    sel = jnp.logical_and(jnp.logical_and(e_lane >= 0, e_lane < N_EXPERTS), (e_lane >> 2) == gidx)
    le = jnp.where(sel, r, ninf)
    m1 = jnp.max(le, axis=-1, keepdims=True)
    i1 = jnp.min(jnp.where(le == m1, lane, big), axis=-1, keepdims=True)
    le2 = jnp.where(lane == i1, ninf, le)
    m2 = jnp.max(le2, axis=-1, keepdims=True)
    i2 = jnp.min(jnp.where(le2 == m2, lane, big), axis=-1, keepdims=True)
    e2 = jnp.exp(m2 - m1)
    den = 1.0 + e2
    comb = jnp.where(lane == i1, gw / den, jnp.where(lane == i2, gw * e2 / den, 0.0))
    return comb, gidx


def _mix_body(x_ref, z_ref, g0_ref, rest_ref, ga1_ref, sh2_ref, sc2_ref, n2g_ref, wba_ref, wout_ref, wr_ref, tri_ref,
              xo_ref, hs_ref, combs_ref, pmt_ref, cnt_ref):
    ya = jnp.dot(z_ref[...].astype(bf16), wba_ref[...], preferred_element_type=f32)
    merged = g0_ref[...] * ya + rest_ref[...]
    xo = x_ref[...] + ga1_ref[...] * jnp.dot(merged.astype(bf16), wout_ref[...], preferred_element_type=f32)
    xo_ref[...] = xo
    ms = jnp.mean(xo * xo, axis=-1, keepdims=True)
    h2 = xo * lax.rsqrt(ms + EPS) * n2g_ref[...]
    h2 = h2 * (1.0 + sc2_ref[...]) + sh2_ref[...]
    h_hi = h2.astype(bf16)
    h_lo = (h2 - h_hi.astype(f32)).astype(bf16)
    p_hi = jnp.dot(h_hi, wr_ref[...], preferred_element_type=f32)
    p_lo = jnp.dot(h_lo, wr_ref[:, 0:LANES], preferred_element_type=f32)
    comb, gidx = _route(p_hi[:, 0:LANES] + p_hi[:, LANES:2 * LANES] + p_lo)

    tm = comb.shape[0]
    lane = lax.broadcasted_iota(jnp.int32, comb.shape, 1)
    onehot = (lane == gidx).astype(f32)
    cum = jnp.dot(tri_ref[...], onehot.astype(bf16), preferred_element_type=f32)
    tot8 = cum[tm - 8:tm, :]
    off8 = pltpu.roll(tot8, 1, 1) + pltpu.roll(tot8, 2, 1) + pltpu.roll(tot8, 3, 1)
    rank = jnp.sum(onehot * (off8[7:8, :] + cum - 1.0), axis=-1, keepdims=True)
    slot = lax.broadcasted_iota(jnp.int32, (tm, tm), 1).astype(f32)
    pmt = (slot == rank).astype(bf16)
    pmt_ref[...] = pmt
    both = jnp.concatenate([h_hi, comb.astype(bf16)], axis=1)
    srt = lax.dot_general(pmt, both, (((0,), (0,)), ((), ())), preferred_element_type=f32).astype(bf16)
    d = h2.shape[1]
    hs_ref[...] = srt[:, 0:d]
    combs_ref[...] = srt[:, d:d + LANES]
    cnt_ref[...] = tot8[7:8, :].astype(jnp.int32)


def _mix_call(x, z, g0, rest, ga1, sh2, sc2, n2g, wba, wout, wr, tri, tm):
    B, L, D = x.shape
    nt = L // tm
    row = lambda b, i: (b, i, 0)
    vspec = pl.BlockSpec((None, 1, D), lambda b, i: (b, 0, 0))
    return pl.pallas_call(
        _mix_body,
        grid=(B, nt),
        in_specs=[
            pl.BlockSpec((None, tm, D), row),
            pl.BlockSpec((None, tm, W_MIX), row),
            pl.BlockSpec((None, tm, D), row),
            pl.BlockSpec((None, tm, D), row),
            vspec, vspec, vspec,
            _const_spec((1, D)),
            _const_spec((W_MIX, D)),
            _const_spec((D, D)),
            _const_spec((D, 2 * LANES)),
            _const_spec((tm, tm)),
        ],
        out_specs=[pl.BlockSpec((None, tm, D), row), pl.BlockSpec((None, tm, D), row),
                   pl.BlockSpec((None, tm, LANES), row), pl.BlockSpec((None, tm, tm), row),
                   pl.BlockSpec((None, None, 1, LANES), lambda b, i: (b, i, 0, 0))],
        out_shape=[jax.ShapeDtypeStruct((B, L, D), f32), jax.ShapeDtypeStruct((B, L, D), bf16),
                   jax.ShapeDtypeStruct((B, L, LANES), bf16), jax.ShapeDtypeStruct((B, L, tm), bf16),
                   jax.ShapeDtypeStruct((B, nt, 1, LANES), jnp.int32)],
        compiler_params=_cparams(("parallel", "arbitrary")),
        name="mix",
    )(x, z, g0, rest, ga1, sh2, sc2, n2g, wba, wout, wr, tri)


def _experts_body(cnt_ref, hs_ref, combs_ref, pmt_ref, xo_ref, ga2_ref, ex_ref, w1_ref, w3_ref, w2_ref, fg_ref, o_ref,
                  acc_s, *, tm, sub, final_norm):
    b, sup, g = pl.program_id(0), pl.program_id(1), pl.program_id(2)
    nchunk = tm // MOE_CHUNK

    @pl.when(g == 0)
    def _():
        acc_s[...] = jnp.zeros_like(acc_s)

    def tile(t, carry):
        base = ((b * pl.num_programs(1) + sup) * sub + t) * N_GROUPS
        lo = jnp.int32(0)
        for gg in range(N_GROUPS - 1):
            lo = lo + jnp.where(gg < g, cnt_ref[base + gg], 0)
        hi = lo + cnt_ref[base + g]

        def run(start, nrows):
            rows = pl.ds(pl.multiple_of(t * tm + start, MOE_CHUNK), nrows)
            h = hs_ref[rows, :]
            a = jnp.dot(h, w1_ref[...], preferred_element_type=f32)
            u = jnp.dot(h, w3_ref[...], preferred_element_type=f32)
            cw = jnp.dot(combs_ref[rows, :], ex_ref[...], preferred_element_type=f32)
            silu = 0.5 * a * (jnp.tanh(0.5 * a) + 1.0)
            acc_s[rows, :] += jnp.dot((silu * u * cw).astype(bf16), w2_ref[...], preferred_element_type=f32)

        win = jnp.minimum(lo - (lo & (MOE_CHUNK - 1)), tm - MOE_WINDOW)

        @pl.when(hi > lo)
        def _():
            run(win, MOE_WINDOW)

        for c in range(nchunk):
            @pl.when(jnp.logical_and(c * MOE_CHUNK >= win + MOE_WINDOW, hi > c * MOE_CHUNK))
            def _():
                run(c * MOE_CHUNK, MOE_CHUNK)
        return carry

    lax.fori_loop(0, sub, tile, 0)

    @pl.when(g == pl.num_programs(2) - 1)
    def _():
        for t in range(sub):
            rows = slice(t * tm, (t + 1) * tm)
            y2 = jnp.dot(pmt_ref[rows, :], acc_s[rows, :].astype(bf16), preferred_element_type=f32)
            y = xo_ref[rows, :] + ga2_ref[...] * y2
            if final_norm:
                ms = jnp.mean(y * y, axis=-1, keepdims=True)
                y = y * lax.rsqrt(ms + EPS) * fg_ref[...]
            o_ref[rows, :] = y


def _experts_call(cnt, hs, combs, pmt, xo, ga2, ex, w1, w3, w2, fg, tm, final_norm):
    B, L, D = hs.shape
    sub = min(MOE_SUB, L // tm)
    rows = sub * tm
    blk = lambda b, s, g, cnt: (b, s, 0)
    return pl.pallas_call(
        functools.partial(_experts_body, tm=tm, sub=sub, final_norm=final_norm),
        grid_spec=pltpu.PrefetchScalarGridSpec(
            num_scalar_prefetch=1,
            grid=(B, L // rows, N_GROUPS),
            in_specs=[
                pl.BlockSpec((None, rows, D), blk),
                pl.BlockSpec((None, rows, LANES), blk),
                pl.BlockSpec((None, rows, tm), blk),
                pl.BlockSpec((None, rows, D), blk),
                pl.BlockSpec((None, 1, D), lambda b, s, g, cnt: (b, 0, 0)),
                pl.BlockSpec((None, LANES, GROUP_HID), lambda b, s, g, cnt: (g, 0, 0)),
                pl.BlockSpec((D, GROUP_HID), lambda b, s, g, cnt: (0, g)),
                pl.BlockSpec((D, GROUP_HID), lambda b, s, g, cnt: (0, g)),
                pl.BlockSpec((GROUP_HID, D), lambda b, s, g, cnt: (g, 0)),
                pl.BlockSpec((1, D), lambda b, s, g, cnt: (0, 0)),
            ],
            out_specs=pl.BlockSpec((None, rows, D), blk),
            scratch_shapes=[pltpu.VMEM((rows, D), f32)],
        ),
        out_shape=jax.ShapeDtypeStruct((B, L, D), f32),
        compiler_params=_cparams(("arbitrary", "arbitrary", "arbitrary")),
        name="experts",
    )(cnt, hs, combs, pmt, xo, ga2, ex, w1, w3, w2, fg)


def _pad_to(a, shape):
    return jnp.pad(a, [(0, s - d) for d, s in zip(a.shape, shape)])


def _prep_layer(l, p):
    max_decay = math.log(HY_TARGET) / HY_FAST_DECAY
    min_decay = math.log(HY_TARGET) / HY_SLOW_DECAY
    deltas = jnp.abs(jnp.linspace(min_decay, max_decay, W_MIX, dtype=f32))
    router = jnp.concatenate([p["router_g"][l], p["router_e"][l]], axis=1)
    lanes = jnp.arange(LANES)[None, :, None]
    cols = jnp.arange(GROUP_HID)[None, None, :]
    grp = jnp.arange(N_GROUPS)[:, None, None]
    expand = (lanes == N_GROUPS + EXP_PER_GROUP * grp + cols // D_EXPERT).astype(bf16)
    router = _pad_to(router, (D_MODEL, LANES))
    router_hi = router.astype(bf16)
    router_lo = (router - router_hi.astype(f32)).astype(bf16)
    zh = jnp.zeros((HY_HID, HY_HID), f32)
    w1 = _pad_to(p["hy_w1"][l], (LANES, HY_HID))
    zw1 = jnp.zeros_like(w1)
    w1_pair = jnp.block([[w1, zw1], [zw1, w1]])
    w2_pair = jnp.block([[p["hy_w2"][l], zh], [zh, p["hy_w2"][l]]])
    pair = lambda a: jnp.concatenate([a, a])[None]
    cw = HY_ORDER * W_MIX
    wo = p["hy_w_out"][l]
    zwo = jnp.zeros((HY_HID, cw), f32)
    wo_f = jnp.concatenate([wo[:, :cw], zwo], axis=0).astype(bf16)
    wo_b = jnp.concatenate([zwo, wo[:, cw:]], axis=0).astype(bf16)
    return dict(
        norm1_g=p["norm1_g"][l][None], norm2_g=p["norm2_g"][l][None],
        w_in=p["w_in"][l].astype(bf16), hy_conv_w=p["hy_conv_w"][l], hy_skip=p["hy_skip"][l],
        pool_w=p["pool_w"][l].astype(bf16), pool_scale=p["pool_scale"][l][None], sc_conv_w=p["sc_conv_w"][l],
        w_br_a=p["w_br_a"][l].astype(bf16), w_br_b=p["w_br_b"][l].astype(bf16), w_br_c=p["w_br_c"][l].astype(bf16),
        w_out=p["w_out"][l].astype(bf16),
        router=jnp.concatenate([router_hi, router_lo], axis=1), expand=expand,
        moe_w1=p["moe_w1"][l].astype(bf16), moe_w3=p["moe_w3"][l].astype(bf16), moe_w2=p["moe_w2"][l].astype(bf16),
        filt=(w1_pair, pair(p["hy_b1"][l]), w2_pair, pair(p["hy_b2"][l]), pair(p["hy_freq"][l]), wo_f, wo_b,
              pair(deltas)),
    )


def _tile(L, want):
    return want if L % want == 0 else L


def _encoder_layer(x, mod, lp, ksp, tabs, final_g, final_norm):
    B, L, D = x.shape
    sh1, sc1, ga1, sh2, sc2, ga2 = (m[:, None, :] for m in jnp.split(mod, 6, axis=-1))
    v, x1, x2, g0, rest = _proj_call(x, sh1, sc1, lp["norm1_g"], lp["w_in"], lp["hy_conv_w"], lp["pool_w"],
                                     lp["pool_scale"], lp["sc_conv_w"], lp["w_br_b"], lp["w_br_c"], _tile(L, 512))
    z = _hyena_conv(v, x1, x2, ksp, lp["hy_skip"], tabs)
    tm = _tile(L, MOE_TILE)
    tri = jnp.tri(tm, dtype=bf16)
    xo, hs, combs, pmt, cnt = _mix_call(x, z, g0, rest, ga1, sh2, sc2, lp["norm2_g"], lp["w_br_a"], lp["w_out"],
                                        lp["router"], tri, tm)
    return _experts_call(cnt[:, :, 0, :N_GROUPS].reshape(-1), hs, combs, pmt, xo, ga2, lp["expand"], lp["moe_w1"],
                         lp["moe_w3"], lp["moe_w2"], final_g, tm, final_norm)


def _forward(xs, cs, p, final_g):
    depth = p["w_in"].shape[0]
    nb = [c.shape[0] for c in cs]
    rows = -(-sum(nb) // 8) * 8
    c_all = _pad_to(jnp.concatenate(cs, axis=0), (rows, D_MODEL))
    lens = sorted({x.shape[1] for x in xs})
    tabs = {L: _dft_tables(L) for L in lens}
    fg = final_g[None]
    mods = _mod_call(c_all, p["ada_w"], p["ada_b"])
    for l in range(depth):
        lp = _prep_layer(l, p)
        mod = mods[l]
        ksp = {L: _filter_spectrum(L, tabs[L], lp["filt"]) for L in lens}
        off = 0
        out = []
        for x, n in zip(xs, nb):
            L = x.shape[1]
            out.append(_encoder_layer(x, mod[off:off + n], lp, ksp[L], tabs[L], fg, l == depth - 1))
            off += n
        xs = out
    return xs


def kernel(x_prompt, x_sample, c_prompt, c_sample, ada_w, ada_b, norm1_g, norm2_g, w_in, hy_conv_w, hy_skip, hy_w1, hy_b1, hy_w2, hy_b2, hy_w_out, hy_freq, pool_w, pool_scale, sc_conv_w, w_br_a, w_br_b, w_br_c, w_out, router_g, router_e, moe_w1, moe_w3, moe_w2, final_g):
    p = dict(ada_w=ada_w, ada_b=ada_b, norm1_g=norm1_g, norm2_g=norm2_g, w_in=w_in, hy_conv_w=hy_conv_w,
             hy_skip=hy_skip, hy_w1=hy_w1, hy_b1=hy_b1, hy_w2=hy_w2, hy_b2=hy_b2, hy_w_out=hy_w_out, hy_freq=hy_freq,
             pool_w=pool_w, pool_scale=pool_scale, sc_conv_w=sc_conv_w, w_br_a=w_br_a, w_br_b=w_br_b, w_br_c=w_br_c,
             w_out=w_out, router_g=router_g, router_e=router_e, moe_w1=moe_w1, moe_w3=moe_w3, moe_w2=moe_w2)
    y_prompt, y_sample = _forward([x_prompt, x_sample], [c_prompt, c_sample], p, final_g)
    return (y_prompt, y_sample)
```

```python
import functools
import math

import jax
import jax.numpy as jnp
from jax import lax
from jax.experimental import pallas as pl
from jax.experimental.pallas import tpu as pltpu

f32 = jnp.float32
bf16 = jnp.bfloat16
HIGHEST = lax.Precision.HIGHEST

D_MODEL = 1024
DEPTH = 2
W_MIX = 512
HY_ORDER = 2
HY_BANDS = 16
HY_HID = 64
HY_FAST_DECAY = 0.3
HY_SLOW_DECAY = 1.5
HY_TARGET = 1e-2
POOL_WINDOWS = (2, 4, 8, 16)
POOL_GROUP = W_MIX // len(POOL_WINDOWS)
COL_HY = 3 * W_MIX
COL_POOL = W_MIX
COL_SC = 3 * W_MIX
COL_GATE = 3 * D_MODEL
OFF_POOL = COL_HY
OFF_SC = COL_HY + COL_POOL
OFF_GATE = COL_HY + COL_POOL + COL_SC
PROJ_COLS = OFF_GATE + COL_GATE
N_GROUPS = 4
EXP_PER_GROUP = 4
N_EXPERTS = N_GROUPS * EXP_PER_GROUP
D_EXPERT = 256
GROUP_HID = EXP_PER_GROUP * D_EXPERT
EPS = 1e-6

HALO = 8
DFT_N2 = 256
DFT_ROWS = 128
DFT_RPB = 16
DFT_SLAB = 256
DFT_STAGE2_BLOCKS = 17
MOE_TILE = 512
MOE_CHUNK = 128
MOE_WINDOW = 256
LANES = 128
VMEM_LIMIT = 56 * 1024 * 1024


def _cparams(sem):
    return pltpu.CompilerParams(dimension_semantics=sem, vmem_limit_bytes=VMEM_LIMIT)


def _const_spec(shape):
    nd = len(shape)
    return pl.BlockSpec(shape, lambda *_: (0,) * nd, pipeline_mode=pl.Buffered(1))


def _mod_body(c_ref, w_ref, b_ref, o_ref):
    c = c_ref[...]
    s = c * jax.nn.sigmoid(c)
    o_ref[...] = jnp.dot(s, w_ref[...], preferred_element_type=f32, precision=HIGHEST) + b_ref[...]


def _mod_call(c_all, ada_w, ada_b):
    rows = c_all.shape[0]
    depth = ada_w.shape[0]
    tn = 1536
    return pl.pallas_call(
        _mod_body,
        grid=(depth, 6 * D_MODEL // tn),
        in_specs=[pl.BlockSpec((rows, D_MODEL), lambda l, j: (0, 0)),
                  pl.BlockSpec((None, D_MODEL, tn), lambda l, j: (l, 0, j)),
                  pl.BlockSpec((None, 1, tn), lambda l, j: (l, 0, j))],
        out_specs=pl.BlockSpec((None, rows, tn), lambda l, j: (l, 0, j)),
        out_shape=jax.ShapeDtypeStruct((depth, rows, 6 * D_MODEL), f32),
        compiler_params=_cparams(("arbitrary", "arbitrary")),
        name="mod",
    )(c_all, ada_w, ada_b[:, None, :])


def _proj_body(xm_ref, xp_ref, xn_ref, sh_ref, sc_ref, g_ref, win_ref, hyw_ref, pw_ref, ps_ref, scw_ref,
               wbb_ref, wbc_ref, v_ref, x1_ref, x2_ref, g0_ref, rest_ref, *, tm, seq_len):
    i = pl.program_id(1)
    nt = pl.num_programs(1)
    rt = tm + 2 * HALO
    ctr = slice(HALO, HALO + tm)

    def modulated(x):
        ms = jnp.mean(x * x, axis=-1, keepdims=True)
        h = x * lax.rsqrt(ms + EPS) * g_ref[...]
        return h * (1.0 + sc_ref[...]) + sh_ref[...]

    hp = jnp.where(i > 0, modulated(xp_ref[...]), 0.0)
    hn = jnp.where(i < nt - 1, modulated(xn_ref[...]), 0.0)
    hc = modulated(xm_ref[...])
    hb = jnp.concatenate([hp, hc, hn], axis=0).astype(bf16)
    hcb = hc.astype(bf16)

    def down(a, s):
        return pltpu.roll(a, s, 0)

    def up(a, s):
        return pltpu.roll(a, rt - s, 0)

    u = jnp.dot(hb, win_ref[:, 0:COL_HY], preferred_element_type=f32)
    w = hyw_ref[...]
    uc = (down(u, 1) * w[0:1] + u * w[1:2] + up(u, 1) * w[2:3])[ctr]
    v_ref[...] = uc[:, 0:W_MIX]
    x1_ref[...] = uc[:, W_MIX:2 * W_MIX]
    x2_ref[...] = uc[:, 2 * W_MIX:3 * W_MIX]

    q = jnp.dot(hb, win_ref[:, OFF_POOL:OFF_POOL + COL_POOL], preferred_element_type=f32)
    s2 = q + down(q, 1)
    s4 = s2 + down(s2, 2)
    s8 = s4 + down(s4, 4)
    s16 = s8 + down(s8, 8)
    tpos = i * tm + lax.broadcasted_iota(jnp.int32, (tm, 1), 0)
    pooled = []
    for g, (win, ssum) in enumerate(zip(POOL_WINDOWS, (s2, s4, s8, s16))):
        lo = win // 2
        hi = win - 1 - lo
        lanes = slice(g * POOL_GROUP, (g + 1) * POOL_GROUP)
        ws = ssum[:, lanes]
        if hi > 0:
            ws = up(ws, hi)
        cnt = (jnp.minimum(tpos + hi + 1, seq_len) - jnp.maximum(tpos - lo, 0)).astype(f32)
        p = ws[ctr] / cnt - q[ctr, lanes]
        pooled.append(jnp.dot(p.astype(bf16), pw_ref[g], preferred_element_type=f32))
    yb_in = jnp.concatenate(pooled, axis=1) * ps_ref[...]
    yb = jnp.dot(yb_in.astype(bf16), wbb_ref[...], preferred_element_type=f32)

    us = jnp.dot(hb, win_ref[:, OFF_SC:OFF_SC + COL_SC], preferred_element_type=f32)
    cx = us[:, W_MIX:2 * W_MIX] * us[:, 2 * W_MIX:3 * W_MIX]
    sw = scw_ref[...]
    dw = down(cx, 1) * sw[0:1] + cx * sw[1:2] + up(cx, 1) * sw[2:3]
    sc_out = (us[:, 0:W_MIX] * dw)[ctr]
    yc = jnp.dot(sc_out.astype(bf16), wbc_ref[...], preferred_element_type=f32)

    gt = 0.5 * jnp.tanh(0.5 * jnp.dot(hcb, win_ref[:, OFF_GATE:PROJ_COLS], preferred_element_type=f32)) + 0.5
    g0_ref[...] = gt[:, 0:D_MODEL]
    rest_ref[...] = gt[:, D_MODEL:2 * D_MODEL] * yb + gt[:, 2 * D_MODEL:3 * D_MODEL] * yc


def _proj_call(x, sh, sc, g1, win, hyw, pw, ps, scw, wbb, wbc, tm):
    B, L, D = x.shape
    nt = L // tm
    hb = tm // HALO
    row = lambda b, i: (b, i, 0)
    vec = lambda b, i: (b, 0, 0)
    out_w = jax.ShapeDtypeStruct((B, L, W_MIX), f32)
    out_d = jax.ShapeDtypeStruct((B, L, D), f32)
    return pl.pallas_call(
        functools.partial(_proj_body, tm=tm, seq_len=L),
        grid=(B, nt),
        in_specs=[
            pl.BlockSpec((None, tm, D), row),
            pl.BlockSpec((None, HALO, D), lambda b, i: (b, jnp.maximum(i * hb - 1, 0), 0)),
            pl.BlockSpec((None, HALO, D), lambda b, i: (b, jnp.minimum((i + 1) * hb, L // HALO - 1), 0)),
            pl.BlockSpec((None, 1, D), vec),
            pl.BlockSpec((None, 1, D), vec),
            _const_spec((1, D)),
            _const_spec((D, PROJ_COLS)),
            _const_spec((3, COL_HY)),
            _const_spec((len(POOL_WINDOWS), POOL_GROUP, POOL_GROUP)),
            _const_spec((1, W_MIX)),
            _const_spec((3, W_MIX)),
            _const_spec((W_MIX, D)),
            _const_spec((W_MIX, D)),
        ],
        out_specs=[pl.BlockSpec((None, tm, W_MIX), row)] * 3 + [pl.BlockSpec((None, tm, D), row)] * 2,
        out_shape=[out_w, out_w, out_w, out_d, out_d],
        compiler_params=_cparams(("parallel", "arbitrary")),
        name="proj",
    )(x, x, x, sh, sc, g1, win, hyw, pw, ps, scw, wbb, wbc)


def _filter_body(ca_ref, sa_ref, cb_ref, sb_ref, w1_ref, b1_ref, w2_ref, b2_ref, fr_ref, wo0_ref, wo1_ref, dl_ref, g_ref,
                 o_ref, asum_ref, h_s, *, rpb, seq_len, n1c):
    j = pl.program_id(0)
    s = pl.program_id(1)
    L = seq_len
    n2 = lax.broadcasted_iota(jnp.int32, (DFT_N2, 1), 0)
    fwd = n2 < DFT_ROWS

    def slot(r):
        pos = (j * rpb + r) + n1c * n2
        return pos, jnp.where(fwd, pos, 2 * L - pos).astype(f32)

    @pl.when(s == 0)
    def _():
        lane = lax.broadcasted_iota(jnp.int32, (DFT_N2, LANES), 1)
        fr = fr_ref[...]
        cb, sb = cb_ref[...], sb_ref[...]
        zs = []
        for r in range(rpb):
            _, lag = slot(r)
            ca = ca_ref[r:r + 1, :]
            sa = jnp.where(fwd, sa_ref[r:r + 1, :], -sa_ref[r:r + 1, :])
            cos_t = ca * cb - sa * sb
            sin_t = sa * cb + ca * sb
            z = jnp.where(lane == 0, lag / (L - 1), jnp.where(lane <= HY_BANDS, cos_t, -sin_t))
            zs.append(jnp.concatenate([z[:DFT_ROWS], z[DFT_ROWS:]], axis=1))
        zz = jnp.concatenate(zs, axis=0)
        h = jnp.sin(fr * (jnp.dot(zz, w1_ref[...], preferred_element_type=f32, precision=HIGHEST) + b1_ref[...]))
        h_s[...] = jnp.sin(fr * (jnp.dot(h, w2_ref[...], preferred_element_type=f32, precision=HIGHEST) + b2_ref[...]))

    hb = h_s[...].astype(bf16)
    ho_f = jnp.dot(hb, wo0_ref[...], preferred_element_type=f32)
    ho_b = jnp.dot(hb, wo1_ref[...], preferred_element_type=f32)
    asum = jnp.zeros(asum_ref.shape[1:], f32)
    bs = []
    for r in range(rpb):
        pos, lag = slot(r)
        rows = slice(r * DFT_ROWS, (r + 1) * DFT_ROWS)
        ho = jnp.concatenate([ho_f[rows], ho_b[rows]], axis=0)
        k = jnp.where(pos == L, 0.0, ho * jnp.exp(-(lag / (L - 1)) * dl_ref[...]))
        asum = asum + jnp.sum(jnp.abs(k), axis=0, keepdims=True)
        bs.append(jnp.dot(g_ref[r], k.astype(bf16), preferred_element_type=f32))
    o_ref[...] = jnp.swapaxes(jnp.stack(bs, axis=0), 0, 1).reshape(o_ref.shape).astype(bf16)

    @pl.when(j == 0)
    def _():
        asum_ref[s] = asum

    @pl.when(j > 0)
    def _():
        asum_ref[s] += asum


def _filter_tables(L):
    n1c = 2 * L // DFT_N2
    bands = jnp.linspace(1e-4, HY_BANDS - 1, HY_BANDS, dtype=f32)
    brow = jnp.zeros((LANES,), f32).at[1:1 + HY_BANDS].set(bands).at[1 + HY_BANDS:1 + 2 * HY_BANDS].set(bands)
    used = (jnp.arange(LANES) >= 1) & (jnp.arange(LANES) <= 2 * HY_BANDS)
    n2 = jnp.arange(DFT_N2)
    part_a = jnp.arange(n1c).astype(f32)
    part_b = (n1c * jnp.where(n2 < DFT_ROWS, n2, DFT_N2 - n2)).astype(f32)

    def cs(part):
        ang = (2 * math.pi / L) * part[:, None] * brow[None, :]
        return jnp.where(used, jnp.cos(ang), 0.0), jnp.where(used, jnp.sin(ang), 0.0)

    return cs(part_a) + cs(part_b)


def _filter_call(L, g_fwd, w1p, b1p, w2p, b2p, frp, wo_f, wo_b, dl_row):
    n1c = 2 * L // DFT_N2
    rpb = min(DFT_RPB, n1c)
    cw = HY_ORDER * W_MIX
    cs = DFT_SLAB
    ns = cw // cs
    ca, sa, cb, sb = _filter_tables(L)
    kh = g_fwd.shape[1] // 2
    bs, asum = pl.pallas_call(
        functools.partial(_filter_body, rpb=rpb, seq_len=L, n1c=n1c),
        grid=(n1c // rpb, ns),
        in_specs=[pl.BlockSpec((rpb, LANES), lambda j, s: (j, 0)), pl.BlockSpec((rpb, LANES), lambda j, s: (j, 0)),
                  _const_spec((DFT_N2, LANES)), _const_spec((DFT_N2, LANES)),
                  _const_spec((2 * LANES, LANES)), _const_spec((1, LANES)),
                  _const_spec((LANES, LANES)), _const_spec((1, LANES)), _const_spec((1, LANES)),
                  pl.BlockSpec((LANES, cs), lambda j, s: (0, s)),
                  pl.BlockSpec((LANES, cs), lambda j, s: (0, s)),
                  pl.BlockSpec((1, cs), lambda j, s: (0, s)),
                  pl.BlockSpec((rpb, 2 * kh, DFT_N2), lambda j, s: (j, 0, 0))],
        out_specs=[pl.BlockSpec((2, kh, rpb, cs), lambda j, s: (0, 0, j, s)),
                   pl.BlockSpec((ns, 1, cs), lambda j, s: (0, 0, 0))],
        out_shape=[jax.ShapeDtypeStruct((2, kh, n1c, cw), bf16), jax.ShapeDtypeStruct((ns, 1, cs), f32)],
        scratch_shapes=[pltpu.VMEM((rpb * DFT_ROWS, LANES), f32)],
        compiler_params=_cparams(("arbitrary", "arbitrary")),
        name="filt",
    )(ca, sa, cb, sb, w1p, b1p, w2p, b2p, frp, wo_f, wo_b, dl_row, g_fwd)
    return bs, asum.reshape(1, cw)


def _dft_half_rows(n1c):
    step = max(8, DFT_ROWS // n1c)
    return -(-(DFT_ROWS + 1) // step) * step


def _dft_tables(L):
    n = 2 * L
    n1c = n // DFT_N2
    kh = _dft_half_rows(n1c)
    k2 = jnp.arange(DFT_N2, dtype=jnp.int32)
    tw_ang = ((jnp.arange(n1c, dtype=jnp.int32)[:, None] * k2[None, :]) % n).astype(f32) * (2 * math.pi / n)
    f_ang = ((k2[:, None] * k2[None, :]) % DFT_N2).astype(f32) * (2 * math.pi / DFT_N2)
    twr, twi = jnp.cos(tw_ang)[:, :kh, None], -jnp.sin(tw_ang)[:, :kh, None]
    fr, fi = jnp.cos(f_ang)[None, :kh], -jnp.sin(f_ang)[None, :kh]
    gr = twr * fr - twi * fi
    gi = twr * fi + twi * fr
    g_fwd = jnp.concatenate([gr, gi], axis=1)
    kk = jnp.arange(kh)
    wgt = jnp.where(kk > DFT_ROWS, 0.0, jnp.where((kk == 0) | (kk == DFT_ROWS), 1.0, 2.0)) * (1.0 / n)
    twr_t, twi_t = jnp.cos(tw_ang)[:, None, :kh] * wgt, -jnp.sin(tw_ang)[:, None, :kh] * wgt
    fr_t = jnp.cos(f_ang)[None, :DFT_ROWS, :kh]
    fi_t = -jnp.sin(f_ang)[None, :DFT_ROWS, :kh]
    g_inv = jnp.concatenate([twr_t * fr_t - twi_t * fi_t, twr_t * fi_t + twi_t * fr_t], axis=2)
    a = jnp.arange(n1c, dtype=jnp.int32)
    s_ang = ((a[:, None] * a[None, :]) % n1c).astype(f32) * (2 * math.pi / n1c)
    eye = jnp.eye(DFT_ROWS // n1c, dtype=f32)
    sr = jnp.kron(eye, jnp.cos(s_ang))
    si = jnp.kron(eye, -jnp.sin(s_ang))
    m_fwd = jnp.block([[sr, -si], [si, sr]])
    m_inv = jnp.block([[sr, si], [-si, sr]])
    return dict(g_fwd=g_fwd.astype(bf16), g_inv=g_inv.astype(bf16), m_fwd=m_fwd.astype(bf16),
                m_inv=m_inv.astype(bf16), n1=n1c, kh=kh)


def _n1_major(a):
    return jnp.swapaxes(a, 0, 1)


def _n1_minor(mats, shape):
    return jnp.swapaxes(jnp.stack(mats, axis=0), 0, 1).reshape(shape)


def _fft1_body(x_ref, g_ref, o_ref, *, rpb):
    x = _n1_major(x_ref[...])
    bs = [jnp.dot(g_ref[r], x[r].astype(bf16), preferred_element_type=f32) for r in range(rpb)]
    o_ref[...] = _n1_minor(bs, o_ref.shape).astype(bf16)


def _fft1_call(x4, g_fwd):
    B, _, n1c, C = x4.shape
    rpb = min(DFT_RPB, n1c)
    cs = DFT_SLAB
    kh = g_fwd.shape[1] // 2
    return pl.pallas_call(
        functools.partial(_fft1_body, rpb=rpb),
        grid=(B, n1c // rpb, C // cs),
        in_specs=[pl.BlockSpec((None, DFT_ROWS, rpb, cs), lambda b, j, s: (b, 0, j, s)),
                  pl.BlockSpec((rpb, 2 * kh, DFT_ROWS), lambda b, j, s: (j, 0, 0))],
        out_specs=pl.BlockSpec((None, 2, kh, rpb, cs), lambda b, j, s: (b, 0, 0, j, s)),
        out_shape=jax.ShapeDtypeStruct((B, 2, kh, n1c, C), bf16),
        compiler_params=_cparams(("parallel", "arbitrary", "arbitrary")),
        name="fft1",
    )(x4, g_fwd)


def _fft2_body(b_ref, kb_ref, asum_ref, mf_ref, mi_ref, o_ref, k_s, *, nsub):
    def block_rows(i):
        return slice(i * DFT_ROWS, (i + 1) * DFT_ROWS)

    def stacked(ref, rows):
        return jnp.concatenate([ref[0, rows, :], ref[1, rows, :]], axis=0)

    @pl.when(pl.program_id(1) == 0)
    def _():
        inv = 1.0 / asum_ref[...]
        for i in range(nsub):
            rows = block_rows(i)
            ks = jnp.dot(mf_ref[...], stacked(kb_ref, rows), preferred_element_type=f32)
            k_s[0, rows, :] = ks[:DFT_ROWS] * inv
            k_s[1, rows, :] = ks[DFT_ROWS:] * inv

    for i in range(nsub):
        rows = block_rows(i)
        xs = jnp.dot(mf_ref[...], stacked(b_ref, rows), preferred_element_type=f32)
        xr, xi = xs[:DFT_ROWS], xs[DFT_ROWS:]
        kr, ki = k_s[0, rows, :], k_s[1, rows, :]
        ys = jnp.concatenate([xr * kr - xi * ki, xr * ki + xi * kr], axis=0).astype(bf16)
        cs = jnp.dot(mi_ref[...], ys, preferred_element_type=f32)
        o_ref[0, rows, :] = cs[:DFT_ROWS].astype(bf16)
        o_ref[1, rows, :] = cs[DFT_ROWS:].astype(bf16)


def _fft2_call(bs, kb, asum, order, m_fwd, m_inv):
    B, _, n, C = bs.shape
    blocks = n // DFT_ROWS
    per_step = max(d for d in range(1, min(DFT_STAGE2_BLOCKS, blocks) + 1) if blocks % d == 0)
    rb = per_step * DFT_ROWS
    blk = pl.BlockSpec((None, 2, rb, C), lambda j, b: (b, 0, j, 0))
    return pl.pallas_call(
        functools.partial(_fft2_body, nsub=per_step),
        grid=(n // rb, B),
        in_specs=[blk,
                  pl.BlockSpec((2, rb, C), lambda j, b: (0, j, order)),
                  pl.BlockSpec((1, C), lambda j, b: (0, order)),
                  _const_spec((2 * DFT_ROWS, 2 * DFT_ROWS)),
                  _const_spec((2 * DFT_ROWS, 2 * DFT_ROWS))],
        out_specs=blk,
        out_shape=jax.ShapeDtypeStruct(bs.shape, bf16),
        scratch_shapes=[pltpu.VMEM((2, rb, C), f32)],
        compiler_params=_cparams(("arbitrary", "arbitrary")),
        name="fft2",
    )(bs, kb, asum, m_fwd, m_inv)


def _fft3_body(c_ref, gi_ref, gate_ref, prev_ref, sk_ref, *rest, rpb, fuse_next):
    if fuse_next:
        gf_ref, z_ref, b_ref = rest
    else:
        (z_ref,) = rest
    cs = c_ref.shape[-1]
    c = _n1_major(c_ref[...].astype(f32).reshape(2 * c_ref.shape[1], rpb, cs))
    gate = _n1_major(gate_ref[...])
    prev = _n1_major(prev_ref[...])
    sk = sk_ref[...]
    zs, bs = [], []
    for r in range(rpb):
        y = jnp.dot(gi_ref[r], c[r].astype(bf16), preferred_element_type=f32)
        z = gate[r] * (y + sk * prev[r])
        zs.append(z)
        if fuse_next:
            bs.append(jnp.dot(gf_ref[r], z.astype(bf16), preferred_element_type=f32))
    z_ref[...] = _n1_minor(zs, z_ref.shape)
    if fuse_next:
        b_ref[...] = _n1_minor(bs, b_ref.shape).astype(bf16)


def _fft3_call(cs5, g_inv, gate, prev, sk_row, g_fwd=None):
    B, _, kh, n1c, C = cs5.shape
    rpb = min(DFT_RPB, n1c)
    cs = DFT_SLAB
    tblk = pl.BlockSpec((None, DFT_ROWS, rpb, cs), lambda b, j, s: (b, 0, j, s))
    sblk = pl.BlockSpec((None, 2, kh, rpb, cs), lambda b, j, s: (b, 0, 0, j, s))
    in_specs = [sblk, pl.BlockSpec((rpb, DFT_ROWS, 2 * kh), lambda b, j, s: (j, 0, 0)), tblk, tblk,
                pl.BlockSpec((1, cs), lambda b, j, s: (0, s))]
    args = [cs5, g_inv, gate, prev, sk_row]
    out_specs = [tblk]
    out_shape = [jax.ShapeDtypeStruct(gate.shape, f32)]
    fuse_next = g_fwd is not None
    if fuse_next:
        in_specs.append(pl.BlockSpec((rpb, 2 * kh, DFT_ROWS), lambda b, j, s: (j, 0, 0)))
        args.append(g_fwd)
        out_specs.append(sblk)
        out_shape.append(jax.ShapeDtypeStruct(cs5.shape, bf16))
    return pl.pallas_call(
        functools.partial(_fft3_body, rpb=rpb, fuse_next=fuse_next),
        grid=(B, n1c // rpb, C // cs),
        in_specs=in_specs,
        out_specs=out_specs,
        out_shape=out_shape,
        compiler_params=_cparams(("parallel", "arbitrary", "arbitrary")),
        name="fft3_next" if fuse_next else "fft3",
    )(*args)


def _filter_spectrum(L, tabs, filt_params):
    bs, asum = _filter_call(L, tabs["g_fwd"], *filt_params)
    return bs.reshape(2, tabs["kh"] * tabs["n1"], bs.shape[-1]), asum


def _hyena_conv(v, x1, x2, ksp, skip, tabs):
    n1c, kh = tabs["n1"], tabs["kh"]
    B, L, C = v.shape
    n = kh * n1c
    kb, asum = ksp
    v4, x14, x24 = (a.reshape(B, DFT_ROWS, n1c, C) for a in (v, x1, x2))
    s5 = (B, 2, kh, n1c, C)
    bs = _fft1_call(v4, tabs["g_fwd"])
    cs = _fft2_call(bs.reshape(B, 2, n, C), kb, asum, 0, tabs["m_fwd"], tabs["m_inv"])
    z1, bs = _fft3_call(cs.reshape(s5), tabs["g_inv"], x14, v4, skip[0:1], tabs["g_fwd"])
    cs = _fft2_call(bs.reshape(B, 2, n, C), kb, asum, 1, tabs["m_fwd"], tabs["m_inv"])
    (z2,) = _fft3_call(cs.reshape(s5), tabs["g_inv"], x24, z1, skip[1:2])
    return z2.reshape(B, L, C)


def _route(r):
    lane = lax.broadcasted_iota(jnp.int32, r.shape, 1)
    ninf = jnp.float32(-jnp.inf)
    big = jnp.int32(1 << 20)
    is_g = lane < N_GROUPS
    gmax = jnp.max(jnp.where(is_g, r, ninf), axis=-1, keepdims=True)
    gidx = jnp.min(jnp.where(jnp.logical_and(is_g, r == gmax), lane, big), axis=-1, keepdims=True)
    gw = 1.0 / jnp.sum(jnp.where(is_g, jnp.exp(r - gmax), 0.0), axis=-1, keepdims=True)
    e_lane = lane - N_GROUPS
    sel = jnp.logical_and(jnp.logical_and(e_lane >= 0, e_lane < N_EXPERTS), (e_lane >> 2) == gidx)
    le = jnp.where(sel, r, ninf)
    m1 = jnp.max(le, axis=-1, keepdims=True)
    i1 = jnp.min(jnp.where(le == m1, lane, big), axis=-1, keepdims=True)
    le2 = jnp.where(lane == i1, ninf, le)
    m2 = jnp.max(le2, axis=-1, keepdims=True)
    i2 = jnp.min(jnp.where(le2 == m2, lane, big), axis=-1, keepdims=True)
    e2 = jnp.exp(m2 - m1)
    den = 1.0 + e2
    comb = jnp.where(lane == i1, gw / den, jnp.where(lane == i2, gw * e2 / den, 0.0))
    return comb, gidx


def _mix_body(x_ref, z_ref, g0_ref, rest_ref, ga1_ref, sh2_ref, sc2_ref, n2g_ref, wba_ref, wout_ref, wr_ref, tri_ref,
              xo_ref, hs_ref, combs_ref, pmt_ref, cnt_ref):
    ya = jnp.dot(z_ref[...].astype(bf16), wba_ref[...], preferred_element_type=f32)
    merged = g0_ref[...] * ya + rest_ref[...]
    xo = x_ref[...] + ga1_ref[...] * jnp.dot(merged.astype(bf16), wout_ref[...], preferred_element_type=f32)
    xo_ref[...] = xo
    ms = jnp.mean(xo * xo, axis=-1, keepdims=True)
    h2 = xo * lax.rsqrt(ms + EPS) * n2g_ref[...]
    h2 = h2 * (1.0 + sc2_ref[...]) + sh2_ref[...]
    h_hi = h2.astype(bf16)
    h_lo = (h2 - h_hi.astype(f32)).astype(bf16)
    p_hi = jnp.dot(h_hi, wr_ref[...], preferred_element_type=f32)
    p_lo = jnp.dot(h_lo, wr_ref[:, 0:LANES], preferred_element_type=f32)
    comb, gidx = _route(p_hi[:, 0:LANES] + p_hi[:, LANES:2 * LANES] + p_lo)

    tm = comb.shape[0]
    lane = lax.broadcasted_iota(jnp.int32, comb.shape, 1)
    onehot = (lane == gidx).astype(f32)
    cum = jnp.dot(tri_ref[...], onehot.astype(bf16), preferred_element_type=f32)
    tot8 = cum[tm - 8:tm, :]
    off8 = pltpu.roll(tot8, 1, 1) + pltpu.roll(tot8, 2, 1) + pltpu.roll(tot8, 3, 1)
    rank = jnp.sum(onehot * (off8[7:8, :] + cum - 1.0), axis=-1, keepdims=True)
    slot = lax.broadcasted_iota(jnp.int32, (tm, tm), 1).astype(f32)
    pmt = (slot == rank).astype(bf16)
    pmt_ref[...] = pmt
    both = jnp.concatenate([h_hi, comb.astype(bf16)], axis=1)
    srt = lax.dot_general(pmt, both, (((0,), (0,)), ((), ())), preferred_element_type=f32).astype(bf16)
    d = h2.shape[1]
    hs_ref[...] = srt[:, 0:d]
    combs_ref[...] = srt[:, d:d + LANES]
    cnt_ref[...] = tot8[7:8, :].astype(jnp.int32)


def _mix_call(x, z, g0, rest, ga1, sh2, sc2, n2g, wba, wout, wr, tri, tm):
    B, L, D = x.shape
    nt = L // tm
    row = lambda b, i: (b, i, 0)
    vspec = pl.BlockSpec((None, 1, D), lambda b, i: (b, 0, 0))
    return pl.pallas_call(
        _mix_body,
        grid=(B, nt),
        in_specs=[
            pl.BlockSpec((None, tm, D), row),
            pl.BlockSpec((None, tm, W_MIX), row),
            pl.BlockSpec((None, tm, D), row),
            pl.BlockSpec((None, tm, D), row),
            vspec, vspec, vspec,
            _const_spec((1, D)),
            _const_spec((W_MIX, D)),
            _const_spec((D, D)),
            _const_spec((D, 2 * LANES)),
            _const_spec((tm, tm)),
        ],
        out_specs=[pl.BlockSpec((None, tm, D), row), pl.BlockSpec((None, tm, D), row),
                   pl.BlockSpec((None, tm, LANES), row), pl.BlockSpec((None, tm, tm), row),
                   pl.BlockSpec((None, None, 1, LANES), lambda b, i: (b, i, 0, 0))],
        out_shape=[jax.ShapeDtypeStruct((B, L, D), f32), jax.ShapeDtypeStruct((B, L, D), bf16),
                   jax.ShapeDtypeStruct((B, L, LANES), bf16), jax.ShapeDtypeStruct((B, L, tm), bf16),
                   jax.ShapeDtypeStruct((B, nt, 1, LANES), jnp.int32)],
        compiler_params=_cparams(("parallel", "arbitrary")),
        name="mix",
    )(x, z, g0, rest, ga1, sh2, sc2, n2g, wba, wout, wr, tri)


def _experts_body(cnt_ref, hs_ref, combs_ref, pmt_ref, xo_ref, ga2_ref, ex_ref, w1_ref, w3_ref, w2_ref, fg_ref, o_ref,
                  acc_s, *, tm, final_norm):
    base = (pl.program_id(0) * pl.num_programs(1) + pl.program_id(1)) * N_GROUPS
    nchunk = tm // MOE_CHUNK
    acc_s[...] = jnp.zeros_like(acc_s)

    def group(g, lo):
        hi = lo + cnt_ref[base + g]

        def run(start, nrows):
            rows = pl.ds(pl.multiple_of(start, MOE_CHUNK), nrows)
            h = hs_ref[rows, :]
            a = jnp.dot(h, w1_ref[g], preferred_element_type=f32)
            u = jnp.dot(h, w3_ref[g], preferred_element_type=f32)
            cw = jnp.dot(combs_ref[rows, :], ex_ref[g], preferred_element_type=f32)
            silu = 0.5 * a * (jnp.tanh(0.5 * a) + 1.0)
            acc_s[rows, :] += jnp.dot((silu * u * cw).astype(bf16), w2_ref[g], preferred_element_type=f32)

        win = jnp.minimum(lo - (lo & (MOE_CHUNK - 1)), tm - MOE_WINDOW)

        @pl.when(hi > lo)
        def _():
            run(win, MOE_WINDOW)

        for c in range(nchunk):
            @pl.when(jnp.logical_and(c * MOE_CHUNK >= win + MOE_WINDOW, hi > c * MOE_CHUNK))
            def _():
                run(c * MOE_CHUNK, MOE_CHUNK)
        return hi

    lax.fori_loop(0, N_GROUPS, group, jnp.int32(0))

    y2 = jnp.dot(pmt_ref[...], acc_s[...].astype(bf16), preferred_element_type=f32)
    y = xo_ref[...] + ga2_ref[...] * y2
    if final_norm:
        ms = jnp.mean(y * y, axis=-1, keepdims=True)
        y = y * lax.rsqrt(ms + EPS) * fg_ref[...]
    o_ref[...] = y


def _experts_call(cnt, hs, combs, pmt, xo, ga2, ex, w1, w3, w2, fg, tm, final_norm):
    B, L, D = hs.shape
    blk = lambda b, i, cnt: (b, i, 0)
    return pl.pallas_call(
        functools.partial(_experts_body, tm=tm, final_norm=final_norm),
        grid_spec=pltpu.PrefetchScalarGridSpec(
            num_scalar_prefetch=1,
            grid=(B, L // tm),
            in_specs=[
                pl.BlockSpec((None, tm, D), blk),
                pl.BlockSpec((None, tm, LANES), blk),
                pl.BlockSpec((None, tm, tm), blk),
                pl.BlockSpec((None, tm, D), blk),
                pl.BlockSpec((None, 1, D), lambda b, i, cnt: (b, 0, 0)),
                _const_spec((N_GROUPS, LANES, GROUP_HID)),
                _const_spec((N_GROUPS, D, GROUP_HID)),
                _const_spec((N_GROUPS, D, GROUP_HID)),
                _const_spec((N_GROUPS, GROUP_HID, D)),
                _const_spec((1, D)),
            ],
            out_specs=pl.BlockSpec((None, tm, D), blk),
            scratch_shapes=[pltpu.VMEM((tm, D), f32)],
        ),
        out_shape=jax.ShapeDtypeStruct((B, L, D), f32),
        compiler_params=_cparams(("arbitrary", "arbitrary")),
        name="experts",
    )(cnt, hs, combs, pmt, xo, ga2, ex, w1, w3, w2, fg)


def _pad_to(a, shape):
    return jnp.pad(a, [(0, s - d) for d, s in zip(a.shape, shape)])


def _prep_layer(l, p):
    max_decay = math.log(HY_TARGET) / HY_FAST_DECAY
    min_decay = math.log(HY_TARGET) / HY_SLOW_DECAY
    deltas = jnp.abs(jnp.linspace(min_decay, max_decay, W_MIX, dtype=f32))
    router = jnp.concatenate([p["router_g"][l], p["router_e"][l]], axis=1)
    lanes = jnp.arange(LANES)[None, :, None]
    cols = jnp.arange(GROUP_HID)[None, None, :]
    grp = jnp.arange(N_GROUPS)[:, None, None]
    expand = (lanes == N_GROUPS + EXP_PER_GROUP * grp + cols // D_EXPERT).astype(bf16)
    router = _pad_to(router, (D_MODEL, LANES))
    router_hi = router.astype(bf16)
    router_lo = (router - router_hi.astype(f32)).astype(bf16)
    zh = jnp.zeros((HY_HID, HY_HID), f32)
    w1 = _pad_to(p["hy_w1"][l], (LANES, HY_HID))
    zw1 = jnp.zeros_like(w1)
    w1_pair = jnp.block([[w1, zw1], [zw1, w1]])
    w2_pair = jnp.block([[p["hy_w2"][l], zh], [zh, p["hy_w2"][l]]])
    pair = lambda a: jnp.concatenate([a, a])[None]
    by_group = lambda w: jnp.transpose(w.astype(bf16).reshape(D_MODEL, N_GROUPS, GROUP_HID), (1, 0, 2))
    cw = HY_ORDER * W_MIX
    wo = p["hy_w_out"][l]
    zwo = jnp.zeros((HY_HID, cw), f32)
    wo_f = jnp.concatenate([wo[:, :cw], zwo], axis=0).astype(bf16)
    wo_b = jnp.concatenate([zwo, wo[:, cw:]], axis=0).astype(bf16)
    return dict(
        norm1_g=p["norm1_g"][l][None], norm2_g=p["norm2_g"][l][None],
        w_in=p["w_in"][l].astype(bf16), hy_conv_w=p["hy_conv_w"][l], hy_skip=p["hy_skip"][l],
        pool_w=p["pool_w"][l].astype(bf16), pool_scale=p["pool_scale"][l][None], sc_conv_w=p["sc_conv_w"][l],
        w_br_a=p["w_br_a"][l].astype(bf16), w_br_b=p["w_br_b"][l].astype(bf16), w_br_c=p["w_br_c"][l].astype(bf16),
        w_out=p["w_out"][l].astype(bf16),
        router=jnp.concatenate([router_hi, router_lo], axis=1), expand=expand,
        moe_w1=by_group(p["moe_w1"][l]), moe_w3=by_group(p["moe_w3"][l]),
        moe_w2=p["moe_w2"][l].astype(bf16).reshape(N_GROUPS, GROUP_HID, D_MODEL),
        filt=(w1_pair, pair(p["hy_b1"][l]), w2_pair, pair(p["hy_b2"][l]), pair(p["hy_freq"][l]), wo_f, wo_b,
              pair(deltas)),
    )


def _tile(L, want):
    return want if L % want == 0 else L


def _encoder_layer(x, mod, lp, ksp, tabs, final_g, final_norm):
    B, L, D = x.shape
    sh1, sc1, ga1, sh2, sc2, ga2 = (m[:, None, :] for m in jnp.split(mod, 6, axis=-1))
    v, x1, x2, g0, rest = _proj_call(x, sh1, sc1, lp["norm1_g"], lp["w_in"], lp["hy_conv_w"], lp["pool_w"],
                                     lp["pool_scale"], lp["sc_conv_w"], lp["w_br_b"], lp["w_br_c"], _tile(L, 512))
    z = _hyena_conv(v, x1, x2, ksp, lp["hy_skip"], tabs)
    tm = _tile(L, MOE_TILE)
    tri = jnp.tri(tm, dtype=bf16)
    xo, hs, combs, pmt, cnt = _mix_call(x, z, g0, rest, ga1, sh2, sc2, lp["norm2_g"], lp["w_br_a"], lp["w_out"],
                                        lp["router"], tri, tm)
    return _experts_call(cnt[:, :, 0, :N_GROUPS].reshape(-1), hs, combs, pmt, xo, ga2, lp["expand"], lp["moe_w1"],
                         lp["moe_w3"], lp["moe_w2"], final_g, tm, final_norm)


def _forward(xs, cs, p, final_g):
    depth = p["w_in"].shape[0]
    nb = [c.shape[0] for c in cs]
    rows = -(-sum(nb) // 8) * 8
    c_all = _pad_to(jnp.concatenate(cs, axis=0), (rows, D_MODEL))
    lens = sorted({x.shape[1] for x in xs})
    tabs = {L: _dft_tables(L) for L in lens}
    fg = final_g[None]
    mods = _mod_call(c_all, p["ada_w"], p["ada_b"])
    for l in range(depth):
        lp = _prep_layer(l, p)
        mod = mods[l]
        ksp = {L: _filter_spectrum(L, tabs[L], lp["filt"]) for L in lens}
        off = 0
        out = []
        for x, n in zip(xs, nb):
            L = x.shape[1]
            out.append(_encoder_layer(x, mod[off:off + n], lp, ksp[L], tabs[L], fg, l == depth - 1))
            off += n
        xs = out
    return xs


def kernel(x_prompt, x_sample, c_prompt, c_sample, ada_w, ada_b, norm1_g, norm2_g, w_in, hy_conv_w, hy_skip, hy_w1, hy_b1, hy_w2, hy_b2, hy_w_out, hy_freq, pool_w, pool_scale, sc_conv_w, w_br_a, w_br_b, w_br_c, w_out, router_g, router_e, moe_w1, moe_w3, moe_w2, final_g):
    p = dict(ada_w=ada_w, ada_b=ada_b, norm1_g=norm1_g, norm2_g=norm2_g, w_in=w_in, hy_conv_w=hy_conv_w,
             hy_skip=hy_skip, hy_w1=hy_w1, hy_b1=hy_b1, hy_w2=hy_w2, hy_b2=hy_b2, hy_w_out=hy_w_out, hy_freq=hy_freq,
             pool_w=pool_w, pool_scale=pool_scale, sc_conv_w=sc_conv_w, w_br_a=w_br_a, w_br_b=w_br_b, w_br_c=w_br_c,
             w_out=w_out, router_g=router_g, router_e=router_e, moe_w1=moe_w1, moe_w3=moe_w3, moe_w2=moe_w2)
    y_prompt, y_sample = _forward([x_prompt, x_sample], [c_prompt, c_sample], p, final_g)
    return (y_prompt, y_sample)
```

```python
import functools
import math

import jax
import jax.numpy as jnp
from jax import lax
from jax.experimental import pallas as pl
from jax.experimental.pallas import tpu as pltpu

f32 = jnp.float32
bf16 = jnp.bfloat16
HIGHEST = lax.Precision.HIGHEST

D_MODEL = 1024
DEPTH = 2
W_MIX = 512
HY_ORDER = 2
HY_BANDS = 16
HY_HID = 64
HY_FAST_DECAY = 0.3
HY_SLOW_DECAY = 1.5
HY_TARGET = 1e-2
POOL_WINDOWS = (2, 4, 8, 16)
POOL_GROUP = W_MIX // len(POOL_WINDOWS)
COL_HY = 3 * W_MIX
COL_POOL = W_MIX
COL_SC = 3 * W_MIX
COL_GATE = 3 * D_MODEL
OFF_POOL = COL_HY
OFF_SC = COL_HY + COL_POOL
OFF_GATE = COL_HY + COL_POOL + COL_SC
PROJ_COLS = OFF_GATE + COL_GATE
N_GROUPS = 4
EXP_PER_GROUP = 4
N_EXPERTS = N_GROUPS * EXP_PER_GROUP
D_EXPERT = 256
GROUP_HID = EXP_PER_GROUP * D_EXPERT
EPS = 1e-6

HALO = 8
DFT_N2 = 256
DFT_ROWS = 128
DFT_RPB = 16
DFT_SLAB = 256
DFT_STAGE2_BLOCKS = 17
MOE_TILE = 512
MOE_CHUNK = 64
MOE_WINDOW = 192
MOE_SUB = 2
LANES = 128
VMEM_LIMIT = 56 * 1024 * 1024


def _cparams(sem):
    return pltpu.CompilerParams(dimension_semantics=sem, vmem_limit_bytes=VMEM_LIMIT)


def _const_spec(shape):
    nd = len(shape)
    return pl.BlockSpec(shape, lambda *_: (0,) * nd, pipeline_mode=pl.Buffered(1))


def _mod_body(c_ref, w_ref, b_ref, o_ref):
    c = c_ref[...]
    s = c * jax.nn.sigmoid(c)
    o_ref[...] = jnp.dot(s, w_ref[...], preferred_element_type=f32, precision=HIGHEST) + b_ref[...]


def _mod_call(c_all, ada_w, ada_b):
    rows = c_all.shape[0]
    depth = ada_w.shape[0]
    tn = 1536
    return pl.pallas_call(
        _mod_body,
        grid=(depth, 6 * D_MODEL // tn),
        in_specs=[pl.BlockSpec((rows, D_MODEL), lambda l, j: (0, 0)),
                  pl.BlockSpec((None, D_MODEL, tn), lambda l, j: (l, 0, j)),
                  pl.BlockSpec((None, 1, tn), lambda l, j: (l, 0, j))],
        out_specs=pl.BlockSpec((None, rows, tn), lambda l, j: (l, 0, j)),
        out_shape=jax.ShapeDtypeStruct((depth, rows, 6 * D_MODEL), f32),
        compiler_params=_cparams(("arbitrary", "arbitrary")),
        name="mod",
    )(c_all, ada_w, ada_b[:, None, :])


def _proj_body(xm_ref, xp_ref, xn_ref, sh_ref, sc_ref, g_ref, win_ref, hyw_ref, pw_ref, ps_ref, scw_ref,
               wbb_ref, wbc_ref, v_ref, x1_ref, x2_ref, g0_ref, rest_ref, *, tm, seq_len):
    i = pl.program_id(1)
    nt = pl.num_programs(1)
    rt = tm + 2 * HALO
    ctr = slice(HALO, HALO + tm)

    def modulated(x):
        ms = jnp.mean(x * x, axis=-1, keepdims=True)
        h = x * lax.rsqrt(ms + EPS) * g_ref[...]
        return h * (1.0 + sc_ref[...]) + sh_ref[...]

    hp = jnp.where(i > 0, modulated(xp_ref[...]), 0.0)
    hn = jnp.where(i < nt - 1, modulated(xn_ref[...]), 0.0)
    hc = modulated(xm_ref[...])
    hb = jnp.concatenate([hp, hc, hn], axis=0).astype(bf16)
    hcb = hc.astype(bf16)

    def down(a, s):
        return pltpu.roll(a, s, 0)

    def up(a, s):
        return pltpu.roll(a, rt - s, 0)

    u = jnp.dot(hb, win_ref[:, 0:COL_HY], preferred_element_type=f32)
    w = hyw_ref[...]
    uc = (down(u, 1) * w[0:1] + u * w[1:2] + up(u, 1) * w[2:3])[ctr]
    v_ref[...] = uc[:, 0:W_MIX]
    x1_ref[...] = uc[:, W_MIX:2 * W_MIX]
    x2_ref[...] = uc[:, 2 * W_MIX:3 * W_MIX]

    q = jnp.dot(hb, win_ref[:, OFF_POOL:OFF_POOL + COL_POOL], preferred_element_type=f32)
    s2 = q + down(q, 1)
    s4 = s2 + down(s2, 2)
    s8 = s4 + down(s4, 4)
    s16 = s8 + down(s8, 8)
    tpos = i * tm + lax.broadcasted_iota(jnp.int32, (tm, 1), 0)
    pooled = []
    for g, (win, ssum) in enumerate(zip(POOL_WINDOWS, (s2, s4, s8, s16))):
        lo = win // 2
        hi = win - 1 - lo
        lanes = slice(g * POOL_GROUP, (g + 1) * POOL_GROUP)
        ws = ssum[:, lanes]
        if hi > 0:
            ws = up(ws, hi)
        cnt = (jnp.minimum(tpos + hi + 1, seq_len) - jnp.maximum(tpos - lo, 0)).astype(f32)
        p = ws[ctr] / cnt - q[ctr, lanes]
        pooled.append(jnp.dot(p.astype(bf16), pw_ref[g], preferred_element_type=f32))
    yb_in = jnp.concatenate(pooled, axis=1) * ps_ref[...]
    yb = jnp.dot(yb_in.astype(bf16), wbb_ref[...], preferred_element_type=f32)

    us = jnp.dot(hb, win_ref[:, OFF_SC:OFF_SC + COL_SC], preferred_element_type=f32)
    cx = us[:, W_MIX:2 * W_MIX] * us[:, 2 * W_MIX:3 * W_MIX]
    sw = scw_ref[...]
    dw = down(cx, 1) * sw[0:1] + cx * sw[1:2] + up(cx, 1) * sw[2:3]
    sc_out = (us[:, 0:W_MIX] * dw)[ctr]
    yc = jnp.dot(sc_out.astype(bf16), wbc_ref[...], preferred_element_type=f32)

    gt = 0.5 * jnp.tanh(0.5 * jnp.dot(hcb, win_ref[:, OFF_GATE:PROJ_COLS], preferred_element_type=f32)) + 0.5
    g0_ref[...] = gt[:, 0:D_MODEL]
    rest_ref[...] = gt[:, D_MODEL:2 * D_MODEL] * yb + gt[:, 2 * D_MODEL:3 * D_MODEL] * yc


def _proj_call(x, sh, sc, g1, win, hyw, pw, ps, scw, wbb, wbc, tm):
    B, L, D = x.shape
    nt = L // tm
    hb = tm // HALO
    row = lambda b, i: (b, i, 0)
    vec = lambda b, i: (b, 0, 0)
    out_w = jax.ShapeDtypeStruct((B, L, W_MIX), f32)
    out_d = jax.ShapeDtypeStruct((B, L, D), f32)
    return pl.pallas_call(
        functools.partial(_proj_body, tm=tm, seq_len=L),
        grid=(B, nt),
        in_specs=[
            pl.BlockSpec((None, tm, D), row),
            pl.BlockSpec((None, HALO, D), lambda b, i: (b, jnp.maximum(i * hb - 1, 0), 0)),
            pl.BlockSpec((None, HALO, D), lambda b, i: (b, jnp.minimum((i + 1) * hb, L // HALO - 1), 0)),
            pl.BlockSpec((None, 1, D), vec),
            pl.BlockSpec((None, 1, D), vec),
            _const_spec((1, D)),
            _const_spec((D, PROJ_COLS)),
            _const_spec((3, COL_HY)),
            _const_spec((len(POOL_WINDOWS), POOL_GROUP, POOL_GROUP)),
            _const_spec((1, W_MIX)),
            _const_spec((3, W_MIX)),
            _const_spec((W_MIX, D)),
            _const_spec((W_MIX, D)),
        ],
        out_specs=[pl.BlockSpec((None, tm, W_MIX), row)] * 3 + [pl.BlockSpec((None, tm, D), row)] * 2,
        out_shape=[out_w, out_w, out_w, out_d, out_d],
        compiler_params=_cparams(("parallel", "arbitrary")),
        name="proj",
    )(x, x, x, sh, sc, g1, win, hyw, pw, ps, scw, wbb, wbc)


def _filter_body(ca_ref, sa_ref, cb_ref, sb_ref, w1_ref, b1_ref, w2_ref, b2_ref, fr_ref, wo0_ref, wo1_ref, dl_ref, g_ref,
                 o_ref, asum_ref, h_s, *, rpb, seq_len, n1c):
    j = pl.program_id(0)
    s = pl.program_id(1)
    L = seq_len
    n2 = lax.broadcasted_iota(jnp.int32, (DFT_N2, 1), 0)
    fwd = n2 < DFT_ROWS

    def slot(r):
        pos = (j * rpb + r) + n1c * n2
        return pos, jnp.where(fwd, pos, 2 * L - pos).astype(f32)

    @pl.when(s == 0)
    def _():
        lane = lax.broadcasted_iota(jnp.int32, (DFT_N2, LANES), 1)
        fr = fr_ref[...]
        cb, sb = cb_ref[...], sb_ref[...]
        zs = []
        for r in range(rpb):
            _, lag = slot(r)
            ca = ca_ref[r:r + 1, :]
            sa = jnp.where(fwd, sa_ref[r:r + 1, :], -sa_ref[r:r + 1, :])
            cos_t = ca * cb - sa * sb
            sin_t = sa * cb + ca * sb
            z = jnp.where(lane == 0, lag / (L - 1), jnp.where(lane <= HY_BANDS, cos_t, -sin_t))
            zs.append(jnp.concatenate([z[:DFT_ROWS], z[DFT_ROWS:]], axis=1))
        zz = jnp.concatenate(zs, axis=0)
        h = jnp.sin(fr * (jnp.dot(zz, w1_ref[...], preferred_element_type=f32, precision=HIGHEST) + b1_ref[...]))
        h_s[...] = jnp.sin(fr * (jnp.dot(h, w2_ref[...], preferred_element_type=f32, precision=HIGHEST) + b2_ref[...]))

    hb = h_s[...].astype(bf16)
    ho_f = jnp.dot(hb, wo0_ref[...], preferred_element_type=f32)
    ho_b = jnp.dot(hb, wo1_ref[...], preferred_element_type=f32)
    asum = jnp.zeros(asum_ref.shape[1:], f32)
    bs = []
    for r in range(rpb):
        pos, lag = slot(r)
        rows = slice(r * DFT_ROWS, (r + 1) * DFT_ROWS)
        ho = jnp.concatenate([ho_f[rows], ho_b[rows]], axis=0)
        k = jnp.where(pos == L, 0.0, ho * jnp.exp(-(lag / (L - 1)) * dl_ref[...]))
        asum = asum + jnp.sum(jnp.abs(k), axis=0, keepdims=True)
        bs.append(jnp.dot(g_ref[r], k.astype(bf16), preferred_element_type=f32))
    o_ref[...] = jnp.swapaxes(jnp.stack(bs, axis=0), 0, 1).reshape(o_ref.shape).astype(bf16)

    @pl.when(j == 0)
    def _():
        asum_ref[s] = asum

    @pl.when(j > 0)
    def _():
        asum_ref[s] += asum


def _filter_tables(L):
    n1c = 2 * L // DFT_N2
    bands = jnp.linspace(1e-4, HY_BANDS - 1, HY_BANDS, dtype=f32)
    brow = jnp.zeros((LANES,), f32).at[1:1 + HY_BANDS].set(bands).at[1 + HY_BANDS:1 + 2 * HY_BANDS].set(bands)
    used = (jnp.arange(LANES) >= 1) & (jnp.arange(LANES) <= 2 * HY_BANDS)
    n2 = jnp.arange(DFT_N2)
    part_a = jnp.arange(n1c).astype(f32)
    part_b = (n1c * jnp.where(n2 < DFT_ROWS, n2, DFT_N2 - n2)).astype(f32)

    def cs(part):
        ang = (2 * math.pi / L) * part[:, None] * brow[None, :]
        return jnp.where(used, jnp.cos(ang), 0.0), jnp.where(used, jnp.sin(ang), 0.0)

    return cs(part_a) + cs(part_b)


def _filter_call(L, g_fwd, w1p, b1p, w2p, b2p, frp, wo_f, wo_b, dl_row):
    n1c = 2 * L // DFT_N2
    rpb = min(DFT_RPB, n1c)
    cw = HY_ORDER * W_MIX
    cs = DFT_SLAB
    ns = cw // cs
    ca, sa, cb, sb = _filter_tables(L)
    kh = g_fwd.shape[1] // 2
    bs, asum = pl.pallas_call(
        functools.partial(_filter_body, rpb=rpb, seq_len=L, n1c=n1c),
        grid=(n1c // rpb, ns),
        in_specs=[pl.BlockSpec((rpb, LANES), lambda j, s: (j, 0)), pl.BlockSpec((rpb, LANES), lambda j, s: (j, 0)),
                  _const_spec((DFT_N2, LANES)), _const_spec((DFT_N2, LANES)),
                  _const_spec((2 * LANES, LANES)), _const_spec((1, LANES)),
                  _const_spec((LANES, LANES)), _const_spec((1, LANES)), _const_spec((1, LANES)),
                  pl.BlockSpec((LANES, cs), lambda j, s: (0, s)),
                  pl.BlockSpec((LANES, cs), lambda j, s: (0, s)),
                  pl.BlockSpec((1, cs), lambda j, s: (0, s)),
                  pl.BlockSpec((rpb, 2 * kh, DFT_N2), lambda j, s: (j, 0, 0))],
        out_specs=[pl.BlockSpec((2, kh, rpb, cs), lambda j, s: (0, 0, j, s)),
                   pl.BlockSpec((ns, 1, cs), lambda j, s: (0, 0, 0))],
        out_shape=[jax.ShapeDtypeStruct((2, kh, n1c, cw), bf16), jax.ShapeDtypeStruct((ns, 1, cs), f32)],
        scratch_shapes=[pltpu.VMEM((rpb * DFT_ROWS, LANES), f32)],
        compiler_params=_cparams(("arbitrary", "arbitrary")),
        name="filt",
    )(ca, sa, cb, sb, w1p, b1p, w2p, b2p, frp, wo_f, wo_b, dl_row, g_fwd)
    return bs, asum.reshape(1, cw)


def _dft_half_rows(n1c):
    step = max(8, DFT_ROWS // n1c)
    return -(-(DFT_ROWS + 1) // step) * step


def _dft_tables(L):
    n = 2 * L
    n1c = n // DFT_N2
    kh = _dft_half_rows(n1c)
    k2 = jnp.arange(DFT_N2, dtype=jnp.int32)
    tw_ang = ((jnp.arange(n1c, dtype=jnp.int32)[:, None] * k2[None, :]) % n).astype(f32) * (2 * math.pi / n)
    f_ang = ((k2[:, None] * k2[None, :]) % DFT_N2).astype(f32) * (2 * math.pi / DFT_N2)
    twr, twi = jnp.cos(tw_ang)[:, :kh, None], -jnp.sin(tw_ang)[:, :kh, None]
    fr, fi = jnp.cos(f_ang)[None, :kh], -jnp.sin(f_ang)[None, :kh]
    gr = twr * fr - twi * fi
    gi = twr * fi + twi * fr
    g_fwd = jnp.concatenate([gr, gi], axis=1)
    kk = jnp.arange(kh)
    wgt = jnp.where(kk > DFT_ROWS, 0.0, jnp.where((kk == 0) | (kk == DFT_ROWS), 1.0, 2.0)) * (1.0 / n)
    twr_t, twi_t = jnp.cos(tw_ang)[:, None, :kh] * wgt, -jnp.sin(tw_ang)[:, None, :kh] * wgt
    fr_t = jnp.cos(f_ang)[None, :DFT_ROWS, :kh]
    fi_t = -jnp.sin(f_ang)[None, :DFT_ROWS, :kh]
    g_inv = jnp.concatenate([twr_t * fr_t - twi_t * fi_t, twr_t * fi_t + twi_t * fr_t], axis=2)
    a = jnp.arange(n1c, dtype=jnp.int32)
    s_ang = ((a[:, None] * a[None, :]) % n1c).astype(f32) * (2 * math.pi / n1c)
    eye = jnp.eye(DFT_ROWS // n1c, dtype=f32)
    sr = jnp.kron(eye, jnp.cos(s_ang))
    si = jnp.kron(eye, -jnp.sin(s_ang))
    m_fwd = jnp.block([[sr, -si], [si, sr]])
    m_inv = jnp.block([[sr, si], [-si, sr]])
    return dict(g_fwd=g_fwd.astype(bf16), g_inv=g_inv.astype(bf16), m_fwd=m_fwd.astype(bf16),
                m_inv=m_inv.astype(bf16), n1=n1c, kh=kh)


def _n1_major(a):
    return jnp.swapaxes(a, 0, 1)


def _n1_minor(mats, shape):
    return jnp.swapaxes(jnp.stack(mats, axis=0), 0, 1).reshape(shape)


def _fft1_body(x_ref, g_ref, o_ref, *, rpb):
    x = _n1_major(x_ref[...])
    bs = [jnp.dot(g_ref[r], x[r].astype(bf16), preferred_element_type=f32) for r in range(rpb)]
    o_ref[...] = _n1_minor(bs, o_ref.shape).astype(bf16)


def _fft1_call(x4, g_fwd):
    B, _, n1c, C = x4.shape
    rpb = min(DFT_RPB, n1c)
    cs = DFT_SLAB
    kh = g_fwd.shape[1] // 2
    return pl.pallas_call(
        functools.partial(_fft1_body, rpb=rpb),
        grid=(B, n1c // rpb, C // cs),
        in_specs=[pl.BlockSpec((None, DFT_ROWS, rpb, cs), lambda b, j, s: (b, 0, j, s)),
                  pl.BlockSpec((rpb, 2 * kh, DFT_ROWS), lambda b, j, s: (j, 0, 0))],
        out_specs=pl.BlockSpec((None, 2, kh, rpb, cs), lambda b, j, s: (b, 0, 0, j, s)),
        out_shape=jax.ShapeDtypeStruct((B, 2, kh, n1c, C), bf16),
        compiler_params=_cparams(("parallel", "arbitrary", "arbitrary")),
        name="fft1",
    )(x4, g_fwd)


def _fft2_body(b_ref, kb_ref, asum_ref, mf_ref, mi_ref, o_ref, k_s, *, nsub):
    def block_rows(i):
        return slice(i * DFT_ROWS, (i + 1) * DFT_ROWS)

    def stacked(ref, rows):
        return jnp.concatenate([ref[0, rows, :], ref[1, rows, :]], axis=0)

    @pl.when(pl.program_id(1) == 0)
    def _():
        inv = 1.0 / asum_ref[...]
        for i in range(nsub):
            rows = block_rows(i)
            ks = jnp.dot(mf_ref[...], stacked(kb_ref, rows), preferred_element_type=f32)
            k_s[0, rows, :] = ks[:DFT_ROWS] * inv
            k_s[1, rows, :] = ks[DFT_ROWS:] * inv

    for i in range(nsub):
        rows = block_rows(i)
        xs = jnp.dot(mf_ref[...], stacked(b_ref, rows), preferred_element_type=f32)
        xr, xi = xs[:DFT_ROWS], xs[DFT_ROWS:]
        kr, ki = k_s[0, rows, :], k_s[1, rows, :]
        ys = jnp.concatenate([xr * kr - xi * ki, xr * ki + xi * kr], axis=0).astype(bf16)
        cs = jnp.dot(mi_ref[...], ys, preferred_element_type=f32)
        o_ref[0, rows, :] = cs[:DFT_ROWS].astype(bf16)
        o_ref[1, rows, :] = cs[DFT_ROWS:].astype(bf16)


def _fft2_call(bs, kb, asum, order, m_fwd, m_inv):
    B, _, n, C = bs.shape
    blocks = n // DFT_ROWS
    per_step = max(d for d in range(1, min(DFT_STAGE2_BLOCKS, blocks) + 1) if blocks % d == 0)
    rb = per_step * DFT_ROWS
    blk = pl.BlockSpec((None, 2, rb, C), lambda j, b: (b, 0, j, 0))
    return pl.pallas_call(
        functools.partial(_fft2_body, nsub=per_step),
        grid=(n // rb, B),
        in_specs=[blk,
                  pl.BlockSpec((2, rb, C), lambda j, b: (0, j, order)),
                  pl.BlockSpec((1, C), lambda j, b: (0, order)),
                  _const_spec((2 * DFT_ROWS, 2 * DFT_ROWS)),
                  _const_spec((2 * DFT_ROWS, 2 * DFT_ROWS))],
        out_specs=blk,
        out_shape=jax.ShapeDtypeStruct(bs.shape, bf16),
        scratch_shapes=[pltpu.VMEM((2, rb, C), f32)],
        compiler_params=_cparams(("arbitrary", "arbitrary")),
        name="fft2",
    )(bs, kb, asum, m_fwd, m_inv)


def _fft3_body(c_ref, gi_ref, gate_ref, prev_ref, sk_ref, *rest, rpb, fuse_next):
    if fuse_next:
        gf_ref, z_ref, b_ref = rest
    else:
        (z_ref,) = rest
    cs = c_ref.shape[-1]
    c = _n1_major(c_ref[...].astype(f32).reshape(2 * c_ref.shape[1], rpb, cs))
    gate = _n1_major(gate_ref[...])
    prev = _n1_major(prev_ref[...])
    sk = sk_ref[...]
    zs, bs = [], []
    for r in range(rpb):
        y = jnp.dot(gi_ref[r], c[r].astype(bf16), preferred_element_type=f32)
        z = gate[r] * (y + sk * prev[r])
        zs.append(z)
        if fuse_next:
            bs.append(jnp.dot(gf_ref[r], z.astype(bf16), preferred_element_type=f32))
    z_ref[...] = _n1_minor(zs, z_ref.shape)
    if fuse_next:
        b_ref[...] = _n1_minor(bs, b_ref.shape).astype(bf16)


def _fft3_call(cs5, g_inv, gate, prev, sk_row, g_fwd=None):
    B, _, kh, n1c, C = cs5.shape
    rpb = min(DFT_RPB, n1c)
    cs = DFT_SLAB
    tblk = pl.BlockSpec((None, DFT_ROWS, rpb, cs), lambda b, j, s: (b, 0, j, s))
    sblk = pl.BlockSpec((None, 2, kh, rpb, cs), lambda b, j, s: (b, 0, 0, j, s))
    in_specs = [sblk, pl.BlockSpec((rpb, DFT_ROWS, 2 * kh), lambda b, j, s: (j, 0, 0)), tblk, tblk,
                pl.BlockSpec((1, cs), lambda b, j, s: (0, s))]
    args = [cs5, g_inv, gate, prev, sk_row]
    out_specs = [tblk]
    out_shape = [jax.ShapeDtypeStruct(gate.shape, f32)]
    fuse_next = g_fwd is not None
    if fuse_next:
        in_specs.append(pl.BlockSpec((rpb, 2 * kh, DFT_ROWS), lambda b, j, s: (j, 0, 0)))
        args.append(g_fwd)
        out_specs.append(sblk)
        out_shape.append(jax.ShapeDtypeStruct(cs5.shape, bf16))
    return pl.pallas_call(
        functools.partial(_fft3_body, rpb=rpb, fuse_next=fuse_next),
        grid=(B, n1c // rpb, C // cs),
        in_specs=in_specs,
        out_specs=out_specs,
        out_shape=out_shape,
        compiler_params=_cparams(("parallel", "arbitrary", "arbitrary")),
        name="fft3_next" if fuse_next else "fft3",
    )(*args)


def _filter_spectrum(L, tabs, filt_params):
    bs, asum = _filter_call(L, tabs["g_fwd"], *filt_params)
    return bs.reshape(2, tabs["kh"] * tabs["n1"], bs.shape[-1]), asum


def _hyena_conv(v, x1, x2, ksp, skip, tabs):
    n1c, kh = tabs["n1"], tabs["kh"]
    B, L, C = v.shape
    n = kh * n1c
    kb, asum = ksp
    v4, x14, x24 = (a.reshape(B, DFT_ROWS, n1c, C) for a in (v, x1, x2))
    s5 = (B, 2, kh, n1c, C)
    bs = _fft1_call(v4, tabs["g_fwd"])
    cs = _fft2_call(bs.reshape(B, 2, n, C), kb, asum, 0, tabs["m_fwd"], tabs["m_inv"])
    z1, bs = _fft3_call(cs.reshape(s5), tabs["g_inv"], x14, v4, skip[0:1], tabs["g_fwd"])
    cs = _fft2_call(bs.reshape(B, 2, n, C), kb, asum, 1, tabs["m_fwd"], tabs["m_inv"])
    (z2,) = _fft3_call(cs.reshape(s5), tabs["g_inv"], x24, z1, skip[1:2])
    return z2.reshape(B, L, C)


def _route(r):
    lane = lax.broadcasted_iota(jnp.int32, r.shape, 1)
    ninf = jnp.float32(-jnp.inf)
    big = jnp.int32(1 << 20)
    is_g = lane < N_GROUPS
    gmax = jnp.max(jnp.where(is_g, r, ninf), axis=-1, keepdims=True)
    gidx = jnp.min(jnp.where(jnp.logical_and(is_g, r == gmax), lane, big), axis=-1, keepdims=True)
    gw = 1.0 / jnp.sum(jnp.where(is_g, jnp.exp(r - gmax), 0.0), axis=-1, keepdims=True)
    e_lane = lane - N_GROUPS
    sel = jnp.logical_and(jnp.logical_and(e_lane >= 0, e_lane < N_EXPERTS), (e_lane >> 2) == gidx)
    le = jnp.where(sel, r, ninf)
    m1 = jnp.max(le, axis=-1, keepdims=True)
    i1 = jnp.min(jnp.where(le == m1, lane, big), axis=-1, keepdims=True)
    le2 = jnp.where(lane == i1, ninf, le)
    m2 = jnp.max(le2, axis=-1, keepdims=True)
    i2 = jnp.min(jnp.where(le2 == m2, lane, big), axis=-1, keepdims=True)
    e2 = jnp.exp(m2 - m1)
    den = 1.0 + e2
    comb = jnp.where(lane == i1, gw / den, jnp.where(lane == i2, gw * e2 / den, 0.0))
    return comb, gidx


def _mix_body(x_ref, z_ref, g0_ref, rest_ref, ga1_ref, sh2_ref, sc2_ref, n2g_ref, wba_ref, wout_ref, wr_ref, tri_ref,
              xo_ref, hs_ref, combs_ref, pmt_ref, cnt_ref):
    ya = jnp.dot(z_ref[...].astype(bf16), wba_ref[...], preferred_element_type=f32)
    merged = g0_ref[...] * ya + rest_ref[...]
    xo = x_ref[...] + ga1_ref[...] * jnp.dot(merged.astype(bf16), wout_ref[...], preferred_element_type=f32)
    xo_ref[...] = xo
    ms = jnp.mean(xo * xo, axis=-1, keepdims=True)
    h2 = xo * lax.rsqrt(ms + EPS) * n2g_ref[...]
    h2 = h2 * (1.0 + sc2_ref[...]) + sh2_ref[...]
    h_hi = h2.astype(bf16)
    h_lo = (h2 - h_hi.astype(f32)).astype(bf16)
    p_hi = jnp.dot(h_hi, wr_ref[...], preferred_element_type=f32)
    p_lo = jnp.dot(h_lo, wr_ref[:, 0:LANES], preferred_element_type=f32)
    comb, gidx = _route(p_hi[:, 0:LANES] + p_hi[:, LANES:2 * LANES] + p_lo)

    tm = comb.shape[0]
    lane = lax.broadcasted_iota(jnp.int32, comb.shape, 1)
    onehot = (lane == gidx).astype(f32)
    cum = jnp.dot(tri_ref[...], onehot.astype(bf16), preferred_element_type=f32)
    tot8 = cum[tm - 8:tm, :]
    off8 = pltpu.roll(tot8, 1, 1) + pltpu.roll(tot8, 2, 1) + pltpu.roll(tot8, 3, 1)
    rank = jnp.sum(onehot * (off8[7:8, :] + cum - 1.0), axis=-1, keepdims=True)
    slot = lax.broadcasted_iota(jnp.int32, (tm, tm), 1).astype(f32)
    pmt = (slot == rank).astype(bf16)
    pmt_ref[...] = pmt
    both = jnp.concatenate([h_hi, comb.astype(bf16)], axis=1)
    srt = lax.dot_general(pmt, both, (((0,), (0,)), ((), ())), preferred_element_type=f32).astype(bf16)
    d = h2.shape[1]
    hs_ref[...] = srt[:, 0:d]
    combs_ref[...] = srt[:, d:d + LANES]
    cnt_ref[...] = tot8[7:8, :].astype(jnp.int32)


def _mix_call(x, z, g0, rest, ga1, sh2, sc2, n2g, wba, wout, wr, tri, tm):
    B, L, D = x.shape
    nt = L // tm
    row = lambda b, i: (b, i, 0)
    vspec = pl.BlockSpec((None, 1, D), lambda b, i: (b, 0, 0))
    return pl.pallas_call(
        _mix_body,
        grid=(B, nt),
        in_specs=[
            pl.BlockSpec((None, tm, D), row),
            pl.BlockSpec((None, tm, W_MIX), row),
            pl.BlockSpec((None, tm, D), row),
            pl.BlockSpec((None, tm, D), row),
            vspec, vspec, vspec,
            _const_spec((1, D)),
            _const_spec((W_MIX, D)),
            _const_spec((D, D)),
            _const_spec((D, 2 * LANES)),
            _const_spec((tm, tm)),
        ],
        out_specs=[pl.BlockSpec((None, tm, D), row), pl.BlockSpec((None, tm, D), row),
                   pl.BlockSpec((None, tm, LANES), row), pl.BlockSpec((None, tm, tm), row),
                   pl.BlockSpec((None, None, 1, LANES), lambda b, i: (b, i, 0, 0))],
        out_shape=[jax.ShapeDtypeStruct((B, L, D), f32), jax.ShapeDtypeStruct((B, L, D), bf16),
                   jax.ShapeDtypeStruct((B, L, LANES), bf16), jax.ShapeDtypeStruct((B, L, tm), bf16),
                   jax.ShapeDtypeStruct((B, nt, 1, LANES), jnp.int32)],
        compiler_params=_cparams(("parallel", "arbitrary")),
        name="mix",
    )(x, z, g0, rest, ga1, sh2, sc2, n2g, wba, wout, wr, tri)


def _experts_body(cnt_ref, hs_ref, combs_ref, pmt_ref, xo_ref, ga2_ref, ex_ref, w1_ref, w3_ref, w2_ref, fg_ref, o_ref,
                  acc_s, *, tm, sub, final_norm):
    b, sup, g = pl.program_id(0), pl.program_id(1), pl.program_id(2)
    nchunk = tm // MOE_CHUNK

    @pl.when(g == 0)
    def _():
        acc_s[...] = jnp.zeros_like(acc_s)

    def tile(t, carry):
        base = ((b * pl.num_programs(1) + sup) * sub + t) * N_GROUPS
        lo = jnp.int32(0)
        for gg in range(N_GROUPS - 1):
            lo = lo + jnp.where(gg < g, cnt_ref[base + gg], 0)
        hi = lo + cnt_ref[base + g]

        def run(start, nrows):
            rows = pl.ds(pl.multiple_of(t * tm + start, MOE_CHUNK), nrows)
            h = hs_ref[rows, :]
            a = jnp.dot(h, w1_ref[...], preferred_element_type=f32)
            u = jnp.dot(h, w3_ref[...], preferred_element_type=f32)
            cw = jnp.dot(combs_ref[rows, :], ex_ref[...], preferred_element_type=f32)
            silu = 0.5 * a * (jnp.tanh(0.5 * a) + 1.0)
            acc_s[rows, :] += jnp.dot((silu * u * cw).astype(bf16), w2_ref[...], preferred_element_type=f32)

        win = jnp.minimum(lo - (lo & (MOE_CHUNK - 1)), tm - MOE_WINDOW)

        @pl.when(hi > lo)
        def _():
            run(win, MOE_WINDOW)

        for c in range(nchunk):
            @pl.when(jnp.logical_and(c * MOE_CHUNK >= win + MOE_WINDOW, hi > c * MOE_CHUNK))
            def _():
                run(c * MOE_CHUNK, MOE_CHUNK)
        return carry

    lax.fori_loop(0, sub, tile, 0)

    @pl.when(g == pl.num_programs(2) - 1)
    def _():
        for t in range(sub):
            rows = slice(t * tm, (t + 1) * tm)
            y2 = jnp.dot(pmt_ref[rows, :], acc_s[rows, :].astype(bf16), preferred_element_type=f32)
            y = xo_ref[rows, :] + ga2_ref[...] * y2
            if final_norm:
                ms = jnp.mean(y * y, axis=-1, keepdims=True)
                y = y * lax.rsqrt(ms + EPS) * fg_ref[...]
            o_ref[rows, :] = y


def _experts_call(cnt, hs, combs, pmt, xo, ga2, ex, w1, w3, w2, fg, tm, final_norm):
    B, L, D = hs.shape
    sub = min(MOE_SUB, L // tm)
    rows = sub * tm
    blk = lambda b, s, g, cnt: (b, s, 0)
    return pl.pallas_call(
        functools.partial(_experts_body, tm=tm, sub=sub, final_norm=final_norm),
        grid_spec=pltpu.PrefetchScalarGridSpec(
            num_scalar_prefetch=1,
            grid=(B, L // rows, N_GROUPS),
            in_specs=[
                pl.BlockSpec((None, rows, D), blk),
                pl.BlockSpec((None, rows, LANES), blk),
                pl.BlockSpec((None, rows, tm), blk),
                pl.BlockSpec((None, rows, D), blk),
                pl.BlockSpec((None, 1, D), lambda b, s, g, cnt: (b, 0, 0)),
                pl.BlockSpec((None, LANES, GROUP_HID), lambda b, s, g, cnt: (g, 0, 0)),
                pl.BlockSpec((D, GROUP_HID), lambda b, s, g, cnt: (0, g)),
                pl.BlockSpec((D, GROUP_HID), lambda b, s, g, cnt: (0, g)),
                pl.BlockSpec((GROUP_HID, D), lambda b, s, g, cnt: (g, 0)),
                pl.BlockSpec((1, D), lambda b, s, g, cnt: (0, 0)),
            ],
            out_specs=pl.BlockSpec((None, rows, D), blk),
            scratch_shapes=[pltpu.VMEM((rows, D), f32)],
        ),
        out_shape=jax.ShapeDtypeStruct((B, L, D), f32),
        compiler_params=_cparams(("arbitrary", "arbitrary", "arbitrary")),
        name="experts",
    )(cnt, hs, combs, pmt, xo, ga2, ex, w1, w3, w2, fg)


def _pad_to(a, shape):
    return jnp.pad(a, [(0, s - d) for d, s in zip(a.shape, shape)])


def _prep_layer(l, p):
    max_decay = math.log(HY_TARGET) / HY_FAST_DECAY
    min_decay = math.log(HY_TARGET) / HY_SLOW_DECAY
    deltas = jnp.abs(jnp.linspace(min_decay, max_decay, W_MIX, dtype=f32))
    router = jnp.concatenate([p["router_g"][l], p["router_e"][l]], axis=1)
    lanes = jnp.arange(LANES)[None, :, None]
    cols = jnp.arange(GROUP_HID)[None, None, :]
    grp = jnp.arange(N_GROUPS)[:, None, None]
    expand = (lanes == N_GROUPS + EXP_PER_GROUP * grp + cols // D_EXPERT).astype(bf16)
    router = _pad_to(router, (D_MODEL, LANES))
    router_hi = router.astype(bf16)
    router_lo = (router - router_hi.astype(f32)).astype(bf16)
    zh = jnp.zeros((HY_HID, HY_HID), f32)
    w1 = _pad_to(p["hy_w1"][l], (LANES, HY_HID))
    zw1 = jnp.zeros_like(w1)
    w1_pair = jnp.block([[w1, zw1], [zw1, w1]])
    w2_pair = jnp.block([[p["hy_w2"][l], zh], [zh, p["hy_w2"][l]]])
    pair = lambda a: jnp.concatenate([a, a])[None]
    cw = HY_ORDER * W_MIX
    wo = p["hy_w_out"][l]
    zwo = jnp.zeros((HY_HID, cw), f32)
    wo_f = jnp.concatenate([wo[:, :cw], zwo], axis=0).astype(bf16)
    wo_b = jnp.concatenate([zwo, wo[:, cw:]], axis=0).astype(bf16)
    return dict(
        norm1_g=p["norm1_g"][l][None], norm2_g=p["norm2_g"][l][None],
        w_in=p["w_in"][l].astype(bf16), hy_conv_w=p["hy_conv_w"][l], hy_skip=p["hy_skip"][l],
        pool_w=p["pool_w"][l].astype(bf16), pool_scale=p["pool_scale"][l][None], sc_conv_w=p["sc_conv_w"][l],
        w_br_a=p["w_br_a"][l].astype(bf16), w_br_b=p["w_br_b"][l].astype(bf16), w_br_c=p["w_br_c"][l].astype(bf16),
        w_out=p["w_out"][l].astype(bf16),
        router=jnp.concatenate([router_hi, router_lo], axis=1), expand=expand,
        moe_w1=p["moe_w1"][l].astype(bf16), moe_w3=p["moe_w3"][l].astype(bf16), moe_w2=p["moe_w2"][l].astype(bf16),
        filt=(w1_pair, pair(p["hy_b1"][l]), w2_pair, pair(p["hy_b2"][l]), pair(p["hy_freq"][l]), wo_f, wo_b,
              pair(deltas)),
    )


def _tile(L, want):
    return want if L % want == 0 else L


def _encoder_layer(x, mod, lp, ksp, tabs, final_g, final_norm):
    B, L, D = x.shape
    sh1, sc1, ga1, sh2, sc2, ga2 = (m[:, None, :] for m in jnp.split(mod, 6, axis=-1))
    v, x1, x2, g0, rest = _proj_call(x, sh1, sc1, lp["norm1_g"], lp["w_in"], lp["hy_conv_w"], lp["pool_w"],
                                     lp["pool_scale"], lp["sc_conv_w"], lp["w_br_b"], lp["w_br_c"], _tile(L, 512))
    z = _hyena_conv(v, x1, x2, ksp, lp["hy_skip"], tabs)
    tm = _tile(L, MOE_TILE)
    tri = jnp.tri(tm, dtype=bf16)
    xo, hs, combs, pmt, cnt = _mix_call(x, z, g0, rest, ga1, sh2, sc2, lp["norm2_g"], lp["w_br_a"], lp["w_out"],
                                        lp["router"], tri, tm)
    return _experts_call(cnt[:, :, 0, :N_GROUPS].reshape(-1), hs, combs, pmt, xo, ga2, lp["expand"], lp["moe_w1"],
                         lp["moe_w3"], lp["moe_w2"], final_g, tm, final_norm)


def _forward(xs, cs, p, final_g):
    depth = p["w_in"].shape[0]
    nb = [c.shape[0] for c in cs]
    rows = -(-sum(nb) // 8) * 8
    c_all = _pad_to(jnp.concatenate(cs, axis=0), (rows, D_MODEL))
    lens = sorted({x.shape[1] for x in xs})
    tabs = {L: _dft_tables(L) for L in lens}
    fg = final_g[None]
    mods = _mod_call(c_all, p["ada_w"], p["ada_b"])
    for l in range(depth):
        lp = _prep_layer(l, p)
        mod = mods[l]
        ksp = {L: _filter_spectrum(L, tabs[L], lp["filt"]) for L in lens}
        off = 0
        out = []
        for x, n in zip(xs, nb):
            L = x.shape[1]
            out.append(_encoder_layer(x, mod[off:off + n], lp, ksp[L], tabs[L], fg, l == depth - 1))
            off += n
        xs = out
    return xs


def kernel(x_prompt, x_sample, c_prompt, c_sample, ada_w, ada_b, norm1_g, norm2_g, w_in, hy_conv_w, hy_skip, hy_w1, hy_b1, hy_w2, hy_b2, hy_w_out, hy_freq, pool_w, pool_scale, sc_conv_w, w_br_a, w_br_b, w_br_c, w_out, router_g, router_e, moe_w1, moe_w3, moe_w2, final_g):
    p = dict(ada_w=ada_w, ada_b=ada_b, norm1_g=norm1_g, norm2_g=norm2_g, w_in=w_in, hy_conv_w=hy_conv_w,
             hy_skip=hy_skip, hy_w1=hy_w1, hy_b1=hy_b1, hy_w2=hy_w2, hy_b2=hy_b2, hy_w_out=hy_w_out, hy_freq=hy_freq,
             pool_w=pool_w, pool_scale=pool_scale, sc_conv_w=sc_conv_w, w_br_a=w_br_a, w_br_b=w_br_b, w_br_c=w_br_c,
             w_out=w_out, router_g=router_g, router_e=router_e, moe_w1=moe_w1, moe_w3=moe_w3, moe_w2=moe_w2)
    y_prompt, y_sample = _forward([x_prompt, x_sample], [c_prompt, c_sample], p, final_g)
    return (y_prompt, y_sample)
```

```python
import functools
import math

import jax
import jax.numpy as jnp
from jax import lax
from jax.experimental import pallas as pl
from jax.experimental.pallas import tpu as pltpu

f32 = jnp.float32
bf16 = jnp.bfloat16
HIGHEST = lax.Precision.HIGHEST

D_MODEL = 1024
W_MIX = 512
HY_ORDER = 2
HY_BANDS = 16
HY_HID = 64
HY_FAST_DECAY = 0.3
HY_SLOW_DECAY = 1.5
HY_TARGET = 1e-2
POOL_WINDOWS = (2, 4, 8, 16)
POOL_GROUP = W_MIX // len(POOL_WINDOWS)
COL_HY = 3 * W_MIX
COL_POOL = W_MIX
COL_SC = 3 * W_MIX
COL_GATE = 3 * D_MODEL
OFF_POOL = COL_HY
OFF_SC = COL_HY + COL_POOL
OFF_GATE = COL_HY + COL_POOL + COL_SC
PROJ_COLS = OFF_GATE + COL_GATE
N_GROUPS = 4
EXP_PER_GROUP = 4
N_EXPERTS = N_GROUPS * EXP_PER_GROUP
D_EXPERT = 256
GROUP_HID = EXP_PER_GROUP * D_EXPERT
EPS = 1e-6

PROJ_TILE = 512
MOD_COLS = 1536
HALO = 8
DFT_N2 = 256
DFT_ROWS = 128
DFT_RPB = 16
DFT_SLAB = 256
DFT_STAGE2_BLOCKS = 17
MOE_TILE = 512
MOE_CHUNK = 64
MOE_WINDOW = 192
MOE_SUB = 2
LANES = 128
V7X_VMEM_BYTES = 64 * 1024 * 1024
VMEM_LIMIT = V7X_VMEM_BYTES * 7 // 8


def _cparams(sem):
    return pltpu.CompilerParams(dimension_semantics=sem, vmem_limit_bytes=VMEM_LIMIT)


def _const_spec(shape):
    nd = len(shape)
    return pl.BlockSpec(shape, lambda *_: (0,) * nd, pipeline_mode=pl.Buffered(1))


def _mod_body(c_ref, w_ref, b_ref, o_ref):
    c = c_ref[...]
    s = c * jax.nn.sigmoid(c)
    o_ref[...] = jnp.dot(s, w_ref[...], preferred_element_type=f32, precision=HIGHEST) + b_ref[...]


def _mod_call(c_all, ada_w, ada_b):
    rows = c_all.shape[0]
    depth = ada_w.shape[0]
    tn = MOD_COLS
    return pl.pallas_call(
        _mod_body,
        grid=(depth, 6 * D_MODEL // tn),
        in_specs=[pl.BlockSpec((rows, D_MODEL), lambda l, j: (0, 0)),
                  pl.BlockSpec((None, D_MODEL, tn), lambda l, j: (l, 0, j)),
                  pl.BlockSpec((None, 1, tn), lambda l, j: (l, 0, j))],
        out_specs=pl.BlockSpec((None, rows, tn), lambda l, j: (l, 0, j)),
        out_shape=jax.ShapeDtypeStruct((depth, rows, 6 * D_MODEL), f32),
        compiler_params=_cparams(("arbitrary", "arbitrary")),
        name="mod",
    )(c_all, ada_w, ada_b[:, None, :])


def _proj_body(xm_ref, xp_ref, xn_ref, sh_ref, sc_ref, g_ref, win_ref, hyw_ref, pw_ref, ps_ref, scw_ref,
               wbb_ref, wbc_ref, v_ref, x1_ref, x2_ref, g0_ref, rest_ref, *, tm, seq_len):
    i = pl.program_id(1)
    nt = pl.num_programs(1)
    rt = tm + 2 * HALO
    ctr = slice(HALO, HALO + tm)

    def modulated(x):
        ms = jnp.mean(x * x, axis=-1, keepdims=True)
        h = x * lax.rsqrt(ms + EPS) * g_ref[...]
        return h * (1.0 + sc_ref[...]) + sh_ref[...]

    hp = jnp.where(i > 0, modulated(xp_ref[...]), 0.0)
    hn = jnp.where(i < nt - 1, modulated(xn_ref[...]), 0.0)
    hc = modulated(xm_ref[...])
    hb = jnp.concatenate([hp, hc, hn], axis=0).astype(bf16)
    hcb = hc.astype(bf16)

    def down(a, s):
        return pltpu.roll(a, s, 0)

    def up(a, s):
        return pltpu.roll(a, rt - s, 0)

    u = jnp.dot(hb, win_ref[:, 0:COL_HY], preferred_element_type=f32)
    w = hyw_ref[...]
    uc = (down(u, 1) * w[0:1] + u * w[1:2] + up(u, 1) * w[2:3])[ctr]
    v_ref[...] = uc[:, 0:W_MIX]
    x1_ref[...] = uc[:, W_MIX:2 * W_MIX]
    x2_ref[...] = uc[:, 2 * W_MIX:3 * W_MIX]

    q = jnp.dot(hb, win_ref[:, OFF_POOL:OFF_POOL + COL_POOL], preferred_element_type=f32)
    s2 = q + down(q, 1)
    s4 = s2 + down(s2, 2)
    s8 = s4 + down(s4, 4)
    s16 = s8 + down(s8, 8)
    tpos = i * tm + lax.broadcasted_iota(jnp.int32, (tm, 1), 0)
    pooled = []
    for g, (win, ssum) in enumerate(zip(POOL_WINDOWS, (s2, s4, s8, s16))):
        lo = win // 2
        hi = win - 1 - lo
        lanes = slice(g * POOL_GROUP, (g + 1) * POOL_GROUP)
        ws = ssum[:, lanes]
        if hi > 0:
            ws = up(ws, hi)
        cnt = (jnp.minimum(tpos + hi + 1, seq_len) - jnp.maximum(tpos - lo, 0)).astype(f32)
        p = ws[ctr] / cnt - q[ctr, lanes]
        pooled.append(jnp.dot(p.astype(bf16), pw_ref[g], preferred_element_type=f32))
    yb_in = jnp.concatenate(pooled, axis=1) * ps_ref[...]
    yb = jnp.dot(yb_in.astype(bf16), wbb_ref[...], preferred_element_type=f32)

    us = jnp.dot(hb, win_ref[:, OFF_SC:OFF_SC + COL_SC], preferred_element_type=f32)
    cx = us[:, W_MIX:2 * W_MIX] * us[:, 2 * W_MIX:3 * W_MIX]
    sw = scw_ref[...]
    dw = down(cx, 1) * sw[0:1] + cx * sw[1:2] + up(cx, 1) * sw[2:3]
    sc_out = (us[:, 0:W_MIX] * dw)[ctr]
    yc = jnp.dot(sc_out.astype(bf16), wbc_ref[...], preferred_element_type=f32)

    gt = 0.5 * jnp.tanh(0.5 * jnp.dot(hcb, win_ref[:, OFF_GATE:PROJ_COLS], preferred_element_type=f32)) + 0.5
    g0_ref[...] = gt[:, 0:D_MODEL]
    rest_ref[...] = gt[:, D_MODEL:2 * D_MODEL] * yb + gt[:, 2 * D_MODEL:3 * D_MODEL] * yc


def _proj_call(x, sh, sc, g1, win, hyw, pw, ps, scw, wbb, wbc, tm):
    B, L, D = x.shape
    nt = L // tm
    hb = tm // HALO
    row = lambda b, i: (b, i, 0)
    vec = lambda b, i: (b, 0, 0)
    out_w = jax.ShapeDtypeStruct((B, L, W_MIX), f32)
    out_d = jax.ShapeDtypeStruct((B, L, D), f32)
    return pl.pallas_call(
        functools.partial(_proj_body, tm=tm, seq_len=L),
        grid=(B, nt),
        in_specs=[
            pl.BlockSpec((None, tm, D), row),
            pl.BlockSpec((None, HALO, D), lambda b, i: (b, jnp.maximum(i * hb - 1, 0), 0)),
            pl.BlockSpec((None, HALO, D), lambda b, i: (b, jnp.minimum((i + 1) * hb, L // HALO - 1), 0)),
            pl.BlockSpec((None, 1, D), vec),
            pl.BlockSpec((None, 1, D), vec),
            _const_spec((1, D)),
            _const_spec((D, PROJ_COLS)),
            _const_spec((3, COL_HY)),
            _const_spec((len(POOL_WINDOWS), POOL_GROUP, POOL_GROUP)),
            _const_spec((1, W_MIX)),
            _const_spec((3, W_MIX)),
            _const_spec((W_MIX, D)),
            _const_spec((W_MIX, D)),
        ],
        out_specs=[pl.BlockSpec((None, tm, W_MIX), row)] * 3 + [pl.BlockSpec((None, tm, D), row)] * 2,
        out_shape=[out_w, out_w, out_w, out_d, out_d],
        compiler_params=_cparams(("parallel", "arbitrary")),
        name="proj",
    )(x, x, x, sh, sc, g1, win, hyw, pw, ps, scw, wbb, wbc)


def _filter_body(ca_ref, sa_ref, cb_ref, sb_ref, w1_ref, b1_ref, w2_ref, b2_ref, fr_ref, wo0_ref, wo1_ref, dl_ref, g_ref,
                 o_ref, asum_ref, h_s, *, rpb, seq_len, n1c):
    j = pl.program_id(0)
    s = pl.program_id(1)
    L = seq_len
    n2 = lax.broadcasted_iota(jnp.int32, (DFT_N2, 1), 0)
    fwd = n2 < DFT_ROWS

    def slot(r):
        pos = (j * rpb + r) + n1c * n2
        return pos, jnp.where(fwd, pos, 2 * L - pos).astype(f32)

    @pl.when(s == 0)
    def _():
        lane = lax.broadcasted_iota(jnp.int32, (DFT_N2, LANES), 1)
        fr = fr_ref[...]
        cb, sb = cb_ref[...], sb_ref[...]
        zs = []
        for r in range(rpb):
            _, lag = slot(r)
            ca = ca_ref[r:r + 1, :]
            sa = jnp.where(fwd, sa_ref[r:r + 1, :], -sa_ref[r:r + 1, :])
            cos_t = ca * cb - sa * sb
            sin_t = sa * cb + ca * sb
            z = jnp.where(lane == 0, lag / (L - 1), jnp.where(lane <= HY_BANDS, cos_t, -sin_t))
            zs.append(jnp.concatenate([z[:DFT_ROWS], z[DFT_ROWS:]], axis=1))
        zz = jnp.concatenate(zs, axis=0)
        h = jnp.sin(fr * (jnp.dot(zz, w1_ref[...], preferred_element_type=f32, precision=HIGHEST) + b1_ref[...]))
        h_s[...] = jnp.sin(fr * (jnp.dot(h, w2_ref[...], preferred_element_type=f32, precision=HIGHEST) + b2_ref[...]))

    hb = h_s[...].astype(bf16)
    ho_f = jnp.dot(hb, wo0_ref[...], preferred_element_type=f32)
    ho_b = jnp.dot(hb, wo1_ref[...], preferred_element_type=f32)
    asum = jnp.zeros(asum_ref.shape[1:], f32)
    bs = []
    for r in range(rpb):
        pos, lag = slot(r)
        rows = slice(r * DFT_ROWS, (r + 1) * DFT_ROWS)
        ho = jnp.concatenate([ho_f[rows], ho_b[rows]], axis=0)
        k = jnp.where(pos == L, 0.0, ho * jnp.exp(-(lag / (L - 1)) * dl_ref[...]))
        asum = asum + jnp.sum(jnp.abs(k), axis=0, keepdims=True)
        bs.append(jnp.dot(g_ref[r], k.astype(bf16), preferred_element_type=f32))
    o_ref[...] = jnp.swapaxes(jnp.stack(bs, axis=0), 0, 1).reshape(o_ref.shape).astype(bf16)

    @pl.when(j == 0)
    def _():
        asum_ref[s] = asum

    @pl.when(j > 0)
    def _():
        asum_ref[s] += asum


def _filter_tables(L):
    n1c = 2 * L // DFT_N2
    bands = jnp.linspace(1e-4, HY_BANDS - 1, HY_BANDS, dtype=f32)
    brow = jnp.zeros((LANES,), f32).at[1:1 + HY_BANDS].set(bands).at[1 + HY_BANDS:1 + 2 * HY_BANDS].set(bands)
    used = (jnp.arange(LANES) >= 1) & (jnp.arange(LANES) <= 2 * HY_BANDS)
    n2 = jnp.arange(DFT_N2)
    part_a = jnp.arange(n1c).astype(f32)
    part_b = (n1c * jnp.where(n2 < DFT_ROWS, n2, DFT_N2 - n2)).astype(f32)

    def cs(part):
        ang = (2 * math.pi / L) * part[:, None] * brow[None, :]
        return jnp.where(used, jnp.cos(ang), 0.0), jnp.where(used, jnp.sin(ang), 0.0)

    return cs(part_a) + cs(part_b)


def _filter_call(L, g_fwd, w1p, b1p, w2p, b2p, frp, wo_f, wo_b, dl_row):
    n1c = 2 * L // DFT_N2
    rpb = min(DFT_RPB, n1c)
    cw = HY_ORDER * W_MIX
    cs = DFT_SLAB
    ns = cw // cs
    ca, sa, cb, sb = _filter_tables(L)
    kh = g_fwd.shape[1] // 2
    bs, asum = pl.pallas_call(
        functools.partial(_filter_body, rpb=rpb, seq_len=L, n1c=n1c),
        grid=(n1c // rpb, ns),
        in_specs=[pl.BlockSpec((rpb, LANES), lambda j, s: (j, 0)), pl.BlockSpec((rpb, LANES), lambda j, s: (j, 0)),
                  _const_spec((DFT_N2, LANES)), _const_spec((DFT_N2, LANES)),
                  _const_spec((2 * LANES, LANES)), _const_spec((1, LANES)),
                  _const_spec((LANES, LANES)), _const_spec((1, LANES)), _const_spec((1, LANES)),
                  pl.BlockSpec((LANES, cs), lambda j, s: (0, s)),
                  pl.BlockSpec((LANES, cs), lambda j, s: (0, s)),
                  pl.BlockSpec((1, cs), lambda j, s: (0, s)),
                  pl.BlockSpec((rpb, 2 * kh, DFT_N2), lambda j, s: (j, 0, 0))],
        out_specs=[pl.BlockSpec((2, kh, rpb, cs), lambda j, s: (0, 0, j, s)),
                   pl.BlockSpec((ns, 1, cs), lambda j, s: (0, 0, 0))],
        out_shape=[jax.ShapeDtypeStruct((2, kh, n1c, cw), bf16), jax.ShapeDtypeStruct((ns, 1, cs), f32)],
        scratch_shapes=[pltpu.VMEM((rpb * DFT_ROWS, LANES), f32)],
        compiler_params=_cparams(("arbitrary", "arbitrary")),
        name="filt",
    )(ca, sa, cb, sb, w1p, b1p, w2p, b2p, frp, wo_f, wo_b, dl_row, g_fwd)
    return bs, asum.reshape(1, cw)


def _dft_half_rows(n1c):
    step = max(8, DFT_ROWS // n1c)
    return -(-(DFT_ROWS + 1) // step) * step


def _dft_tables(L):
    n = 2 * L
    n1c = n // DFT_N2
    kh = _dft_half_rows(n1c)
    k2 = jnp.arange(DFT_N2, dtype=jnp.int32)
    tw_ang = ((jnp.arange(n1c, dtype=jnp.int32)[:, None] * k2[None, :]) % n).astype(f32) * (2 * math.pi / n)
    f_ang = ((k2[:, None] * k2[None, :]) % DFT_N2).astype(f32) * (2 * math.pi / DFT_N2)
    twr, twi = jnp.cos(tw_ang)[:, :kh, None], -jnp.sin(tw_ang)[:, :kh, None]
    fr, fi = jnp.cos(f_ang)[None, :kh], -jnp.sin(f_ang)[None, :kh]
    gr = twr * fr - twi * fi
    gi = twr * fi + twi * fr
    g_fwd = jnp.concatenate([gr, gi], axis=1)
    kk = jnp.arange(kh)
    wgt = jnp.where(kk > DFT_ROWS, 0.0, jnp.where((kk == 0) | (kk == DFT_ROWS), 1.0, 2.0)) * (1.0 / n)
    twr_t, twi_t = jnp.cos(tw_ang)[:, None, :kh] * wgt, -jnp.sin(tw_ang)[:, None, :kh] * wgt
    fr_t = jnp.cos(f_ang)[None, :DFT_ROWS, :kh]
    fi_t = -jnp.sin(f_ang)[None, :DFT_ROWS, :kh]
    g_inv = jnp.concatenate([twr_t * fr_t - twi_t * fi_t, twr_t * fi_t + twi_t * fr_t], axis=2)
    a = jnp.arange(n1c, dtype=jnp.int32)
    s_ang = ((a[:, None] * a[None, :]) % n1c).astype(f32) * (2 * math.pi / n1c)
    eye = jnp.eye(DFT_ROWS // n1c, dtype=f32)
    sr = jnp.kron(eye, jnp.cos(s_ang))
    si = jnp.kron(eye, -jnp.sin(s_ang))
    m_fwd = jnp.block([[sr, -si], [si, sr]])
    m_inv = jnp.block([[sr, si], [-si, sr]])
    return dict(g_fwd=g_fwd.astype(bf16), g_inv=g_inv.astype(bf16), m_fwd=m_fwd.astype(bf16),
                m_inv=m_inv.astype(bf16), n1=n1c, kh=kh)


def _n1_major(a):
    return jnp.swapaxes(a, 0, 1)


def _n1_minor(mats, shape):
    return jnp.swapaxes(jnp.stack(mats, axis=0), 0, 1).reshape(shape)


def _fft1_body(x_ref, g_ref, o_ref, *, rpb):
    x = _n1_major(x_ref[...])
    bs = [jnp.dot(g_ref[r], x[r].astype(bf16), preferred_element_type=f32) for r in range(rpb)]
    o_ref[...] = _n1_minor(bs, o_ref.shape).astype(bf16)


def _fft1_call(x4, g_fwd):
    B, _, n1c, C = x4.shape
    rpb = min(DFT_RPB, n1c)
    cs = DFT_SLAB
    kh = g_fwd.shape[1] // 2
    return pl.pallas_call(
        functools.partial(_fft1_body, rpb=rpb),
        grid=(B, n1c // rpb, C // cs),
        in_specs=[pl.BlockSpec((None, DFT_ROWS, rpb, cs), lambda b, j, s: (b, 0, j, s)),
                  pl.BlockSpec((rpb, 2 * kh, DFT_ROWS), lambda b, j, s: (j, 0, 0))],
        out_specs=pl.BlockSpec((None, 2, kh, rpb, cs), lambda b, j, s: (b, 0, 0, j, s)),
        out_shape=jax.ShapeDtypeStruct((B, 2, kh, n1c, C), bf16),
        compiler_params=_cparams(("parallel", "arbitrary", "arbitrary")),
        name="fft1",
    )(x4, g_fwd)


def _fft2_body(b_ref, kb_ref, asum_ref, mf_ref, mi_ref, o_ref, k_s, *, nsub):
    def block_rows(i):
        return slice(i * DFT_ROWS, (i + 1) * DFT_ROWS)

    def stacked(ref, rows):
        return jnp.concatenate([ref[0, rows, :], ref[1, rows, :]], axis=0)

    @pl.when(pl.program_id(1) == 0)
    def _():
        inv = 1.0 / asum_ref[...]
        for i in range(nsub):
            rows = block_rows(i)
            ks = jnp.dot(mf_ref[...], stacked(kb_ref, rows), preferred_element_type=f32)
            k_s[0, rows, :] = ks[:DFT_ROWS] * inv
            k_s[1, rows, :] = ks[DFT_ROWS:] * inv

    for i in range(nsub):
        rows = block_rows(i)
        xs = jnp.dot(mf_ref[...], stacked(b_ref, rows), preferred_element_type=f32)
        xr, xi = xs[:DFT_ROWS], xs[DFT_ROWS:]
        kr, ki = k_s[0, rows, :], k_s[1, rows, :]
        ys = jnp.concatenate([xr * kr - xi * ki, xr * ki + xi * kr], axis=0).astype(bf16)
        cs = jnp.dot(mi_ref[...], ys, preferred_element_type=f32)
        o_ref[0, rows, :] = cs[:DFT_ROWS].astype(bf16)
        o_ref[1, rows, :] = cs[DFT_ROWS:].astype(bf16)


def _fft2_call(bs, kb, asum, order, m_fwd, m_inv):
    B, _, n, C = bs.shape
    blocks = n // DFT_ROWS
    per_step = max(d for d in range(1, min(DFT_STAGE2_BLOCKS, blocks) + 1) if blocks % d == 0)
    rb = per_step * DFT_ROWS
    blk = pl.BlockSpec((None, 2, rb, C), lambda j, b: (b, 0, j, 0))
    return pl.pallas_call(
        functools.partial(_fft2_body, nsub=per_step),
        grid=(n // rb, B),
        in_specs=[blk,
                  pl.BlockSpec((2, rb, C), lambda j, b: (0, j, order)),
                  pl.BlockSpec((1, C), lambda j, b: (0, order)),
                  _const_spec((2 * DFT_ROWS, 2 * DFT_ROWS)),
                  _const_spec((2 * DFT_ROWS, 2 * DFT_ROWS))],
        out_specs=blk,
        out_shape=jax.ShapeDtypeStruct(bs.shape, bf16),
        scratch_shapes=[pltpu.VMEM((2, rb, C), f32)],
        compiler_params=_cparams(("arbitrary", "arbitrary")),
        name="fft2",
    )(bs, kb, asum, m_fwd, m_inv)


def _fft3_body(c_ref, gi_ref, gate_ref, prev_ref, sk_ref, *rest, rpb, fuse_next):
    if fuse_next:
        gf_ref, z_ref, b_ref = rest
    else:
        (z_ref,) = rest
    cs = c_ref.shape[-1]
    c = _n1_major(c_ref[...].astype(f32).reshape(2 * c_ref.shape[1], rpb, cs))
    gate = _n1_major(gate_ref[...])
    prev = _n1_major(prev_ref[...])
    sk = sk_ref[...]
    zs, bs = [], []
    for r in range(rpb):
        y = jnp.dot(gi_ref[r], c[r].astype(bf16), preferred_element_type=f32)
        z = gate[r] * (y + sk * prev[r])
        zs.append(z)
        if fuse_next:
            bs.append(jnp.dot(gf_ref[r], z.astype(bf16), preferred_element_type=f32))
    z_ref[...] = _n1_minor(zs, z_ref.shape)
    if fuse_next:
        b_ref[...] = _n1_minor(bs, b_ref.shape).astype(bf16)


def _fft3_call(cs5, g_inv, gate, prev, sk_row, g_fwd=None):
    B, _, kh, n1c, C = cs5.shape
    rpb = min(DFT_RPB, n1c)
    cs = DFT_SLAB
    tblk = pl.BlockSpec((None, DFT_ROWS, rpb, cs), lambda b, j, s: (b, 0, j, s))
    sblk = pl.BlockSpec((None, 2, kh, rpb, cs), lambda b, j, s: (b, 0, 0, j, s))
    in_specs = [sblk, pl.BlockSpec((rpb, DFT_ROWS, 2 * kh), lambda b, j, s: (j, 0, 0)), tblk, tblk,
                pl.BlockSpec((1, cs), lambda b, j, s: (0, s))]
    args = [cs5, g_inv, gate, prev, sk_row]
    out_specs = [tblk]
    out_shape = [jax.ShapeDtypeStruct(gate.shape, f32)]
    fuse_next = g_fwd is not None
    if fuse_next:
        in_specs.append(pl.BlockSpec((rpb, 2 * kh, DFT_ROWS), lambda b, j, s: (j, 0, 0)))
        args.append(g_fwd)
        out_specs.append(sblk)
        out_shape.append(jax.ShapeDtypeStruct(cs5.shape, bf16))
    return pl.pallas_call(
        functools.partial(_fft3_body, rpb=rpb, fuse_next=fuse_next),
        grid=(B, n1c // rpb, C // cs),
        in_specs=in_specs,
        out_specs=out_specs,
        out_shape=out_shape,
        compiler_params=_cparams(("parallel", "arbitrary", "arbitrary")),
        name="fft3_next" if fuse_next else "fft3",
    )(*args)


def _filter_spectrum(L, tabs, filt_params):
    bs, asum = _filter_call(L, tabs["g_fwd"], *filt_params)
    return bs.reshape(2, tabs["kh"] * tabs["n1"], bs.shape[-1]), asum


def _hyena_conv(v, x1, x2, ksp, skip, tabs):
    n1c, kh = tabs["n1"], tabs["kh"]
    B, L, C = v.shape
    n = kh * n1c
    kb, asum = ksp
    v4, x14, x24 = (a.reshape(B, DFT_ROWS, n1c, C) for a in (v, x1, x2))
    s5 = (B, 2, kh, n1c, C)
    bs = _fft1_call(v4, tabs["g_fwd"])
    cs = _fft2_call(bs.reshape(B, 2, n, C), kb, asum, 0, tabs["m_fwd"], tabs["m_inv"])
    z1, bs = _fft3_call(cs.reshape(s5), tabs["g_inv"], x14, v4, skip[0:1], tabs["g_fwd"])
    cs = _fft2_call(bs.reshape(B, 2, n, C), kb, asum, 1, tabs["m_fwd"], tabs["m_inv"])
    (z2,) = _fft3_call(cs.reshape(s5), tabs["g_inv"], x24, z1, skip[1:2])
    return z2.reshape(B, L, C)


def _route(r):
    lane = lax.broadcasted_iota(jnp.int32, r.shape, 1)
    ninf = jnp.float32(-jnp.inf)
    big = jnp.int32(1 << 20)
    is_g = lane < N_GROUPS
    gmax = jnp.max(jnp.where(is_g, r, ninf), axis=-1, keepdims=True)
    gidx = jnp.min(jnp.where(jnp.logical_and(is_g, r == gmax), lane, big), axis=-1, keepdims=True)
    gw = 1.0 / jnp.sum(jnp.where(is_g, jnp.exp(r - gmax), 0.0), axis=-1, keepdims=True)
    e_lane = lane - N_GROUPS
    sel = jnp.logical_and(jnp.logical_and(e_lane >= 0, e_lane < N_EXPERTS), (e_lane >> 2) == gidx)
    le = jnp.where(sel, r, ninf)
    m1 = jnp.max(le, axis=-1, keepdims=True)
    i1 = jnp.min(jnp.where(le == m1, lane, big), axis=-1, keepdims=True)
    le2 = jnp.where(lane == i1, ninf, le)
    m2 = jnp.max(le2, axis=-1, keepdims=True)
    i2 = jnp.min(jnp.where(le2 == m2, lane, big), axis=-1, keepdims=True)
    e2 = jnp.exp(m2 - m1)
    den = 1.0 + e2
    comb = jnp.where(lane == i1, gw / den, jnp.where(lane == i2, gw * e2 / den, 0.0))
    return comb, gidx


def _mix_body(x_ref, z_ref, g0_ref, rest_ref, ga1_ref, sh2_ref, sc2_ref, n2g_ref, wba_ref, wout_ref, wr_ref, tri_ref,
              xo_ref, hs_ref, combs_ref, pmt_ref, cnt_ref):
    ya = jnp.dot(z_ref[...].astype(bf16), wba_ref[...], preferred_element_type=f32)
    merged = g0_ref[...] * ya + rest_ref[...]
    xo = x_ref[...] + ga1_ref[...] * jnp.dot(merged.astype(bf16), wout_ref[...], preferred_element_type=f32)
    xo_ref[...] = xo
    ms = jnp.mean(xo * xo, axis=-1, keepdims=True)
    h2 = xo * lax.rsqrt(ms + EPS) * n2g_ref[...]
    h2 = h2 * (1.0 + sc2_ref[...]) + sh2_ref[...]
    h_hi = h2.astype(bf16)
    h_lo = (h2 - h_hi.astype(f32)).astype(bf16)
    p_hi = jnp.dot(h_hi, wr_ref[...], preferred_element_type=f32)
    p_lo = jnp.dot(h_lo, wr_ref[:, 0:LANES], preferred_element_type=f32)
    comb, gidx = _route(p_hi[:, 0:LANES] + p_hi[:, LANES:2 * LANES] + p_lo)

    tm = comb.shape[0]
    lane = lax.broadcasted_iota(jnp.int32, comb.shape, 1)
    onehot = (lane == gidx).astype(f32)
    cum = jnp.dot(tri_ref[...], onehot.astype(bf16), preferred_element_type=f32)
    tot8 = cum[tm - 8:tm, :]
    off8 = pltpu.roll(tot8, 1, 1) + pltpu.roll(tot8, 2, 1) + pltpu.roll(tot8, 3, 1)
    rank = jnp.sum(onehot * (off8[7:8, :] + cum - 1.0), axis=-1, keepdims=True)
    slot = lax.broadcasted_iota(jnp.int32, (tm, tm), 1).astype(f32)
    pmt = (slot == rank).astype(bf16)
    pmt_ref[...] = pmt
    both = jnp.concatenate([h_hi, comb.astype(bf16)], axis=1)
    srt = lax.dot_general(pmt, both, (((0,), (0,)), ((), ())), preferred_element_type=f32).astype(bf16)
    d = h2.shape[1]
    hs_ref[...] = srt[:, 0:d]
    combs_ref[...] = srt[:, d:d + LANES]
    cnt_ref[...] = tot8[7:8, :].astype(jnp.int32)


def _mix_call(x, z, g0, rest, ga1, sh2, sc2, n2g, wba, wout, wr, tri, tm):
    B, L, D = x.shape
    nt = L // tm
    row = lambda b, i: (b, i, 0)
    vspec = pl.BlockSpec((None, 1, D), lambda b, i: (b, 0, 0))
    return pl.pallas_call(
        _mix_body,
        grid=(B, nt),
        in_specs=[
            pl.BlockSpec((None, tm, D), row),
            pl.BlockSpec((None, tm, W_MIX), row),
            pl.BlockSpec((None, tm, D), row),
            pl.BlockSpec((None, tm, D), row),
            vspec, vspec, vspec,
            _const_spec((1, D)),
            _const_spec((W_MIX, D)),
            _const_spec((D, D)),
            _const_spec((D, 2 * LANES)),
            _const_spec((tm, tm)),
        ],
        out_specs=[pl.BlockSpec((None, tm, D), row), pl.BlockSpec((None, tm, D), row),
                   pl.BlockSpec((None, tm, LANES), row), pl.BlockSpec((None, tm, tm), row),
                   pl.BlockSpec((None, None, 1, LANES), lambda b, i: (b, i, 0, 0))],
        out_shape=[jax.ShapeDtypeStruct((B, L, D), f32), jax.ShapeDtypeStruct((B, L, D), bf16),
                   jax.ShapeDtypeStruct((B, L, LANES), bf16), jax.ShapeDtypeStruct((B, L, tm), bf16),
                   jax.ShapeDtypeStruct((B, nt, 1, LANES), jnp.int32)],
        compiler_params=_cparams(("parallel", "arbitrary")),
        name="mix",
    )(x, z, g0, rest, ga1, sh2, sc2, n2g, wba, wout, wr, tri)


def _experts_body(cnt_ref, hs_ref, combs_ref, pmt_ref, xo_ref, ga2_ref, ex_ref, w1_ref, w3_ref, w2_ref, fg_ref, o_ref,
                  acc_s, *, tm, sub, final_norm):
    b, sup, g = pl.program_id(0), pl.program_id(1), pl.program_id(2)
    nchunk = tm // MOE_CHUNK

    @pl.when(g == 0)
    def _():
        acc_s[...] = jnp.zeros_like(acc_s)

    def tile(t, carry):
        base = ((b * pl.num_programs(1) + sup) * sub + t) * N_GROUPS
        lo = jnp.int32(0)
        for gg in range(N_GROUPS - 1):
            lo = lo + jnp.where(gg < g, cnt_ref[base + gg], 0)
        hi = lo + cnt_ref[base + g]

        def run(start, nrows):
            rows = pl.ds(pl.multiple_of(t * tm + start, MOE_CHUNK), nrows)
            h = hs_ref[rows, :]
            a = jnp.dot(h, w1_ref[...], preferred_element_type=f32)
            u = jnp.dot(h, w3_ref[...], preferred_element_type=f32)
            cw = jnp.dot(combs_ref[rows, :], ex_ref[...], preferred_element_type=f32)
            silu = 0.5 * a * (jnp.tanh(0.5 * a) + 1.0)
            acc_s[rows, :] += jnp.dot((silu * u * cw).astype(bf16), w2_ref[...], preferred_element_type=f32)

        win = jnp.minimum(lo - (lo & (MOE_CHUNK - 1)), tm - MOE_WINDOW)

        @pl.when(hi > lo)
        def _():
            run(win, MOE_WINDOW)

        for c in range(nchunk):
            @pl.when(jnp.logical_and(c * MOE_CHUNK >= win + MOE_WINDOW, hi > c * MOE_CHUNK))
            def _():
                run(c * MOE_CHUNK, MOE_CHUNK)
        return carry

    lax.fori_loop(0, sub, tile, 0)

    @pl.when(g == pl.num_programs(2) - 1)
    def _():
        for t in range(sub):
            rows = slice(t * tm, (t + 1) * tm)
            y2 = jnp.dot(pmt_ref[rows, :], acc_s[rows, :].astype(bf16), preferred_element_type=f32)
            y = xo_ref[rows, :] + ga2_ref[...] * y2
            if final_norm:
                ms = jnp.mean(y * y, axis=-1, keepdims=True)
                y = y * lax.rsqrt(ms + EPS) * fg_ref[...]
            o_ref[rows, :] = y


def _experts_call(cnt, hs, combs, pmt, xo, ga2, ex, w1, w3, w2, fg, tm, final_norm):
    B, L, D = hs.shape
    sub = min(MOE_SUB, L // tm)
    rows = sub * tm
    blk = lambda b, s, g, cnt: (b, s, 0)
    return pl.pallas_call(
        functools.partial(_experts_body, tm=tm, sub=sub, final_norm=final_norm),
        grid_spec=pltpu.PrefetchScalarGridSpec(
            num_scalar_prefetch=1,
            grid=(B, L // rows, N_GROUPS),
            in_specs=[
                pl.BlockSpec((None, rows, D), blk),
                pl.BlockSpec((None, rows, LANES), blk),
                pl.BlockSpec((None, rows, tm), blk),
                pl.BlockSpec((None, rows, D), blk),
                pl.BlockSpec((None, 1, D), lambda b, s, g, cnt: (b, 0, 0)),
                pl.BlockSpec((None, LANES, GROUP_HID), lambda b, s, g, cnt: (g, 0, 0)),
                pl.BlockSpec((D, GROUP_HID), lambda b, s, g, cnt: (0, g)),
                pl.BlockSpec((D, GROUP_HID), lambda b, s, g, cnt: (0, g)),
                pl.BlockSpec((GROUP_HID, D), lambda b, s, g, cnt: (g, 0)),
                pl.BlockSpec((1, D), lambda b, s, g, cnt: (0, 0)),
            ],
            out_specs=pl.BlockSpec((None, rows, D), blk),
            scratch_shapes=[pltpu.VMEM((rows, D), f32)],
        ),
        out_shape=jax.ShapeDtypeStruct((B, L, D), f32),
        compiler_params=_cparams(("arbitrary", "arbitrary", "arbitrary")),
        name="experts",
    )(cnt, hs, combs, pmt, xo, ga2, ex, w1, w3, w2, fg)


def _pad_to(a, shape):
    return jnp.pad(a, [(0, s - d) for d, s in zip(a.shape, shape)])


def _prep_layer(l, p):
    max_decay = math.log(HY_TARGET) / HY_FAST_DECAY
    min_decay = math.log(HY_TARGET) / HY_SLOW_DECAY
    deltas = jnp.abs(jnp.linspace(min_decay, max_decay, W_MIX, dtype=f32))
    router = jnp.concatenate([p["router_g"][l], p["router_e"][l]], axis=1)
    lanes = jnp.arange(LANES)[None, :, None]
    cols = jnp.arange(GROUP_HID)[None, None, :]
    grp = jnp.arange(N_GROUPS)[:, None, None]
    expand = (lanes == N_GROUPS + EXP_PER_GROUP * grp + cols // D_EXPERT).astype(bf16)
    router = _pad_to(router, (D_MODEL, LANES))
    router_hi = router.astype(bf16)
    router_lo = (router - router_hi.astype(f32)).astype(bf16)
    zh = jnp.zeros((HY_HID, HY_HID), f32)
    w1 = _pad_to(p["hy_w1"][l], (LANES, HY_HID))
    zw1 = jnp.zeros_like(w1)
    w1_pair = jnp.block([[w1, zw1], [zw1, w1]])
    w2_pair = jnp.block([[p["hy_w2"][l], zh], [zh, p["hy_w2"][l]]])
    pair = lambda a: jnp.concatenate([a, a])[None]
    cw = HY_ORDER * W_MIX
    wo = p["hy_w_out"][l]
    zwo = jnp.zeros((HY_HID, cw), f32)
    wo_f = jnp.concatenate([wo[:, :cw], zwo], axis=0).astype(bf16)
    wo_b = jnp.concatenate([zwo, wo[:, cw:]], axis=0).astype(bf16)
    return dict(
        norm1_g=p["norm1_g"][l][None], norm2_g=p["norm2_g"][l][None],
        w_in=p["w_in"][l].astype(bf16), hy_conv_w=p["hy_conv_w"][l], hy_skip=p["hy_skip"][l],
        pool_w=p["pool_w"][l].astype(bf16), pool_scale=p["pool_scale"][l][None], sc_conv_w=p["sc_conv_w"][l],
        w_br_a=p["w_br_a"][l].astype(bf16), w_br_b=p["w_br_b"][l].astype(bf16), w_br_c=p["w_br_c"][l].astype(bf16),
        w_out=p["w_out"][l].astype(bf16),
        router=jnp.concatenate([router_hi, router_lo], axis=1), expand=expand,
        moe_w1=p["moe_w1"][l].astype(bf16), moe_w3=p["moe_w3"][l].astype(bf16), moe_w2=p["moe_w2"][l].astype(bf16),
        filt=(w1_pair, pair(p["hy_b1"][l]), w2_pair, pair(p["hy_b2"][l]), pair(p["hy_freq"][l]), wo_f, wo_b,
              pair(deltas)),
    )


def _tile(L, want):
    return want if L % want == 0 else L


def _encoder_layer(x, mod, lp, ksp, tabs, final_g, final_norm):
    B, L, D = x.shape
    sh1, sc1, ga1, sh2, sc2, ga2 = (m[:, None, :] for m in jnp.split(mod, 6, axis=-1))
    v, x1, x2, g0, rest = _proj_call(x, sh1, sc1, lp["norm1_g"], lp["w_in"], lp["hy_conv_w"], lp["pool_w"],
                                     lp["pool_scale"], lp["sc_conv_w"], lp["w_br_b"], lp["w_br_c"], _tile(L, PROJ_TILE))
    z = _hyena_conv(v, x1, x2, ksp, lp["hy_skip"], tabs)
    tm = _tile(L, MOE_TILE)
    tri = jnp.tri(tm, dtype=bf16)
    xo, hs, combs, pmt, cnt = _mix_call(x, z, g0, rest, ga1, sh2, sc2, lp["norm2_g"], lp["w_br_a"], lp["w_out"],
                                        lp["router"], tri, tm)
    return _experts_call(cnt[:, :, 0, :N_GROUPS].reshape(-1), hs, combs, pmt, xo, ga2, lp["expand"], lp["moe_w1"],
                         lp["moe_w3"], lp["moe_w2"], final_g, tm, final_norm)


def _forward(xs, cs, p, final_g):
    depth = p["w_in"].shape[0]
    nb = [c.shape[0] for c in cs]
    rows = -(-sum(nb) // 8) * 8
    c_all = _pad_to(jnp.concatenate(cs, axis=0), (rows, D_MODEL))
    lens = sorted({x.shape[1] for x in xs})
    tabs = {L: _dft_tables(L) for L in lens}
    fg = final_g[None]
    mods = _mod_call(c_all, p["ada_w"], p["ada_b"])
    for l in range(depth):
        lp = _prep_layer(l, p)
        mod = mods[l]
        ksp = {L: _filter_spectrum(L, tabs[L], lp["filt"]) for L in lens}
        off = 0
        out = []
        for x, n in zip(xs, nb):
            L = x.shape[1]
            out.append(_encoder_layer(x, mod[off:off + n], lp, ksp[L], tabs[L], fg, l == depth - 1))
            off += n
        xs = out
    return xs


def kernel(x_prompt, x_sample, c_prompt, c_sample, ada_w, ada_b, norm1_g, norm2_g, w_in, hy_conv_w, hy_skip, hy_w1, hy_b1, hy_w2, hy_b2, hy_w_out, hy_freq, pool_w, pool_scale, sc_conv_w, w_br_a, w_br_b, w_br_c, w_out, router_g, router_e, moe_w1, moe_w3, moe_w2, final_g):
    p = dict(ada_w=ada_w, ada_b=ada_b, norm1_g=norm1_g, norm2_g=norm2_g, w_in=w_in, hy_conv_w=hy_conv_w,
             hy_skip=hy_skip, hy_w1=hy_w1, hy_b1=hy_b1, hy_w2=hy_w2, hy_b2=hy_b2, hy_w_out=hy_w_out, hy_freq=hy_freq,
             pool_w=pool_w, pool_scale=pool_scale, sc_conv_w=sc_conv_w, w_br_a=w_br_a, w_br_b=w_br_b, w_br_c=w_br_c,
             w_out=w_out, router_g=router_g, router_e=router_e, moe_w1=moe_w1, moe_w3=moe_w3, moe_w2=moe_w2)
    y_prompt, y_sample = _forward([x_prompt, x_sample], [c_prompt, c_sample], p, final_g)
    return (y_prompt, y_sample)
```

```python
import functools
import math

import jax
import jax.numpy as jnp
from jax import lax
from jax.experimental import pallas as pl
from jax.experimental.pallas import tpu as pltpu

f32 = jnp.float32
bf16 = jnp.bfloat16
HIGHEST = lax.Precision.HIGHEST

D_MODEL = 1024
W_MIX = 512
HY_ORDER = 2
HY_BANDS = 16
HY_HID = 64
HY_FAST_DECAY = 0.3
HY_SLOW_DECAY = 1.5
HY_TARGET = 1e-2
POOL_WINDOWS = (2, 4, 8, 16)
POOL_GROUP = W_MIX // len(POOL_WINDOWS)
COL_HY = 3 * W_MIX
COL_POOL = W_MIX
COL_SC = 3 * W_MIX
COL_GATE = 3 * D_MODEL
OFF_POOL = COL_HY
OFF_SC = COL_HY + COL_POOL
OFF_GATE = COL_HY + COL_POOL + COL_SC
PROJ_COLS = OFF_GATE + COL_GATE
N_GROUPS = 4
EXP_PER_GROUP = 4
N_EXPERTS = N_GROUPS * EXP_PER_GROUP
D_EXPERT = 256
GROUP_HID = EXP_PER_GROUP * D_EXPERT
EPS = 1e-6

PROJ_TILE = 512
MOD_COLS = 1536
HALO = 8
DFT_N2 = 256
DFT_ROWS = 128
DFT_RPB = 16
DFT_SLAB = 256
DFT_STAGE2_BLOCKS = 17
MOE_TILE = 512
MOE_CHUNK = 32
MOE_WINDOW = 160
MOE_SUB = 2
LANES = 128
V7X_VMEM_BYTES = 64 * 1024 * 1024
VMEM_LIMIT = V7X_VMEM_BYTES * 7 // 8


def _cparams(sem):
    return pltpu.CompilerParams(dimension_semantics=sem, vmem_limit_bytes=VMEM_LIMIT)


def _const_spec(shape):
    nd = len(shape)
    return pl.BlockSpec(shape, lambda *_: (0,) * nd, pipeline_mode=pl.Buffered(1))


def _mod_body(c_ref, w_ref, b_ref, o_ref):
    c = c_ref[...]
    s = c * jax.nn.sigmoid(c)
    o_ref[...] = jnp.dot(s, w_ref[...], preferred_element_type=f32, precision=HIGHEST) + b_ref[...]


def _mod_call(c_all, ada_w, ada_b):
    rows = c_all.shape[0]
    depth = ada_w.shape[0]
    tn = MOD_COLS
    return pl.pallas_call(
        _mod_body,
        grid=(depth, 6 * D_MODEL // tn),
        in_specs=[pl.BlockSpec((rows, D_MODEL), lambda l, j: (0, 0)),
                  pl.BlockSpec((None, D_MODEL, tn), lambda l, j: (l, 0, j)),
                  pl.BlockSpec((None, 1, tn), lambda l, j: (l, 0, j))],
        out_specs=pl.BlockSpec((None, rows, tn), lambda l, j: (l, 0, j)),
        out_shape=jax.ShapeDtypeStruct((depth, rows, 6 * D_MODEL), f32),
        compiler_params=_cparams(("arbitrary", "arbitrary")),
        name="mod",
    )(c_all, ada_w, ada_b[:, None, :])


def _proj_body(xm_ref, xp_ref, xn_ref, sh_ref, sc_ref, g_ref, win_ref, hyw_ref, pw_ref, ps_ref, scw_ref,
               wbb_ref, wbc_ref, v_ref, x1_ref, x2_ref, g0_ref, rest_ref, *, tm, seq_len):
    i = pl.program_id(1)
    nt = pl.num_programs(1)
    rt = tm + 2 * HALO
    ctr = slice(HALO, HALO + tm)

    def modulated(x):
        ms = jnp.mean(x * x, axis=-1, keepdims=True)
        h = x * lax.rsqrt(ms + EPS) * g_ref[...]
        return h * (1.0 + sc_ref[...]) + sh_ref[...]

    hp = jnp.where(i > 0, modulated(xp_ref[...]), 0.0)
    hn = jnp.where(i < nt - 1, modulated(xn_ref[...]), 0.0)
    hc = modulated(xm_ref[...])
    hb = jnp.concatenate([hp, hc, hn], axis=0).astype(bf16)
    hcb = hc.astype(bf16)

    def down(a, s):
        return pltpu.roll(a, s, 0)

    def up(a, s):
        return pltpu.roll(a, rt - s, 0)

    u = jnp.dot(hb, win_ref[:, 0:COL_HY], preferred_element_type=f32)
    w = hyw_ref[...]
    uc = (down(u, 1) * w[0:1] + u * w[1:2] + up(u, 1) * w[2:3])[ctr]
    v_ref[...] = uc[:, 0:W_MIX]
    x1_ref[...] = uc[:, W_MIX:2 * W_MIX]
    x2_ref[...] = uc[:, 2 * W_MIX:3 * W_MIX]

    q = jnp.dot(hb, win_ref[:, OFF_POOL:OFF_POOL + COL_POOL], preferred_element_type=f32)
    s2 = q + down(q, 1)
    s4 = s2 + down(s2, 2)
    s8 = s4 + down(s4, 4)
    s16 = s8 + down(s8, 8)
    tpos = i * tm + lax.broadcasted_iota(jnp.int32, (tm, 1), 0)
    pooled = []
    for g, (win, ssum) in enumerate(zip(POOL_WINDOWS, (s2, s4, s8, s16))):
        lo = win // 2
        hi = win - 1 - lo
        lanes = slice(g * POOL_GROUP, (g + 1) * POOL_GROUP)
        ws = ssum[:, lanes]
        if hi > 0:
            ws = up(ws, hi)
        cnt = (jnp.minimum(tpos + hi + 1, seq_len) - jnp.maximum(tpos - lo, 0)).astype(f32)
        p = ws[ctr] / cnt - q[ctr, lanes]
        pooled.append(jnp.dot(p.astype(bf16), pw_ref[g], preferred_element_type=f32))
    yb_in = jnp.concatenate(pooled, axis=1) * ps_ref[...]
    yb = jnp.dot(yb_in.astype(bf16), wbb_ref[...], preferred_element_type=f32)

    us = jnp.dot(hb, win_ref[:, OFF_SC:OFF_SC + COL_SC], preferred_element_type=f32)
    cx = us[:, W_MIX:2 * W_MIX] * us[:, 2 * W_MIX:3 * W_MIX]
    sw = scw_ref[...]
    dw = down(cx, 1) * sw[0:1] + cx * sw[1:2] + up(cx, 1) * sw[2:3]
    sc_out = (us[:, 0:W_MIX] * dw)[ctr]
    yc = jnp.dot(sc_out.astype(bf16), wbc_ref[...], preferred_element_type=f32)

    gt = 0.5 * jnp.tanh(0.5 * jnp.dot(hcb, win_ref[:, OFF_GATE:PROJ_COLS], preferred_element_type=f32)) + 0.5
    g0_ref[...] = gt[:, 0:D_MODEL]
    rest_ref[...] = gt[:, D_MODEL:2 * D_MODEL] * yb + gt[:, 2 * D_MODEL:3 * D_MODEL] * yc


def _proj_call(x, sh, sc, g1, win, hyw, pw, ps, scw, wbb, wbc, tm):
    B, L, D = x.shape
    nt = L // tm
    hb = tm // HALO
    row = lambda b, i: (b, i, 0)
    vec = lambda b, i: (b, 0, 0)
    out_w = jax.ShapeDtypeStruct((B, L, W_MIX), f32)
    out_d = jax.ShapeDtypeStruct((B, L, D), f32)
    return pl.pallas_call(
        functools.partial(_proj_body, tm=tm, seq_len=L),
        grid=(B, nt),
        in_specs=[
            pl.BlockSpec((None, tm, D), row),
            pl.BlockSpec((None, HALO, D), lambda b, i: (b, jnp.maximum(i * hb - 1, 0), 0)),
            pl.BlockSpec((None, HALO, D), lambda b, i: (b, jnp.minimum((i + 1) * hb, L // HALO - 1), 0)),
            pl.BlockSpec((None, 1, D), vec),
            pl.BlockSpec((None, 1, D), vec),
            _const_spec((1, D)),
            _const_spec((D, PROJ_COLS)),
            _const_spec((3, COL_HY)),
            _const_spec((len(POOL_WINDOWS), POOL_GROUP, POOL_GROUP)),
            _const_spec((1, W_MIX)),
            _const_spec((3, W_MIX)),
            _const_spec((W_MIX, D)),
            _const_spec((W_MIX, D)),
        ],
        out_specs=[pl.BlockSpec((None, tm, W_MIX), row)] * 3 + [pl.BlockSpec((None, tm, D), row)] * 2,
        out_shape=[out_w, out_w, out_w, out_d, out_d],
        compiler_params=_cparams(("parallel", "arbitrary")),
        name="proj",
    )(x, x, x, sh, sc, g1, win, hyw, pw, ps, scw, wbb, wbc)


def _filter_body(ca_ref, sa_ref, cb_ref, sb_ref, w1_ref, b1_ref, w2_ref, b2_ref, fr_ref, wo0_ref, wo1_ref, dl_ref, g_ref,
                 o_ref, asum_ref, h_s, *, rpb, seq_len, n1c):
    j = pl.program_id(0)
    s = pl.program_id(1)
    L = seq_len
    n2 = lax.broadcasted_iota(jnp.int32, (DFT_N2, 1), 0)
    fwd = n2 < DFT_ROWS

    def slot(r):
        pos = (j * rpb + r) + n1c * n2
        return pos, jnp.where(fwd, pos, 2 * L - pos).astype(f32)

    @pl.when(s == 0)
    def _():
        lane = lax.broadcasted_iota(jnp.int32, (DFT_N2, LANES), 1)
        fr = fr_ref[...]
        cb, sb = cb_ref[...], sb_ref[...]
        zs = []
        for r in range(rpb):
            _, lag = slot(r)
            ca = ca_ref[r:r + 1, :]
            sa = jnp.where(fwd, sa_ref[r:r + 1, :], -sa_ref[r:r + 1, :])
            cos_t = ca * cb - sa * sb
            sin_t = sa * cb + ca * sb
            z = jnp.where(lane == 0, lag / (L - 1), jnp.where(lane <= HY_BANDS, cos_t, -sin_t))
            zs.append(jnp.concatenate([z[:DFT_ROWS], z[DFT_ROWS:]], axis=1))
        zz = jnp.concatenate(zs, axis=0)
        h = jnp.sin(fr * (jnp.dot(zz, w1_ref[...], preferred_element_type=f32, precision=HIGHEST) + b1_ref[...]))
        h_s[...] = jnp.sin(fr * (jnp.dot(h, w2_ref[...], preferred_element_type=f32, precision=HIGHEST) + b2_ref[...]))

    hb = h_s[...].astype(bf16)
    ho_f = jnp.dot(hb, wo0_ref[...], preferred_element_type=f32)
    ho_b = jnp.dot(hb, wo1_ref[...], preferred_element_type=f32)
    asum = jnp.zeros(asum_ref.shape[1:], f32)
    bs = []
    for r in range(rpb):
        pos, lag = slot(r)
        rows = slice(r * DFT_ROWS, (r + 1) * DFT_ROWS)
        ho = jnp.concatenate([ho_f[rows], ho_b[rows]], axis=0)
        k = jnp.where(pos == L, 0.0, ho * jnp.exp(-(lag / (L - 1)) * dl_ref[...]))
        asum = asum + jnp.sum(jnp.abs(k), axis=0, keepdims=True)
        bs.append(jnp.dot(g_ref[r], k.astype(bf16), preferred_element_type=f32))
    o_ref[...] = jnp.swapaxes(jnp.stack(bs, axis=0), 0, 1).reshape(o_ref.shape).astype(bf16)

    @pl.when(j == 0)
    def _():
        asum_ref[s] = asum

    @pl.when(j > 0)
    def _():
        asum_ref[s] += asum


def _filter_tables(L):
    n1c = 2 * L // DFT_N2
    bands = jnp.linspace(1e-4, HY_BANDS - 1, HY_BANDS, dtype=f32)
    brow = jnp.zeros((LANES,), f32).at[1:1 + HY_BANDS].set(bands).at[1 + HY_BANDS:1 + 2 * HY_BANDS].set(bands)
    used = (jnp.arange(LANES) >= 1) & (jnp.arange(LANES) <= 2 * HY_BANDS)
    n2 = jnp.arange(DFT_N2)
    part_a = jnp.arange(n1c).astype(f32)
    part_b = (n1c * jnp.where(n2 < DFT_ROWS, n2, DFT_N2 - n2)).astype(f32)

    def cs(part):
        ang = (2 * math.pi / L) * part[:, None] * brow[None, :]
        return jnp.where(used, jnp.cos(ang), 0.0), jnp.where(used, jnp.sin(ang), 0.0)

    return cs(part_a) + cs(part_b)


def _filter_call(L, g_fwd, w1p, b1p, w2p, b2p, frp, wo_f, wo_b, dl_row):
    n1c = 2 * L // DFT_N2
    rpb = min(DFT_RPB, n1c)
    cw = HY_ORDER * W_MIX
    cs = DFT_SLAB
    ns = cw // cs
    ca, sa, cb, sb = _filter_tables(L)
    kh = g_fwd.shape[1] // 2
    bs, asum = pl.pallas_call(
        functools.partial(_filter_body, rpb=rpb, seq_len=L, n1c=n1c),
        grid=(n1c // rpb, ns),
        in_specs=[pl.BlockSpec((rpb, LANES), lambda j, s: (j, 0)), pl.BlockSpec((rpb, LANES), lambda j, s: (j, 0)),
                  _const_spec((DFT_N2, LANES)), _const_spec((DFT_N2, LANES)),
                  _const_spec((2 * LANES, LANES)), _const_spec((1, LANES)),
                  _const_spec((LANES, LANES)), _const_spec((1, LANES)), _const_spec((1, LANES)),
                  pl.BlockSpec((LANES, cs), lambda j, s: (0, s)),
                  pl.BlockSpec((LANES, cs), lambda j, s: (0, s)),
                  pl.BlockSpec((1, cs), lambda j, s: (0, s)),
                  pl.BlockSpec((rpb, 2 * kh, DFT_N2), lambda j, s: (j, 0, 0))],
        out_specs=[pl.BlockSpec((2, kh, rpb, cs), lambda j, s: (0, 0, j, s)),
                   pl.BlockSpec((ns, 1, cs), lambda j, s: (0, 0, 0))],
        out_shape=[jax.ShapeDtypeStruct((2, kh, n1c, cw), bf16), jax.ShapeDtypeStruct((ns, 1, cs), f32)],
        scratch_shapes=[pltpu.VMEM((rpb * DFT_ROWS, LANES), f32)],
        compiler_params=_cparams(("arbitrary", "arbitrary")),
        name="filt",
    )(ca, sa, cb, sb, w1p, b1p, w2p, b2p, frp, wo_f, wo_b, dl_row, g_fwd)
    return bs, asum.reshape(1, cw)


def _dft_half_rows(n1c):
    step = max(8, DFT_ROWS // n1c)
    return -(-(DFT_ROWS + 1) // step) * step


def _dft_tables(L):
    n = 2 * L
    n1c = n // DFT_N2
    kh = _dft_half_rows(n1c)
    k2 = jnp.arange(DFT_N2, dtype=jnp.int32)
    tw_ang = ((jnp.arange(n1c, dtype=jnp.int32)[:, None] * k2[None, :]) % n).astype(f32) * (2 * math.pi / n)
    f_ang = ((k2[:, None] * k2[None, :]) % DFT_N2).astype(f32) * (2 * math.pi / DFT_N2)
    twr, twi = jnp.cos(tw_ang)[:, :kh, None], -jnp.sin(tw_ang)[:, :kh, None]
    fr, fi = jnp.cos(f_ang)[None, :kh], -jnp.sin(f_ang)[None, :kh]
    gr = twr * fr - twi * fi
    gi = twr * fi + twi * fr
    g_fwd = jnp.concatenate([gr, gi], axis=1)
    kk = jnp.arange(kh)
    wgt = jnp.where(kk > DFT_ROWS, 0.0, jnp.where((kk == 0) | (kk == DFT_ROWS), 1.0, 2.0)) * (1.0 / n)
    twr_t, twi_t = jnp.cos(tw_ang)[:, None, :kh] * wgt, -jnp.sin(tw_ang)[:, None, :kh] * wgt
    fr_t = jnp.cos(f_ang)[None, :DFT_ROWS, :kh]
    fi_t = -jnp.sin(f_ang)[None, :DFT_ROWS, :kh]
    g_inv = jnp.concatenate([twr_t * fr_t - twi_t * fi_t, twr_t * fi_t + twi_t * fr_t], axis=2)
    a = jnp.arange(n1c, dtype=jnp.int32)
    s_ang = ((a[:, None] * a[None, :]) % n1c).astype(f32) * (2 * math.pi / n1c)
    eye = jnp.eye(DFT_ROWS // n1c, dtype=f32)
    sr = jnp.kron(eye, jnp.cos(s_ang))
    si = jnp.kron(eye, -jnp.sin(s_ang))
    m_fwd = jnp.block([[sr, -si], [si, sr]])
    m_inv = jnp.block([[sr, si], [-si, sr]])
    return dict(g_fwd=g_fwd.astype(bf16), g_inv=g_inv.astype(bf16), m_fwd=m_fwd.astype(bf16),
                m_inv=m_inv.astype(bf16), n1=n1c, kh=kh)


def _n1_major(a):
    return jnp.swapaxes(a, 0, 1)


def _n1_minor(mats, shape):
    return jnp.swapaxes(jnp.stack(mats, axis=0), 0, 1).reshape(shape)


def _fft1_body(x_ref, g_ref, o_ref, *, rpb):
    x = _n1_major(x_ref[...])
    bs = [jnp.dot(g_ref[r], x[r].astype(bf16), preferred_element_type=f32) for r in range(rpb)]
    o_ref[...] = _n1_minor(bs, o_ref.shape).astype(bf16)


def _fft1_call(x4, g_fwd):
    B, _, n1c, C = x4.shape
    rpb = min(DFT_RPB, n1c)
    cs = DFT_SLAB
    kh = g_fwd.shape[1] // 2
    return pl.pallas_call(
        functools.partial(_fft1_body, rpb=rpb),
        grid=(B, n1c // rpb, C // cs),
        in_specs=[pl.BlockSpec((None, DFT_ROWS, rpb, cs), lambda b, j, s: (b, 0, j, s)),
                  pl.BlockSpec((rpb, 2 * kh, DFT_ROWS), lambda b, j, s: (j, 0, 0))],
        out_specs=pl.BlockSpec((None, 2, kh, rpb, cs), lambda b, j, s: (b, 0, 0, j, s)),
        out_shape=jax.ShapeDtypeStruct((B, 2, kh, n1c, C), bf16),
        compiler_params=_cparams(("parallel", "arbitrary", "arbitrary")),
        name="fft1",
    )(x4, g_fwd)


def _fft2_body(b_ref, kb_ref, asum_ref, mf_ref, mi_ref, o_ref, k_s, *, nsub):
    def block_rows(i):
        return slice(i * DFT_ROWS, (i + 1) * DFT_ROWS)

    def stacked(ref, rows):
        return jnp.concatenate([ref[0, rows, :], ref[1, rows, :]], axis=0)

    @pl.when(pl.program_id(1) == 0)
    def _():
        inv = 1.0 / asum_ref[...]
        for i in range(nsub):
            rows = block_rows(i)
            ks = jnp.dot(mf_ref[...], stacked(kb_ref, rows), preferred_element_type=f32)
            k_s[0, rows, :] = ks[:DFT_ROWS] * inv
            k_s[1, rows, :] = ks[DFT_ROWS:] * inv

    for i in range(nsub):
        rows = block_rows(i)
        xs = jnp.dot(mf_ref[...], stacked(b_ref, rows), preferred_element_type=f32)
        xr, xi = xs[:DFT_ROWS], xs[DFT_ROWS:]
        kr, ki = k_s[0, rows, :], k_s[1, rows, :]
        ys = jnp.concatenate([xr * kr - xi * ki, xr * ki + xi * kr], axis=0).astype(bf16)
        cs = jnp.dot(mi_ref[...], ys, preferred_element_type=f32)
        o_ref[0, rows, :] = cs[:DFT_ROWS].astype(bf16)
        o_ref[1, rows, :] = cs[DFT_ROWS:].astype(bf16)


def _fft2_call(bs, kb, asum, order, m_fwd, m_inv):
    B, _, n, C = bs.shape
    blocks = n // DFT_ROWS
    per_step = max(d for d in range(1, min(DFT_STAGE2_BLOCKS, blocks) + 1) if blocks % d == 0)
    rb = per_step * DFT_ROWS
    blk = pl.BlockSpec((None, 2, rb, C), lambda j, b: (b, 0, j, 0))
    return pl.pallas_call(
        functools.partial(_fft2_body, nsub=per_step),
        grid=(n // rb, B),
        in_specs=[blk,
                  pl.BlockSpec((2, rb, C), lambda j, b: (0, j, order)),
                  pl.BlockSpec((1, C), lambda j, b: (0, order)),
                  _const_spec((2 * DFT_ROWS, 2 * DFT_ROWS)),
                  _const_spec((2 * DFT_ROWS, 2 * DFT_ROWS))],
        out_specs=blk,
        out_shape=jax.ShapeDtypeStruct(bs.shape, bf16),
        scratch_shapes=[pltpu.VMEM((2, rb, C), f32)],
        compiler_params=_cparams(("arbitrary", "arbitrary")),
        name="fft2",
    )(bs, kb, asum, m_fwd, m_inv)


def _fft3_body(c_ref, gi_ref, gate_ref, prev_ref, sk_ref, *rest, rpb, fuse_next):
    if fuse_next:
        gf_ref, z_ref, b_ref = rest
    else:
        (z_ref,) = rest
    cs = c_ref.shape[-1]
    c = _n1_major(c_ref[...].astype(f32).reshape(2 * c_ref.shape[1], rpb, cs))
    gate = _n1_major(gate_ref[...])
    prev = _n1_major(prev_ref[...])
    sk = sk_ref[...]
    zs, bs = [], []
    for r in range(rpb):
        y = jnp.dot(gi_ref[r], c[r].astype(bf16), preferred_element_type=f32)
        z = gate[r] * (y + sk * prev[r])
        zs.append(z)
        if fuse_next:
            bs.append(jnp.dot(gf_ref[r], z.astype(bf16), preferred_element_type=f32))
    z_ref[...] = _n1_minor(zs, z_ref.shape)
    if fuse_next:
        b_ref[...] = _n1_minor(bs, b_ref.shape).astype(bf16)


def _fft3_call(cs5, g_inv, gate, prev, sk_row, g_fwd=None):
    B, _, kh, n1c, C = cs5.shape
    rpb = min(DFT_RPB, n1c)
    cs = DFT_SLAB
    tblk = pl.BlockSpec((None, DFT_ROWS, rpb, cs), lambda b, j, s: (b, 0, j, s))
    sblk = pl.BlockSpec((None, 2, kh, rpb, cs), lambda b, j, s: (b, 0, 0, j, s))
    in_specs = [sblk, pl.BlockSpec((rpb, DFT_ROWS, 2 * kh), lambda b, j, s: (j, 0, 0)), tblk, tblk,
                pl.BlockSpec((1, cs), lambda b, j, s: (0, s))]
    args = [cs5, g_inv, gate, prev, sk_row]
    out_specs = [tblk]
    out_shape = [jax.ShapeDtypeStruct(gate.shape, f32)]
    fuse_next = g_fwd is not None
    if fuse_next:
        in_specs.append(pl.BlockSpec((rpb, 2 * kh, DFT_ROWS), lambda b, j, s: (j, 0, 0)))
        args.append(g_fwd)
        out_specs.append(sblk)
        out_shape.append(jax.ShapeDtypeStruct(cs5.shape, bf16))
    return pl.pallas_call(
        functools.partial(_fft3_body, rpb=rpb, fuse_next=fuse_next),
        grid=(B, n1c // rpb, C // cs),
        in_specs=in_specs,
        out_specs=out_specs,
        out_shape=out_shape,
        compiler_params=_cparams(("parallel", "arbitrary", "arbitrary")),
        name="fft3_next" if fuse_next else "fft3",
    )(*args)


def _filter_spectrum(L, tabs, filt_params):
    bs, asum = _filter_call(L, tabs["g_fwd"], *filt_params)
    return bs.reshape(2, tabs["kh"] * tabs["n1"], bs.shape[-1]), asum


def _hyena_conv(v, x1, x2, ksp, skip, tabs):
    n1c, kh = tabs["n1"], tabs["kh"]
    B, L, C = v.shape
    n = kh * n1c
    kb, asum = ksp
    v4, x14, x24 = (a.reshape(B, DFT_ROWS, n1c, C) for a in (v, x1, x2))
    s5 = (B, 2, kh, n1c, C)
    bs = _fft1_call(v4, tabs["g_fwd"])
    cs = _fft2_call(bs.reshape(B, 2, n, C), kb, asum, 0, tabs["m_fwd"], tabs["m_inv"])
    z1, bs = _fft3_call(cs.reshape(s5), tabs["g_inv"], x14, v4, skip[0:1], tabs["g_fwd"])
    cs = _fft2_call(bs.reshape(B, 2, n, C), kb, asum, 1, tabs["m_fwd"], tabs["m_inv"])
    (z2,) = _fft3_call(cs.reshape(s5), tabs["g_inv"], x24, z1, skip[1:2])
    return z2.reshape(B, L, C)


def _route(r):
    lane = lax.broadcasted_iota(jnp.int32, r.shape, 1)
    ninf = jnp.float32(-jnp.inf)
    big = jnp.int32(1 << 20)
    is_g = lane < N_GROUPS
    gmax = jnp.max(jnp.where(is_g, r, ninf), axis=-1, keepdims=True)
    gidx = jnp.min(jnp.where(jnp.logical_and(is_g, r == gmax), lane, big), axis=-1, keepdims=True)
    gw = 1.0 / jnp.sum(jnp.where(is_g, jnp.exp(r - gmax), 0.0), axis=-1, keepdims=True)
    e_lane = lane - N_GROUPS
    sel = jnp.logical_and(jnp.logical_and(e_lane >= 0, e_lane < N_EXPERTS), (e_lane >> 2) == gidx)
    le = jnp.where(sel, r, ninf)
    m1 = jnp.max(le, axis=-1, keepdims=True)
    i1 = jnp.min(jnp.where(le == m1, lane, big), axis=-1, keepdims=True)
    le2 = jnp.where(lane == i1, ninf, le)
    m2 = jnp.max(le2, axis=-1, keepdims=True)
    i2 = jnp.min(jnp.where(le2 == m2, lane, big), axis=-1, keepdims=True)
    e2 = jnp.exp(m2 - m1)
    den = 1.0 + e2
    comb = jnp.where(lane == i1, gw / den, jnp.where(lane == i2, gw * e2 / den, 0.0))
    return comb, gidx


def _mix_body(x_ref, z_ref, g0_ref, rest_ref, ga1_ref, sh2_ref, sc2_ref, n2g_ref, wba_ref, wout_ref, wr_ref, tri_ref,
              xo_ref, hs_ref, combs_ref, pmt_ref, cnt_ref):
    ya = jnp.dot(z_ref[...].astype(bf16), wba_ref[...], preferred_element_type=f32)
    merged = g0_ref[...] * ya + rest_ref[...]
    xo = x_ref[...] + ga1_ref[...] * jnp.dot(merged.astype(bf16), wout_ref[...], preferred_element_type=f32)
    xo_ref[...] = xo
    ms = jnp.mean(xo * xo, axis=-1, keepdims=True)
    h2 = xo * lax.rsqrt(ms + EPS) * n2g_ref[...]
    h2 = h2 * (1.0 + sc2_ref[...]) + sh2_ref[...]
    h_hi = h2.astype(bf16)
    h_lo = (h2 - h_hi.astype(f32)).astype(bf16)
    p_hi = jnp.dot(h_hi, wr_ref[...], preferred_element_type=f32)
    p_lo = jnp.dot(h_lo, wr_ref[:, 0:LANES], preferred_element_type=f32)
    comb, gidx = _route(p_hi[:, 0:LANES] + p_hi[:, LANES:2 * LANES] + p_lo)

    tm = comb.shape[0]
    lane = lax.broadcasted_iota(jnp.int32, comb.shape, 1)
    onehot = (lane == gidx).astype(f32)
    cum = jnp.dot(tri_ref[...], onehot.astype(bf16), preferred_element_type=f32)
    tot8 = cum[tm - 8:tm, :]
    off8 = pltpu.roll(tot8, 1, 1) + pltpu.roll(tot8, 2, 1) + pltpu.roll(tot8, 3, 1)
    rank = jnp.sum(onehot * (off8[7:8, :] + cum - 1.0), axis=-1, keepdims=True)
    slot = lax.broadcasted_iota(jnp.int32, (tm, tm), 1).astype(f32)
    pmt = (slot == rank).astype(bf16)
    pmt_ref[...] = pmt
    both = jnp.concatenate([h_hi, comb.astype(bf16)], axis=1)
    srt = lax.dot_general(pmt, both, (((0,), (0,)), ((), ())), preferred_element_type=f32).astype(bf16)
    d = h2.shape[1]
    hs_ref[...] = srt[:, 0:d]
    combs_ref[...] = srt[:, d:d + LANES]
    cnt_ref[...] = tot8[7:8, :].astype(jnp.int32)


def _mix_call(x, z, g0, rest, ga1, sh2, sc2, n2g, wba, wout, wr, tri, tm):
    B, L, D = x.shape
    nt = L // tm
    row = lambda b, i: (b, i, 0)
    vspec = pl.BlockSpec((None, 1, D), lambda b, i: (b, 0, 0))
    return pl.pallas_call(
        _mix_body,
        grid=(B, nt),
        in_specs=[
            pl.BlockSpec((None, tm, D), row),
            pl.BlockSpec((None, tm, W_MIX), row),
            pl.BlockSpec((None, tm, D), row),
            pl.BlockSpec((None, tm, D), row),
            vspec, vspec, vspec,
            _const_spec((1, D)),
            _const_spec((W_MIX, D)),
            _const_spec((D, D)),
            _const_spec((D, 2 * LANES)),
            _const_spec((tm, tm)),
        ],
        out_specs=[pl.BlockSpec((None, tm, D), row), pl.BlockSpec((None, tm, D), row),
                   pl.BlockSpec((None, tm, LANES), row), pl.BlockSpec((None, tm, tm), row),
                   pl.BlockSpec((None, None, 1, LANES), lambda b, i: (b, i, 0, 0))],
        out_shape=[jax.ShapeDtypeStruct((B, L, D), f32), jax.ShapeDtypeStruct((B, L, D), bf16),
                   jax.ShapeDtypeStruct((B, L, LANES), bf16), jax.ShapeDtypeStruct((B, L, tm), bf16),
                   jax.ShapeDtypeStruct((B, nt, 1, LANES), jnp.int32)],
        compiler_params=_cparams(("parallel", "arbitrary")),
        name="mix",
    )(x, z, g0, rest, ga1, sh2, sc2, n2g, wba, wout, wr, tri)


def _experts_body(cnt_ref, hs_ref, combs_ref, pmt_ref, xo_ref, ga2_ref, ex_ref, w1_ref, w3_ref, w2_ref, fg_ref, o_ref,
                  acc_s, *, tm, sub, final_norm):
    b, sup, g = pl.program_id(0), pl.program_id(1), pl.program_id(2)
    nchunk = tm // MOE_CHUNK

    @pl.when(g == 0)
    def _():
        acc_s[...] = jnp.zeros_like(acc_s)

    def tile(t, carry):
        base = ((b * pl.num_programs(1) + sup) * sub + t) * N_GROUPS
        lo = jnp.int32(0)
        for gg in range(N_GROUPS - 1):
            lo = lo + jnp.where(gg < g, cnt_ref[base + gg], 0)
        hi = lo + cnt_ref[base + g]

        def run(start, nrows):
            rows = pl.ds(pl.multiple_of(t * tm + start, MOE_CHUNK), nrows)
            h = hs_ref[rows, :]
            a = jnp.dot(h, w1_ref[...], preferred_element_type=f32)
            u = jnp.dot(h, w3_ref[...], preferred_element_type=f32)
            cw = jnp.dot(combs_ref[rows, :], ex_ref[...], preferred_element_type=f32)
            silu = 0.5 * a * (jnp.tanh(0.5 * a) + 1.0)
            acc_s[rows, :] += jnp.dot((silu * u * cw).astype(bf16), w2_ref[...], preferred_element_type=f32)

        win = jnp.minimum(lo - (lo & (MOE_CHUNK - 1)), tm - MOE_WINDOW)

        @pl.when(hi > lo)
        def _():
            run(win, MOE_WINDOW)

        for c in range(nchunk):
            @pl.when(jnp.logical_and(c * MOE_CHUNK >= win + MOE_WINDOW, hi > c * MOE_CHUNK))
            def _():
                run(c * MOE_CHUNK, MOE_CHUNK)
        return carry

    lax.fori_loop(0, sub, tile, 0)

    @pl.when(g == pl.num_programs(2) - 1)
    def _():
        for t in range(sub):
            rows = slice(t * tm, (t + 1) * tm)
            y2 = jnp.dot(pmt_ref[rows, :], acc_s[rows, :].astype(bf16), preferred_element_type=f32)
            y = xo_ref[rows, :] + ga2_ref[...] * y2
            if final_norm:
                ms = jnp.mean(y * y, axis=-1, keepdims=True)
                y = y * lax.rsqrt(ms + EPS) * fg_ref[...]
            o_ref[rows, :] = y


def _experts_call(cnt, hs, combs, pmt, xo, ga2, ex, w1, w3, w2, fg, tm, final_norm):
    B, L, D = hs.shape
    sub = min(MOE_SUB, L // tm)
    rows = sub * tm
    blk = lambda b, s, g, cnt: (b, s, 0)
    return pl.pallas_call(
        functools.partial(_experts_body, tm=tm, sub=sub, final_norm=final_norm),
        grid_spec=pltpu.PrefetchScalarGridSpec(
            num_scalar_prefetch=1,
            grid=(B, L // rows, N_GROUPS),
            in_specs=[
                pl.BlockSpec((None, rows, D), blk),
                pl.BlockSpec((None, rows, LANES), blk),
                pl.BlockSpec((None, rows, tm), blk),
                pl.BlockSpec((None, rows, D), blk),
                pl.BlockSpec((None, 1, D), lambda b, s, g, cnt: (b, 0, 0)),
                pl.BlockSpec((None, LANES, GROUP_HID), lambda b, s, g, cnt: (g, 0, 0)),
                pl.BlockSpec((D, GROUP_HID), lambda b, s, g, cnt: (0, g)),
                pl.BlockSpec((D, GROUP_HID), lambda b, s, g, cnt: (0, g)),
                pl.BlockSpec((GROUP_HID, D), lambda b, s, g, cnt: (g, 0)),
                pl.BlockSpec((1, D), lambda b, s, g, cnt: (0, 0)),
            ],
            out_specs=pl.BlockSpec((None, rows, D), blk),
            scratch_shapes=[pltpu.VMEM((rows, D), f32)],
        ),
        out_shape=jax.ShapeDtypeStruct((B, L, D), f32),
        compiler_params=_cparams(("arbitrary", "arbitrary", "arbitrary")),
        name="experts",
    )(cnt, hs, combs, pmt, xo, ga2, ex, w1, w3, w2, fg)


def _pad_to(a, shape):
    return jnp.pad(a, [(0, s - d) for d, s in zip(a.shape, shape)])


def _prep_layer(l, p):
    max_decay = math.log(HY_TARGET) / HY_FAST_DECAY
    min_decay = math.log(HY_TARGET) / HY_SLOW_DECAY
    deltas = jnp.abs(jnp.linspace(min_decay, max_decay, W_MIX, dtype=f32))
    router = jnp.concatenate([p["router_g"][l], p["router_e"][l]], axis=1)
    lanes = jnp.arange(LANES)[None, :, None]
    cols = jnp.arange(GROUP_HID)[None, None, :]
    grp = jnp.arange(N_GROUPS)[:, None, None]
    expand = (lanes == N_GROUPS + EXP_PER_GROUP * grp + cols // D_EXPERT).astype(bf16)
    router = _pad_to(router, (D_MODEL, LANES))
    router_hi = router.astype(bf16)
    router_lo = (router - router_hi.astype(f32)).astype(bf16)
    zh = jnp.zeros((HY_HID, HY_HID), f32)
    w1 = _pad_to(p["hy_w1"][l], (LANES, HY_HID))
    zw1 = jnp.zeros_like(w1)
    w1_pair = jnp.block([[w1, zw1], [zw1, w1]])
    w2_pair = jnp.block([[p["hy_w2"][l], zh], [zh, p["hy_w2"][l]]])
    pair = lambda a: jnp.concatenate([a, a])[None]
    cw = HY_ORDER * W_MIX
    wo = p["hy_w_out"][l]
    zwo = jnp.zeros((HY_HID, cw), f32)
    wo_f = jnp.concatenate([wo[:, :cw], zwo], axis=0).astype(bf16)
    wo_b = jnp.concatenate([zwo, wo[:, cw:]], axis=0).astype(bf16)
    return dict(
        norm1_g=p["norm1_g"][l][None], norm2_g=p["norm2_g"][l][None],
        w_in=p["w_in"][l].astype(bf16), hy_conv_w=p["hy_conv_w"][l], hy_skip=p["hy_skip"][l],
        pool_w=p["pool_w"][l].astype(bf16), pool_scale=p["pool_scale"][l][None], sc_conv_w=p["sc_conv_w"][l],
        w_br_a=p["w_br_a"][l].astype(bf16), w_br_b=p["w_br_b"][l].astype(bf16), w_br_c=p["w_br_c"][l].astype(bf16),
        w_out=p["w_out"][l].astype(bf16),
        router=jnp.concatenate([router_hi, router_lo], axis=1), expand=expand,
        moe_w1=p["moe_w1"][l].astype(bf16), moe_w3=p["moe_w3"][l].astype(bf16), moe_w2=p["moe_w2"][l].astype(bf16),
        filt=(w1_pair, pair(p["hy_b1"][l]), w2_pair, pair(p["hy_b2"][l]), pair(p["hy_freq"][l]), wo_f, wo_b,
              pair(deltas)),
    )


def _tile(L, want):
    return want if L % want == 0 else L


def _encoder_layer(x, mod, lp, ksp, tabs, final_g, final_norm):
    B, L, D = x.shape
    sh1, sc1, ga1, sh2, sc2, ga2 = (m[:, None, :] for m in jnp.split(mod, 6, axis=-1))
    v, x1, x2, g0, rest = _proj_call(x, sh1, sc1, lp["norm1_g"], lp["w_in"], lp["hy_conv_w"], lp["pool_w"],
                                     lp["pool_scale"], lp["sc_conv_w"], lp["w_br_b"], lp["w_br_c"], _tile(L, PROJ_TILE))
    z = _hyena_conv(v, x1, x2, ksp, lp["hy_skip"], tabs)
    tm = _tile(L, MOE_TILE)
    tri = jnp.tri(tm, dtype=bf16)
    xo, hs, combs, pmt, cnt = _mix_call(x, z, g0, rest, ga1, sh2, sc2, lp["norm2_g"], lp["w_br_a"], lp["w_out"],
                                        lp["router"], tri, tm)
    return _experts_call(cnt[:, :, 0, :N_GROUPS].reshape(-1), hs, combs, pmt, xo, ga2, lp["expand"], lp["moe_w1"],
                         lp["moe_w3"], lp["moe_w2"], final_g, tm, final_norm)


def _forward(xs, cs, p, final_g):
    depth = p["w_in"].shape[0]
    nb = [c.shape[0] for c in cs]
    rows = -(-sum(nb) // 8) * 8
    c_all = _pad_to(jnp.concatenate(cs, axis=0), (rows, D_MODEL))
    lens = sorted({x.shape[1] for x in xs})
    tabs = {L: _dft_tables(L) for L in lens}
    fg = final_g[None]
    mods = _mod_call(c_all, p["ada_w"], p["ada_b"])
    for l in range(depth):
        lp = _prep_layer(l, p)
        mod = mods[l]
        ksp = {L: _filter_spectrum(L, tabs[L], lp["filt"]) for L in lens}
        off = 0
        out = []
        for x, n in zip(xs, nb):
            L = x.shape[1]
            out.append(_encoder_layer(x, mod[off:off + n], lp, ksp[L], tabs[L], fg, l == depth - 1))
            off += n
        xs = out
    return xs


def kernel(x_prompt, x_sample, c_prompt, c_sample, ada_w, ada_b, norm1_g, norm2_g, w_in, hy_conv_w, hy_skip, hy_w1, hy_b1, hy_w2, hy_b2, hy_w_out, hy_freq, pool_w, pool_scale, sc_conv_w, w_br_a, w_br_b, w_br_c, w_out, router_g, router_e, moe_w1, moe_w3, moe_w2, final_g):
    p = dict(ada_w=ada_w, ada_b=ada_b, norm1_g=norm1_g, norm2_g=norm2_g, w_in=w_in, hy_conv_w=hy_conv_w,
             hy_skip=hy_skip, hy_w1=hy_w1, hy_b1=hy_b1, hy_w2=hy_w2, hy_b2=hy_b2, hy_w_out=hy_w_out, hy_freq=hy_freq,
             pool_w=pool_w, pool_scale=pool_scale, sc_conv_w=sc_conv_w, w_br_a=w_br_a, w_br_b=w_br_b, w_br_c=w_br_c,
             w_out=w_out, router_g=router_g, router_e=router_e, moe_w1=moe_w1, moe_w3=moe_w3, moe_w2=moe_w2)
    y_prompt, y_sample = _forward([x_prompt, x_sample], [c_prompt, c_sample], p, final_g)
    return (y_prompt, y_sample)
```

```python
import functools
import math

import jax
import jax.numpy as jnp
from jax import lax
from jax.experimental import pallas as pl
from jax.experimental.pallas import tpu as pltpu

f32 = jnp.float32
bf16 = jnp.bfloat16
HIGHEST = lax.Precision.HIGHEST

D_MODEL = 1024
W_MIX = 512
HY_ORDER = 2
HY_BANDS = 16
HY_HID = 64
HY_FAST_DECAY = 0.3
HY_SLOW_DECAY = 1.5
HY_TARGET = 1e-2
POOL_WINDOWS = (2, 4, 8, 16)
POOL_GROUP = W_MIX // len(POOL_WINDOWS)
COL_HY = 3 * W_MIX
COL_POOL = W_MIX
COL_SC = 3 * W_MIX
COL_GATE = 3 * D_MODEL
OFF_POOL = COL_HY
OFF_SC = COL_HY + COL_POOL
OFF_GATE = COL_HY + COL_POOL + COL_SC
PROJ_COLS = OFF_GATE + COL_GATE
N_GROUPS = 4
EXP_PER_GROUP = 4
N_EXPERTS = N_GROUPS * EXP_PER_GROUP
D_EXPERT = 256
GROUP_HID = EXP_PER_GROUP * D_EXPERT
EPS = 1e-6

PROJ_TILE = 512
MOD_COLS = 1536
HALO = 8
DFT_N2 = 256
DFT_ROWS = 128
DFT_RPB = 16
DFT_SLAB = 256
DFT_STAGE2_BLOCKS = 17
MOE_TILE = 512
MOE_CHUNK = 64
MOE_WINDOW = 192
MOE_SUB = 2
LANES = 128
V7X_VMEM_BYTES = 64 * 1024 * 1024
VMEM_LIMIT = V7X_VMEM_BYTES * 7 // 8


def _cparams(sem):
    return pltpu.CompilerParams(dimension_semantics=sem, vmem_limit_bytes=VMEM_LIMIT)


def _const_spec(shape):
    nd = len(shape)
    return pl.BlockSpec(shape, lambda *_: (0,) * nd, pipeline_mode=pl.Buffered(1))


def _mod_body(c_ref, w_ref, b_ref, o_ref):
    c = c_ref[...]
    s = c * jax.nn.sigmoid(c)
    o_ref[...] = jnp.dot(s, w_ref[...], preferred_element_type=f32, precision=HIGHEST) + b_ref[...]


def _mod_call(c_all, ada_w, ada_b):
    rows = c_all.shape[0]
    depth = ada_w.shape[0]
    tn = MOD_COLS
    return pl.pallas_call(
        _mod_body,
        grid=(depth, 6 * D_MODEL // tn),
        in_specs=[pl.BlockSpec((rows, D_MODEL), lambda l, j: (0, 0)),
                  pl.BlockSpec((None, D_MODEL, tn), lambda l, j: (l, 0, j)),
                  pl.BlockSpec((None, 1, tn), lambda l, j: (l, 0, j))],
        out_specs=pl.BlockSpec((None, rows, tn), lambda l, j: (l, 0, j)),
        out_shape=jax.ShapeDtypeStruct((depth, rows, 6 * D_MODEL), f32),
        compiler_params=_cparams(("arbitrary", "arbitrary")),
        name="mod",
    )(c_all, ada_w, ada_b[:, None, :])


def _proj_body(xm_ref, xp_ref, xn_ref, sh_ref, sc_ref, g_ref, win_ref, hyw_ref, pw_ref, ps_ref, scw_ref,
               wbb_ref, wbc_ref, v_ref, x1_ref, x2_ref, g0_ref, rest_ref, *, tm, seq_len, n1_major):
    i = pl.program_id(1)
    nt = pl.num_programs(1)
    rt = tm + 2 * HALO
    ctr = slice(HALO, HALO + tm)

    def modulated(x):
        ms = jnp.mean(x * x, axis=-1, keepdims=True)
        h = x * lax.rsqrt(ms + EPS) * g_ref[...]
        return h * (1.0 + sc_ref[...]) + sh_ref[...]

    hp = jnp.where(i > 0, modulated(xp_ref[...]), 0.0)
    hn = jnp.where(i < nt - 1, modulated(xn_ref[...]), 0.0)
    hc = modulated(xm_ref[...])
    hb = jnp.concatenate([hp, hc, hn], axis=0).astype(bf16)
    hcb = hc.astype(bf16)

    def down(a, s):
        return pltpu.roll(a, s, 0)

    def up(a, s):
        return pltpu.roll(a, rt - s, 0)

    u = jnp.dot(hb, win_ref[:, 0:COL_HY], preferred_element_type=f32)
    w = hyw_ref[...]
    uc = (down(u, 1) * w[0:1] + u * w[1:2] + up(u, 1) * w[2:3])[ctr]
    for k, ref in enumerate((v_ref, x1_ref, x2_ref)):
        a = uc[:, k * W_MIX:(k + 1) * W_MIX]
        if n1_major:
            a = jnp.swapaxes(a.reshape(ref.shape[1], ref.shape[0], W_MIX), 0, 1)
        ref[...] = a

    q = jnp.dot(hb, win_ref[:, OFF_POOL:OFF_POOL + COL_POOL], preferred_element_type=f32)
    s2 = q + down(q, 1)
    s4 = s2 + down(s2, 2)
    s8 = s4 + down(s4, 4)
    s16 = s8 + down(s8, 8)
    tpos = i * tm + lax.broadcasted_iota(jnp.int32, (tm, 1), 0)
    pooled = []
    for g, (win, ssum) in enumerate(zip(POOL_WINDOWS, (s2, s4, s8, s16))):
        lo = win // 2
        hi = win - 1 - lo
        lanes = slice(g * POOL_GROUP, (g + 1) * POOL_GROUP)
        ws = ssum[:, lanes]
        if hi > 0:
            ws = up(ws, hi)
        cnt = (jnp.minimum(tpos + hi + 1, seq_len) - jnp.maximum(tpos - lo, 0)).astype(f32)
        p = ws[ctr] / cnt - q[ctr, lanes]
        pooled.append(jnp.dot(p.astype(bf16), pw_ref[g], preferred_element_type=f32))
    yb_in = jnp.concatenate(pooled, axis=1) * ps_ref[...]
    yb = jnp.dot(yb_in.astype(bf16), wbb_ref[...], preferred_element_type=f32)

    us = jnp.dot(hb, win_ref[:, OFF_SC:OFF_SC + COL_SC], preferred_element_type=f32)
    cx = us[:, W_MIX:2 * W_MIX] * us[:, 2 * W_MIX:3 * W_MIX]
    sw = scw_ref[...]
    dw = down(cx, 1) * sw[0:1] + cx * sw[1:2] + up(cx, 1) * sw[2:3]
    sc_out = (us[:, 0:W_MIX] * dw)[ctr]
    yc = jnp.dot(sc_out.astype(bf16), wbc_ref[...], preferred_element_type=f32)

    gt = 0.5 * jnp.tanh(0.5 * jnp.dot(hcb, win_ref[:, OFF_GATE:PROJ_COLS], preferred_element_type=f32)) + 0.5
    g0_ref[...] = gt[:, 0:D_MODEL]
    rest_ref[...] = gt[:, D_MODEL:2 * D_MODEL] * yb + gt[:, 2 * D_MODEL:3 * D_MODEL] * yc


def _proj_call(x, sh, sc, g1, win, hyw, pw, ps, scw, wbb, wbc, tm):
    B, L, D = x.shape
    nt = L // tm
    hb = tm // HALO
    row = lambda b, i: (b, i, 0)
    vec = lambda b, i: (b, 0, 0)
    n1c = 2 * L // DFT_N2
    n1_major = tm % n1c == 0 and (tm // n1c) % 8 == 0
    if n1_major:
        out_w = jax.ShapeDtypeStruct((B, n1c, L // n1c, W_MIX), f32)
        spec_w = pl.BlockSpec((None, n1c, tm // n1c, W_MIX), lambda b, i: (b, 0, i, 0))
    else:
        out_w = jax.ShapeDtypeStruct((B, L, W_MIX), f32)
        spec_w = pl.BlockSpec((None, tm, W_MIX), row)
    out_d = jax.ShapeDtypeStruct((B, L, D), f32)
    return pl.pallas_call(
        functools.partial(_proj_body, tm=tm, seq_len=L, n1_major=n1_major),
        grid=(B, nt),
        in_specs=[
            pl.BlockSpec((None, tm, D), row),
            pl.BlockSpec((None, HALO, D), lambda b, i: (b, jnp.maximum(i * hb - 1, 0), 0)),
            pl.BlockSpec((None, HALO, D), lambda b, i: (b, jnp.minimum((i + 1) * hb, L // HALO - 1), 0)),
            pl.BlockSpec((None, 1, D), vec),
            pl.BlockSpec((None, 1, D), vec),
            _const_spec((1, D)),
            _const_spec((D, PROJ_COLS)),
            _const_spec((3, COL_HY)),
            _const_spec((len(POOL_WINDOWS), POOL_GROUP, POOL_GROUP)),
            _const_spec((1, W_MIX)),
            _const_spec((3, W_MIX)),
            _const_spec((W_MIX, D)),
            _const_spec((W_MIX, D)),
        ],
        out_specs=[spec_w] * 3 + [pl.BlockSpec((None, tm, D), row)] * 2,
        out_shape=[out_w, out_w, out_w, out_d, out_d],
        compiler_params=_cparams(("parallel", "arbitrary")),
        name="proj",
    )(x, x, x, sh, sc, g1, win, hyw, pw, ps, scw, wbb, wbc)


def _filter_body(ca_ref, sa_ref, cb_ref, sb_ref, w1_ref, b1_ref, w2_ref, b2_ref, fr_ref, wo0_ref, wo1_ref, dl_ref, g_ref,
                 o_ref, asum_ref, h_s, *, rpb, seq_len, n1c):
    j = pl.program_id(0)
    s = pl.program_id(1)
    L = seq_len
    n2 = lax.broadcasted_iota(jnp.int32, (DFT_N2, 1), 0)
    fwd = n2 < DFT_ROWS

    def slot(r):
        pos = (j * rpb + r) + n1c * n2
        return pos, jnp.where(fwd, pos, 2 * L - pos).astype(f32)

    @pl.when(s == 0)
    def _():
        lane = lax.broadcasted_iota(jnp.int32, (DFT_N2, LANES), 1)
        fr = fr_ref[...]
        cb, sb = cb_ref[...], sb_ref[...]
        zs = []
        for r in range(rpb):
            _, lag = slot(r)
            ca = ca_ref[r:r + 1, :]
            sa = jnp.where(fwd, sa_ref[r:r + 1, :], -sa_ref[r:r + 1, :])
            cos_t = ca * cb - sa * sb
            sin_t = sa * cb + ca * sb
            z = jnp.where(lane == 0, lag / (L - 1), jnp.where(lane <= HY_BANDS, cos_t, -sin_t))
            zs.append(jnp.concatenate([z[:DFT_ROWS], z[DFT_ROWS:]], axis=1))
        zz = jnp.concatenate(zs, axis=0)
        h = jnp.sin(fr * (jnp.dot(zz, w1_ref[...], preferred_element_type=f32, precision=HIGHEST) + b1_ref[...]))
        h_s[...] = jnp.sin(fr * (jnp.dot(h, w2_ref[...], preferred_element_type=f32, precision=HIGHEST) + b2_ref[...]))

    hb = h_s[...].astype(bf16)
    ho_f = jnp.dot(hb, wo0_ref[...], preferred_element_type=f32)
    ho_b = jnp.dot(hb, wo1_ref[...], preferred_element_type=f32)
    asum = jnp.zeros(asum_ref.shape[1:], f32)
    bs = []
    for r in range(rpb):
        pos, lag = slot(r)
        rows = slice(r * DFT_ROWS, (r + 1) * DFT_ROWS)
        ho = jnp.concatenate([ho_f[rows], ho_b[rows]], axis=0)
        k = jnp.where(pos == L, 0.0, ho * jnp.exp(-(lag / (L - 1)) * dl_ref[...]))
        asum = asum + jnp.sum(jnp.abs(k), axis=0, keepdims=True)
        bs.append(jnp.dot(g_ref[r], k.astype(bf16), preferred_element_type=f32))
    o_ref[...] = jnp.swapaxes(jnp.stack(bs, axis=0), 0, 1).reshape(o_ref.shape).astype(bf16)

    @pl.when(j == 0)
    def _():
        asum_ref[s] = asum

    @pl.when(j > 0)
    def _():
        asum_ref[s] += asum


def _filter_tables(L):
    n1c = 2 * L // DFT_N2
    bands = jnp.linspace(1e-4, HY_BANDS - 1, HY_BANDS, dtype=f32)
    brow = jnp.zeros((LANES,), f32).at[1:1 + HY_BANDS].set(bands).at[1 + HY_BANDS:1 + 2 * HY_BANDS].set(bands)
    used = (jnp.arange(LANES) >= 1) & (jnp.arange(LANES) <= 2 * HY_BANDS)
    n2 = jnp.arange(DFT_N2)
    part_a = jnp.arange(n1c).astype(f32)
    part_b = (n1c * jnp.where(n2 < DFT_ROWS, n2, DFT_N2 - n2)).astype(f32)

    def cs(part):
        ang = (2 * math.pi / L) * part[:, None] * brow[None, :]
        return jnp.where(used, jnp.cos(ang), 0.0), jnp.where(used, jnp.sin(ang), 0.0)

    return cs(part_a) + cs(part_b)


def _filter_call(L, g_fwd, w1p, b1p, w2p, b2p, frp, wo_f, wo_b, dl_row):
    n1c = 2 * L // DFT_N2
    rpb = min(DFT_RPB, n1c)
    cw = HY_ORDER * W_MIX
    cs = DFT_SLAB
    ns = cw // cs
    ca, sa, cb, sb = _filter_tables(L)
    kh = g_fwd.shape[1] // 2
    bs, asum = pl.pallas_call(
        functools.partial(_filter_body, rpb=rpb, seq_len=L, n1c=n1c),
        grid=(n1c // rpb, ns),
        in_specs=[pl.BlockSpec((rpb, LANES), lambda j, s: (j, 0)), pl.BlockSpec((rpb, LANES), lambda j, s: (j, 0)),
                  _const_spec((DFT_N2, LANES)), _const_spec((DFT_N2, LANES)),
                  _const_spec((2 * LANES, LANES)), _const_spec((1, LANES)),
                  _const_spec((LANES, LANES)), _const_spec((1, LANES)), _const_spec((1, LANES)),
                  pl.BlockSpec((LANES, cs), lambda j, s: (0, s)),
                  pl.BlockSpec((LANES, cs), lambda j, s: (0, s)),
                  pl.BlockSpec((1, cs), lambda j, s: (0, s)),
                  pl.BlockSpec((rpb, 2 * kh, DFT_N2), lambda j, s: (j, 0, 0))],
        out_specs=[pl.BlockSpec((2, kh, rpb, cs), lambda j, s: (0, 0, j, s)),
                   pl.BlockSpec((ns, 1, cs), lambda j, s: (0, 0, 0))],
        out_shape=[jax.ShapeDtypeStruct((2, kh, n1c, cw), bf16), jax.ShapeDtypeStruct((ns, 1, cs), f32)],
        scratch_shapes=[pltpu.VMEM((rpb * DFT_ROWS, LANES), f32)],
        compiler_params=_cparams(("arbitrary", "arbitrary")),
        name="filt",
    )(ca, sa, cb, sb, w1p, b1p, w2p, b2p, frp, wo_f, wo_b, dl_row, g_fwd)
    return bs, asum.reshape(1, cw)


def _dft_half_rows(n1c):
    step = max(8, DFT_ROWS // n1c)
    return -(-(DFT_ROWS + 1) // step) * step


def _dft_tables(L):
    n = 2 * L
    n1c = n // DFT_N2
    kh = _dft_half_rows(n1c)
    k2 = jnp.arange(DFT_N2, dtype=jnp.int32)
    tw_ang = ((jnp.arange(n1c, dtype=jnp.int32)[:, None] * k2[None, :]) % n).astype(f32) * (2 * math.pi / n)
    f_ang = ((k2[:, None] * k2[None, :]) % DFT_N2).astype(f32) * (2 * math.pi / DFT_N2)
    twr, twi = jnp.cos(tw_ang)[:, :kh, None], -jnp.sin(tw_ang)[:, :kh, None]
    fr, fi = jnp.cos(f_ang)[None, :kh], -jnp.sin(f_ang)[None, :kh]
    gr = twr * fr - twi * fi
    gi = twr * fi + twi * fr
    g_fwd = jnp.concatenate([gr, gi], axis=1)
    kk = jnp.arange(kh)
    wgt = jnp.where(kk > DFT_ROWS, 0.0, jnp.where((kk == 0) | (kk == DFT_ROWS), 1.0, 2.0)) * (1.0 / n)
    twr_t, twi_t = jnp.cos(tw_ang)[:, None, :kh] * wgt, -jnp.sin(tw_ang)[:, None, :kh] * wgt
    fr_t = jnp.cos(f_ang)[None, :DFT_ROWS, :kh]
    fi_t = -jnp.sin(f_ang)[None, :DFT_ROWS, :kh]
    g_inv = jnp.concatenate([twr_t * fr_t - twi_t * fi_t, twr_t * fi_t + twi_t * fr_t], axis=2)
    a = jnp.arange(n1c, dtype=jnp.int32)
    s_ang = ((a[:, None] * a[None, :]) % n1c).astype(f32) * (2 * math.pi / n1c)
    eye = jnp.eye(DFT_ROWS // n1c, dtype=f32)
    sr = jnp.kron(eye, jnp.cos(s_ang))
    si = jnp.kron(eye, -jnp.sin(s_ang))
    m_fwd = jnp.block([[sr, -si], [si, sr]])
    m_inv = jnp.block([[sr, si], [-si, sr]])
    return dict(g_fwd=g_fwd.astype(bf16), g_inv=g_inv.astype(bf16), m_fwd=m_fwd.astype(bf16),
                m_inv=m_inv.astype(bf16), n1=n1c, kh=kh)


def _n1_major(a):
    return jnp.swapaxes(a, 0, 1)


def _n1_minor(mats, shape):
    return jnp.swapaxes(jnp.stack(mats, axis=0), 0, 1).reshape(shape)


def _time_spec(rpb, cs, n1_major):
    if n1_major:
        return pl.BlockSpec((None, rpb, DFT_ROWS, cs), lambda b, j, s: (b, j, 0, s))
    return pl.BlockSpec((None, DFT_ROWS, rpb, cs), lambda b, j, s: (b, 0, j, s))


def _fft1_body(x_ref, g_ref, o_ref, *, rpb, n1_major):
    x = x_ref[...] if n1_major else _n1_major(x_ref[...])
    bs = [jnp.dot(g_ref[r], x[r].astype(bf16), preferred_element_type=f32) for r in range(rpb)]
    o_ref[...] = _n1_minor(bs, o_ref.shape).astype(bf16)


def _fft1_call(x4, g_fwd, n1_major):
    B, C = x4.shape[0], x4.shape[3]
    n1c = g_fwd.shape[0]
    rpb = min(DFT_RPB, n1c)
    cs = DFT_SLAB
    kh = g_fwd.shape[1] // 2
    return pl.pallas_call(
        functools.partial(_fft1_body, rpb=rpb, n1_major=n1_major),
        grid=(B, n1c // rpb, C // cs),
        in_specs=[_time_spec(rpb, cs, n1_major),
                  pl.BlockSpec((rpb, 2 * kh, DFT_ROWS), lambda b, j, s: (j, 0, 0))],
        out_specs=pl.BlockSpec((None, 2, kh, rpb, cs), lambda b, j, s: (b, 0, 0, j, s)),
        out_shape=jax.ShapeDtypeStruct((B, 2, kh, n1c, C), bf16),
        compiler_params=_cparams(("parallel", "arbitrary", "arbitrary")),
        name="fft1",
    )(x4, g_fwd)


def _fft2_body(b_ref, kb_ref, asum_ref, mf_ref, mi_ref, o_ref, k_s, *, nsub):
    def block_rows(i):
        return slice(i * DFT_ROWS, (i + 1) * DFT_ROWS)

    def stacked(ref, rows):
        return jnp.concatenate([ref[0, rows, :], ref[1, rows, :]], axis=0)

    @pl.when(pl.program_id(1) == 0)
    def _():
        inv = 1.0 / asum_ref[...]
        for i in range(nsub):
            rows = block_rows(i)
            ks = jnp.dot(mf_ref[...], stacked(kb_ref, rows), preferred_element_type=f32)
            k_s[0, rows, :] = ks[:DFT_ROWS] * inv
            k_s[1, rows, :] = ks[DFT_ROWS:] * inv

    for i in range(nsub):
        rows = block_rows(i)
        xs = jnp.dot(mf_ref[...], stacked(b_ref, rows), preferred_element_type=f32)
        xr, xi = xs[:DFT_ROWS], xs[DFT_ROWS:]
        kr, ki = k_s[0, rows, :], k_s[1, rows, :]
        ys = jnp.concatenate([xr * kr - xi * ki, xr * ki + xi * kr], axis=0).astype(bf16)
        cs = jnp.dot(mi_ref[...], ys, preferred_element_type=f32)
        o_ref[0, rows, :] = cs[:DFT_ROWS].astype(bf16)
        o_ref[1, rows, :] = cs[DFT_ROWS:].astype(bf16)


def _fft2_call(bs, kb, asum, order, m_fwd, m_inv):
    B, _, n, C = bs.shape
    blocks = n // DFT_ROWS
    per_step = max(d for d in range(1, min(DFT_STAGE2_BLOCKS, blocks) + 1) if blocks % d == 0)
    rb = per_step * DFT_ROWS
    blk = pl.BlockSpec((None, 2, rb, C), lambda j, b: (b, 0, j, 0))
    return pl.pallas_call(
        functools.partial(_fft2_body, nsub=per_step),
        grid=(n // rb, B),
        in_specs=[blk,
                  pl.BlockSpec((2, rb, C), lambda j, b: (0, j, order)),
                  pl.BlockSpec((1, C), lambda j, b: (0, order)),
                  _const_spec((2 * DFT_ROWS, 2 * DFT_ROWS)),
                  _const_spec((2 * DFT_ROWS, 2 * DFT_ROWS))],
        out_specs=blk,
        out_shape=jax.ShapeDtypeStruct(bs.shape, bf16),
        scratch_shapes=[pltpu.VMEM((2, rb, C), f32)],
        compiler_params=_cparams(("arbitrary", "arbitrary")),
        name="fft2",
    )(bs, kb, asum, m_fwd, m_inv)


def _fft3_body(c_ref, gi_ref, gate_ref, prev_ref, sk_ref, *rest, rpb, fuse_next, in_major, out_major):
    if fuse_next:
        gf_ref, z_ref, b_ref = rest
    else:
        (z_ref,) = rest
    cs = c_ref.shape[-1]
    c = _n1_major(c_ref[...].astype(f32).reshape(2 * c_ref.shape[1], rpb, cs))
    gate = gate_ref[...] if in_major else _n1_major(gate_ref[...])
    prev = prev_ref[...] if in_major else _n1_major(prev_ref[...])
    sk = sk_ref[...]
    zs, bs = [], []
    for r in range(rpb):
        y = jnp.dot(gi_ref[r], c[r].astype(bf16), preferred_element_type=f32)
        z = gate[r] * (y + sk * prev[r])
        zs.append(z)
        if fuse_next:
            bs.append(jnp.dot(gf_ref[r], z.astype(bf16), preferred_element_type=f32))
    z_ref[...] = jnp.stack(zs, axis=0) if out_major else _n1_minor(zs, z_ref.shape)
    if fuse_next:
        b_ref[...] = _n1_minor(bs, b_ref.shape).astype(bf16)


def _fft3_call(cs5, g_inv, gate, prev, sk_row, in_major, out_major, g_fwd=None):
    B, _, kh, n1c, C = cs5.shape
    rpb = min(DFT_RPB, n1c)
    cs = DFT_SLAB
    tblk = _time_spec(rpb, cs, in_major)
    sblk = pl.BlockSpec((None, 2, kh, rpb, cs), lambda b, j, s: (b, 0, 0, j, s))
    in_specs = [sblk, pl.BlockSpec((rpb, DFT_ROWS, 2 * kh), lambda b, j, s: (j, 0, 0)), tblk, tblk,
                pl.BlockSpec((1, cs), lambda b, j, s: (0, s))]
    args = [cs5, g_inv, gate, prev, sk_row]
    out_specs = [_time_spec(rpb, cs, out_major)]
    out_shape = [jax.ShapeDtypeStruct((B, n1c, DFT_ROWS, C) if out_major else (B, DFT_ROWS, n1c, C), f32)]
    fuse_next = g_fwd is not None
    if fuse_next:
        in_specs.append(pl.BlockSpec((rpb, 2 * kh, DFT_ROWS), lambda b, j, s: (j, 0, 0)))
        args.append(g_fwd)
        out_specs.append(sblk)
        out_shape.append(jax.ShapeDtypeStruct(cs5.shape, bf16))
    return pl.pallas_call(
        functools.partial(_fft3_body, rpb=rpb, fuse_next=fuse_next, in_major=in_major, out_major=out_major),
        grid=(B, n1c // rpb, C // cs),
        in_specs=in_specs,
        out_specs=out_specs,
        out_shape=out_shape,
        compiler_params=_cparams(("parallel", "arbitrary", "arbitrary")),
        name="fft3_next" if fuse_next else "fft3",
    )(*args)


def _filter_spectrum(L, tabs, filt_params):
    bs, asum = _filter_call(L, tabs["g_fwd"], *filt_params)
    return bs.reshape(2, tabs["kh"] * tabs["n1"], bs.shape[-1]), asum


def _hyena_conv(v, x1, x2, ksp, skip, tabs):
    n1c, kh = tabs["n1"], tabs["kh"]
    n1_major = v.ndim == 4
    B, C = v.shape[0], v.shape[-1]
    n = kh * n1c
    kb, asum = ksp
    if not n1_major:
        v, x1, x2 = (a.reshape(B, DFT_ROWS, n1c, C) for a in (v, x1, x2))
    s5 = (B, 2, kh, n1c, C)
    bs = _fft1_call(v, tabs["g_fwd"], n1_major)
    cs = _fft2_call(bs.reshape(B, 2, n, C), kb, asum, 0, tabs["m_fwd"], tabs["m_inv"])
    z1, bs = _fft3_call(cs.reshape(s5), tabs["g_inv"], x1, v, skip[0:1], n1_major, n1_major, tabs["g_fwd"])
    cs = _fft2_call(bs.reshape(B, 2, n, C), kb, asum, 1, tabs["m_fwd"], tabs["m_inv"])
    (z2,) = _fft3_call(cs.reshape(s5), tabs["g_inv"], x2, z1, skip[1:2], n1_major, False)
    return z2.reshape(B, DFT_ROWS * n1c, C)


def _route(r):
    lane = lax.broadcasted_iota(jnp.int32, r.shape, 1)
    ninf = jnp.float32(-jnp.inf)
    big = jnp.int32(1 << 20)
    is_g = lane < N_GROUPS
    gmax = jnp.max(jnp.where(is_g, r, ninf), axis=-1, keepdims=True)
    gidx = jnp.min(jnp.where(jnp.logical_and(is_g, r == gmax), lane, big), axis=-1, keepdims=True)
    gw = 1.0 / jnp.sum(jnp.where(is_g, jnp.exp(r - gmax), 0.0), axis=-1, keepdims=True)
    e_lane = lane - N_GROUPS
    sel = jnp.logical_and(jnp.logical_and(e_lane >= 0, e_lane < N_EXPERTS), (e_lane >> 2) == gidx)
    le = jnp.where(sel, r, ninf)
    m1 = jnp.max(le, axis=-1, keepdims=True)
    i1 = jnp.min(jnp.where(le == m1, lane, big), axis=-1, keepdims=True)
    le2 = jnp.where(lane == i1, ninf, le)
    m2 = jnp.max(le2, axis=-1, keepdims=True)
    i2 = jnp.min(jnp.where(le2 == m2, lane, big), axis=-1, keepdims=True)
    e2 = jnp.exp(m2 - m1)
    den = 1.0 + e2
    comb = jnp.where(lane == i1, gw / den, jnp.where(lane == i2, gw * e2 / den, 0.0))
    return comb, gidx


def _mix_body(x_ref, z_ref, g0_ref, rest_ref, ga1_ref, sh2_ref, sc2_ref, n2g_ref, wba_ref, wout_ref, wr_ref, tri_ref,
              xo_ref, hs_ref, combs_ref, pmt_ref, cnt_ref):
    ya = jnp.dot(z_ref[...].astype(bf16), wba_ref[...], preferred_element_type=f32)
    merged = g0_ref[...] * ya + rest_ref[...]
    xo = x_ref[...] + ga1_ref[...] * jnp.dot(merged.astype(bf16), wout_ref[...], preferred_element_type=f32)
    xo_ref[...] = xo
    ms = jnp.mean(xo * xo, axis=-1, keepdims=True)
    h2 = xo * lax.rsqrt(ms + EPS) * n2g_ref[...]
    h2 = h2 * (1.0 + sc2_ref[...]) + sh2_ref[...]
    h_hi = h2.astype(bf16)
    h_lo = (h2 - h_hi.astype(f32)).astype(bf16)
    p_hi = jnp.dot(h_hi, wr_ref[...], preferred_element_type=f32)
    p_lo = jnp.dot(h_lo, wr_ref[:, 0:LANES], preferred_element_type=f32)
    comb, gidx = _route(p_hi[:, 0:LANES] + p_hi[:, LANES:2 * LANES] + p_lo)

    tm = comb.shape[0]
    lane = lax.broadcasted_iota(jnp.int32, comb.shape, 1)
    onehot = (lane == gidx).astype(f32)
    cum = jnp.dot(tri_ref[...], onehot.astype(bf16), preferred_element_type=f32)
    tot8 = cum[tm - 8:tm, :]
    off8 = pltpu.roll(tot8, 1, 1) + pltpu.roll(tot8, 2, 1) + pltpu.roll(tot8, 3, 1)
    rank = jnp.sum(onehot * (off8[7:8, :] + cum - 1.0), axis=-1, keepdims=True)
    slot = lax.broadcasted_iota(jnp.int32, (tm, tm), 1).astype(f32)
    pmt = (slot == rank).astype(bf16)
    pmt_ref[...] = pmt
    both = jnp.concatenate([h_hi, comb.astype(bf16)], axis=1)
    srt = lax.dot_general(pmt, both, (((0,), (0,)), ((), ())), preferred_element_type=f32).astype(bf16)
    d = h2.shape[1]
    hs_ref[...] = srt[:, 0:d]
    combs_ref[...] = srt[:, d:d + LANES]
    cnt_ref[...] = tot8[7:8, :].astype(jnp.int32)


def _mix_call(x, z, g0, rest, ga1, sh2, sc2, n2g, wba, wout, wr, tri, tm):
    B, L, D = x.shape
    nt = L // tm
    row = lambda b, i: (b, i, 0)
    vspec = pl.BlockSpec((None, 1, D), lambda b, i: (b, 0, 0))
    return pl.pallas_call(
        _mix_body,
        grid=(B, nt),
        in_specs=[
            pl.BlockSpec((None, tm, D), row),
            pl.BlockSpec((None, tm, W_MIX), row),
            pl.BlockSpec((None, tm, D), row),
            pl.BlockSpec((None, tm, D), row),
            vspec, vspec, vspec,
            _const_spec((1, D)),
            _const_spec((W_MIX, D)),
            _const_spec((D, D)),
            _const_spec((D, 2 * LANES)),
            _const_spec((tm, tm)),
        ],
        out_specs=[pl.BlockSpec((None, tm, D), row), pl.BlockSpec((None, tm, D), row),
                   pl.BlockSpec((None, tm, LANES), row), pl.BlockSpec((None, tm, tm), row),
                   pl.BlockSpec((None, None, 1, LANES), lambda b, i: (b, i, 0, 0))],
        out_shape=[jax.ShapeDtypeStruct((B, L, D), f32), jax.ShapeDtypeStruct((B, L, D), bf16),
                   jax.ShapeDtypeStruct((B, L, LANES), bf16), jax.ShapeDtypeStruct((B, L, tm), bf16),
                   jax.ShapeDtypeStruct((B, nt, 1, LANES), jnp.int32)],
        compiler_params=_cparams(("parallel", "arbitrary")),
        name="mix",
    )(x, z, g0, rest, ga1, sh2, sc2, n2g, wba, wout, wr, tri)


def _experts_body(cnt_ref, hs_ref, combs_ref, pmt_ref, xo_ref, ga2_ref, ex_ref, w1_ref, w3_ref, w2_ref, fg_ref, o_ref,
                  acc_s, *, tm, sub, final_norm):
    b, sup, g = pl.program_id(0), pl.program_id(1), pl.program_id(2)
    nchunk = tm // MOE_CHUNK

    @pl.when(g == 0)
    def _():
        acc_s[...] = jnp.zeros_like(acc_s)

    def tile(t, carry):
        base = ((b * pl.num_programs(1) + sup) * sub + t) * N_GROUPS
        lo = jnp.int32(0)
        for gg in range(N_GROUPS - 1):
            lo = lo + jnp.where(gg < g, cnt_ref[base + gg], 0)
        hi = lo + cnt_ref[base + g]

        def run(start, nrows):
            rows = pl.ds(pl.multiple_of(t * tm + start, MOE_CHUNK), nrows)
            h = hs_ref[rows, :]
            a = jnp.dot(h, w1_ref[...], preferred_element_type=f32)
            u = jnp.dot(h, w3_ref[...], preferred_element_type=f32)
            cw = jnp.dot(combs_ref[rows, :], ex_ref[...], preferred_element_type=f32)
            silu = 0.5 * a * (jnp.tanh(0.5 * a) + 1.0)
            acc_s[rows, :] += jnp.dot((silu * u * cw).astype(bf16), w2_ref[...], preferred_element_type=f32)

        win = jnp.minimum(lo - (lo & (MOE_CHUNK - 1)), tm - MOE_WINDOW)

        @pl.when(hi > lo)
        def _():
            run(win, MOE_WINDOW)

        for c in range(nchunk):
            @pl.when(jnp.logical_and(c * MOE_CHUNK >= win + MOE_WINDOW, hi > c * MOE_CHUNK))
            def _():
                run(c * MOE_CHUNK, MOE_CHUNK)
        return carry

    lax.fori_loop(0, sub, tile, 0)

    @pl.when(g == pl.num_programs(2) - 1)
    def _():
        for t in range(sub):
            rows = slice(t * tm, (t + 1) * tm)
            y2 = jnp.dot(pmt_ref[rows, :], acc_s[rows, :].astype(bf16), preferred_element_type=f32)
            y = xo_ref[rows, :] + ga2_ref[...] * y2
            if final_norm:
                ms = jnp.mean(y * y, axis=-1, keepdims=True)
                y = y * lax.rsqrt(ms + EPS) * fg_ref[...]
            o_ref[rows, :] = y


def _experts_call(cnt, hs, combs, pmt, xo, ga2, ex, w1, w3, w2, fg, tm, final_norm):
    B, L, D = hs.shape
    sub = min(MOE_SUB, L // tm)
    rows = sub * tm
    blk = lambda b, s, g, cnt: (b, s, 0)
    return pl.pallas_call(
        functools.partial(_experts_body, tm=tm, sub=sub, final_norm=final_norm),
        grid_spec=pltpu.PrefetchScalarGridSpec(
            num_scalar_prefetch=1,
            grid=(B, L // rows, N_GROUPS),
            in_specs=[
                pl.BlockSpec((None, rows, D), blk),
                pl.BlockSpec((None, rows, LANES), blk),
                pl.BlockSpec((None, rows, tm), blk),
                pl.BlockSpec((None, rows, D), blk),
                pl.BlockSpec((None, 1, D), lambda b, s, g, cnt: (b, 0, 0)),
                pl.BlockSpec((None, LANES, GROUP_HID), lambda b, s, g, cnt: (g, 0, 0)),
                pl.BlockSpec((D, GROUP_HID), lambda b, s, g, cnt: (0, g)),
                pl.BlockSpec((D, GROUP_HID), lambda b, s, g, cnt: (0, g)),
                pl.BlockSpec((GROUP_HID, D), lambda b, s, g, cnt: (g, 0)),
                pl.BlockSpec((1, D), lambda b, s, g, cnt: (0, 0)),
            ],
            out_specs=pl.BlockSpec((None, rows, D), blk),
            scratch_shapes=[pltpu.VMEM((rows, D), f32)],
        ),
        out_shape=jax.ShapeDtypeStruct((B, L, D), f32),
        compiler_params=_cparams(("arbitrary", "arbitrary", "arbitrary")),
        name="experts",
    )(cnt, hs, combs, pmt, xo, ga2, ex, w1, w3, w2, fg)


def _pad_to(a, shape):
    return jnp.pad(a, [(0, s - d) for d, s in zip(a.shape, shape)])


def _prep_layer(l, p):
    max_decay = math.log(HY_TARGET) / HY_FAST_DECAY
    min_decay = math.log(HY_TARGET) / HY_SLOW_DECAY
    deltas = jnp.abs(jnp.linspace(min_decay, max_decay, W_MIX, dtype=f32))
    router = jnp.concatenate([p["router_g"][l], p["router_e"][l]], axis=1)
    lanes = jnp.arange(LANES)[None, :, None]
    cols = jnp.arange(GROUP_HID)[None, None, :]
    grp = jnp.arange(N_GROUPS)[:, None, None]
    expand = (lanes == N_GROUPS + EXP_PER_GROUP * grp + cols // D_EXPERT).astype(bf16)
    router = _pad_to(router, (D_MODEL, LANES))
    router_hi = router.astype(bf16)
    router_lo = (router - router_hi.astype(f32)).astype(bf16)
    zh = jnp.zeros((HY_HID, HY_HID), f32)
    w1 = _pad_to(p["hy_w1"][l], (LANES, HY_HID))
    zw1 = jnp.zeros_like(w1)
    w1_pair = jnp.block([[w1, zw1], [zw1, w1]])
    w2_pair = jnp.block([[p["hy_w2"][l], zh], [zh, p["hy_w2"][l]]])
    pair = lambda a: jnp.concatenate([a, a])[None]
    cw = HY_ORDER * W_MIX
    wo = p["hy_w_out"][l]
    zwo = jnp.zeros((HY_HID, cw), f32)
    wo_f = jnp.concatenate([wo[:, :cw], zwo], axis=0).astype(bf16)
    wo_b = jnp.concatenate([zwo, wo[:, cw:]], axis=0).astype(bf16)
    return dict(
        norm1_g=p["norm1_g"][l][None], norm2_g=p["norm2_g"][l][None],
        w_in=p["w_in"][l].astype(bf16), hy_conv_w=p["hy_conv_w"][l], hy_skip=p["hy_skip"][l],
        pool_w=p["pool_w"][l].astype(bf16), pool_scale=p["pool_scale"][l][None], sc_conv_w=p["sc_conv_w"][l],
        w_br_a=p["w_br_a"][l].astype(bf16), w_br_b=p["w_br_b"][l].astype(bf16), w_br_c=p["w_br_c"][l].astype(bf16),
        w_out=p["w_out"][l].astype(bf16),
        router=jnp.concatenate([router_hi, router_lo], axis=1), expand=expand,
        moe_w1=p["moe_w1"][l].astype(bf16), moe_w3=p["moe_w3"][l].astype(bf16), moe_w2=p["moe_w2"][l].astype(bf16),
        filt=(w1_pair, pair(p["hy_b1"][l]), w2_pair, pair(p["hy_b2"][l]), pair(p["hy_freq"][l]), wo_f, wo_b,
              pair(deltas)),
    )


def _tile(L, want):
    return want if L % want == 0 else L


def _encoder_layer(x, mod, lp, ksp, tabs, final_g, final_norm):
    B, L, D = x.shape
    sh1, sc1, ga1, sh2, sc2, ga2 = (m[:, None, :] for m in jnp.split(mod, 6, axis=-1))
    v, x1, x2, g0, rest = _proj_call(x, sh1, sc1, lp["norm1_g"], lp["w_in"], lp["hy_conv_w"], lp["pool_w"],
                                     lp["pool_scale"], lp["sc_conv_w"], lp["w_br_b"], lp["w_br_c"], _tile(L, PROJ_TILE))
    z = _hyena_conv(v, x1, x2, ksp, lp["hy_skip"], tabs)
    tm = _tile(L, MOE_TILE)
    tri = jnp.tri(tm, dtype=bf16)
    xo, hs, combs, pmt, cnt = _mix_call(x, z, g0, rest, ga1, sh2, sc2, lp["norm2_g"], lp["w_br_a"], lp["w_out"],
                                        lp["router"], tri, tm)
    return _experts_call(cnt[:, :, 0, :N_GROUPS].reshape(-1), hs, combs, pmt, xo, ga2, lp["expand"], lp["moe_w1"],
                         lp["moe_w3"], lp["moe_w2"], final_g, tm, final_norm)


def _forward(xs, cs, p, final_g):
    depth = p["w_in"].shape[0]
    nb = [c.shape[0] for c in cs]
    rows = -(-sum(nb) // 8) * 8
    c_all = _pad_to(jnp.concatenate(cs, axis=0), (rows, D_MODEL))
    lens = sorted({x.shape[1] for x in xs})
    tabs = {L: _dft_tables(L) for L in lens}
    fg = final_g[None]
    mods = _mod_call(c_all, p["ada_w"], p["ada_b"])
    for l in range(depth):
        lp = _prep_layer(l, p)
        mod = mods[l]
        ksp = {L: _filter_spectrum(L, tabs[L], lp["filt"]) for L in lens}
        off = 0
        out = []
        for x, n in zip(xs, nb):
            L = x.shape[1]
            out.append(_encoder_layer(x, mod[off:off + n], lp, ksp[L], tabs[L], fg, l == depth - 1))
            off += n
        xs = out
    return xs


def kernel(x_prompt, x_sample, c_prompt, c_sample, ada_w, ada_b, norm1_g, norm2_g, w_in, hy_conv_w, hy_skip, hy_w1, hy_b1, hy_w2, hy_b2, hy_w_out, hy_freq, pool_w, pool_scale, sc_conv_w, w_br_a, w_br_b, w_br_c, w_out, router_g, router_e, moe_w1, moe_w3, moe_w2, final_g):
    p = dict(ada_w=ada_w, ada_b=ada_b, norm1_g=norm1_g, norm2_g=norm2_g, w_in=w_in, hy_conv_w=hy_conv_w,
             hy_skip=hy_skip, hy_w1=hy_w1, hy_b1=hy_b1, hy_w2=hy_w2, hy_b2=hy_b2, hy_w_out=hy_w_out, hy_freq=hy_freq,
             pool_w=pool_w, pool_scale=pool_scale, sc_conv_w=sc_conv_w, w_br_a=w_br_a, w_br_b=w_br_b, w_br_c=w_br_c,
             w_out=w_out, router_g=router_g, router_e=router_e, moe_w1=moe_w1, moe_w3=moe_w3, moe_w2=moe_w2)
    y_prompt, y_sample = _forward([x_prompt, x_sample], [c_prompt, c_sample], p, final_g)
    return (y_prompt, y_sample)
```
